```python
import jax, jax.numpy as jnp
from jax import lax
import numpy as np

D_MODEL = 2048
BATCH = 8
SEQ = 2048
DEPTH = 4
DEC_BATCH = 16
DEC_SEQ = 16
PAST_LEN = 2048

CHUNK = 64
Q_BLOCK = 128
N_HEADS = 8
NOPE_DIM = 128
ROPE_DIM = 64
V_DIM = 128
QK_DIM = NOPE_DIM + ROPE_DIM
Q_LORA = 512
KV_LORA = 512
D_ATTN = N_HEADS * V_DIM
D_CONV = D_MODEL - D_ATTN
D_MIX = D_ATTN + D_CONV
CONV_W = 3
D_FF = 4 * D_MODEL
IN_COLS = Q_LORA + KV_LORA + ROPE_DIM + 3 * D_CONV
SPLITS = (Q_LORA, Q_LORA + KV_LORA, Q_LORA + KV_LORA + ROPE_DIM,
          Q_LORA + KV_LORA + ROPE_DIM + D_CONV, Q_LORA + KV_LORA + ROPE_DIM + 2 * D_CONV)
ROPE_BASE = 10000.0
EPS = 1e-6
ATTN_SCALE = QK_DIM ** -0.5

kernel_name = "hymba_mla_shortconv_adaln_stream_step"


def rms_norm(x, gain=None):
    xf = x.astype(jnp.float32)
    y = xf * lax.rsqrt(jnp.mean(xf * xf, axis=-1, keepdims=True) + EPS)
    if gain is not None:
        y = y * gain.astype(jnp.float32)
    return y.astype(x.dtype)


def rope_tables(pos):
    half = ROPE_DIM // 2
    inv = ROPE_BASE ** (-jnp.arange(half, dtype=jnp.float32) / half)
    ang = pos.astype(jnp.float32)[:, None] * inv[None, :]
    return jnp.cos(ang), jnp.sin(ang)


def rope(x, cos, sin):
    xf = x.astype(jnp.float32)
    x1, x2 = xf[..., :ROPE_DIM // 2], xf[..., ROPE_DIM // 2:]
    out = jnp.concatenate([x1 * cos - x2 * sin, x2 * cos + x1 * sin], axis=-1)
    return out.astype(x.dtype)


def modulation(c, w_ada, b_ada):
    m = jax.nn.silu(c) @ w_ada + b_ada
    return jnp.split(m[:, None, :], 6, axis=-1)


def expand_latent(c_kv, w_ukv):
    b, t, _ = c_kv.shape
    kv = (c_kv @ w_ukv).reshape(b, t, N_HEADS, NOPE_DIM + V_DIM)
    return kv[..., :NOPE_DIM], kv[..., NOPE_DIM:]


def attend(q_nope, q_pe, k_nope, k_pe, v, mask):
    s = (jnp.einsum('bqhd,bkhd->bhqk', q_nope, k_nope)
         + jnp.einsum('bqhr,bkr->bhqk', q_pe, k_pe)).astype(jnp.float32) * ATTN_SCALE
    if mask is not None:
        s = jnp.where(mask, s, -jnp.inf)
    p = jax.nn.softmax(s, axis=-1).astype(v.dtype)
    return jnp.einsum('bhqk,bkhd->bqhd', p, v)


def prompt_attention(q_nope, q_pe, k_nope, k_pe, v):
    b, s = q_nope.shape[:2]
    key_chunk = jnp.arange(s) // CHUNK

    def block(i):
        start = i * Q_BLOCK
        qn = lax.dynamic_slice_in_dim(q_nope, start, Q_BLOCK, axis=1)
        qp = lax.dynamic_slice_in_dim(q_pe, start, Q_BLOCK, axis=1)
        q_chunk = (start + jnp.arange(Q_BLOCK)) // CHUNK
        mask = key_chunk[None, :] <= q_chunk[:, None]
        return attend(qn, qp, k_nope, k_pe, v, mask[None, None])

    out = lax.map(block, jnp.arange(s // Q_BLOCK))
    return jnp.moveaxis(out, 0, 1).reshape(b, s, D_ATTN)


def short_conv(u, prev, w):
    t = u.shape[1]
    full = jnp.concatenate([prev.astype(u.dtype), u], axis=1)
    y = w[0] * full[:, 0:t]
    for k in range(1, CONV_W):
        y = y + w[k] * full[:, k:k + t]
    return y, full[:, t:]


def trunk_layer(x, c, cos, sin, past_ckv, past_kpe, conv_prev,
                w_ada, b_ada, w_in, q_norm, w_uq, kv_norm, w_ukv, conv_w,
                attn_out_norm, conv_out_norm, w_out, w_up, w_down):
    b, t, _ = x.shape
    sh_a, sc_a, g_a, sh_m, sc_m, g_m = modulation(c, w_ada, b_ada)
    h = rms_norm(x) * (1 + sc_a) + sh_a
    z = h @ w_in
    q_lat, kv_lat, k_pe, gate_b, gate_c, u = jnp.split(z, SPLITS, axis=-1)
    q = (rms_norm(q_lat, q_norm) @ w_uq).reshape(b, t, N_HEADS, QK_DIM)
    q_nope = q[..., :NOPE_DIM]
    q_pe = rope(q[..., NOPE_DIM:], cos[:, None, :], sin[:, None, :])
    c_kv = rms_norm(kv_lat, kv_norm)
    k_pe = rope(k_pe, cos, sin)
    if past_ckv is None:
        k_nope, v = expand_latent(c_kv, w_ukv)
        attn = prompt_attention(q_nope, q_pe, k_nope, k_pe, v)
    else:
        ckv_all = jnp.concatenate([past_ckv.astype(c_kv.dtype), c_kv], axis=1)
        kpe_all = jnp.concatenate([past_kpe.astype(k_pe.dtype), k_pe], axis=1)
        k_nope, v = expand_latent(ckv_all, w_ukv)
        attn = attend(q_nope, q_pe, k_nope, kpe_all, v, None).reshape(b, t, D_ATTN)
    conv_y, conv_state = short_conv(gate_c * u, conv_prev, conv_w)
    conv_o = gate_b * conv_y
    mix = jnp.concatenate([rms_norm(attn, attn_out_norm),
                           rms_norm(conv_o, conv_out_norm)], axis=-1) @ w_out
    x = x + g_a * mix
    h = rms_norm(x) * (1 + sc_m) + sh_m
    x = x + g_m * (jnp.square(jax.nn.relu(h @ w_up)) @ w_down)
    return x, c_kv, k_pe, conv_state


def setup_inputs(seed: int = 0) -> dict:
    key = jax.random.key(seed)
    ks = jax.random.split(key, 24)
    f32 = jnp.float32
    nrm = lambda k, shape, s: jax.random.normal(k, shape, f32) * s
    return {
        "x_prompt": nrm(ks[0], (BATCH, SEQ, D_MODEL), 1.0),
        "x_sample": nrm(ks[1], (DEC_BATCH, DEC_SEQ, D_MODEL), 1.0),
        "c_prompt": nrm(ks[2], (BATCH, D_MODEL), 1.0),
        "c_sample": nrm(ks[3], (DEC_BATCH, D_MODEL), 1.0),
        "cache_kv_latent": nrm(ks[4], (DEPTH, DEC_BATCH, PAST_LEN, KV_LORA), 1.0),
        "cache_k_rope": nrm(ks[5], (DEPTH, DEC_BATCH, PAST_LEN, ROPE_DIM), 1.0),
        "state_conv": nrm(ks[6], (DEPTH, DEC_BATCH, CONV_W - 1, D_CONV), 1.0),
        "w_ada": nrm(ks[7], (DEPTH, D_MODEL, 6 * D_MODEL), 0.3 * D_MODEL ** -0.5),
        "b_ada": nrm(ks[8], (DEPTH, 6 * D_MODEL), 0.01),
        "w_in": nrm(ks[9], (DEPTH, D_MODEL, IN_COLS), D_MODEL ** -0.5),
        "q_norm": 1.0 + nrm(ks[10], (DEPTH, Q_LORA), 0.01),
        "w_uq": nrm(ks[11], (DEPTH, Q_LORA, N_HEADS * QK_DIM), Q_LORA ** -0.5),
        "kv_norm": 1.0 + nrm(ks[12], (DEPTH, KV_LORA), 0.01),
        "w_ukv": nrm(ks[13], (DEPTH, KV_LORA, N_HEADS * (NOPE_DIM + V_DIM)), KV_LORA ** -0.5),
        "conv_w": nrm(ks[14], (DEPTH, CONV_W, D_CONV), CONV_W ** -0.5),
        "attn_out_norm": 1.0 + nrm(ks[15], (DEPTH, D_ATTN), 0.01),
        "conv_out_norm": 1.0 + nrm(ks[16], (DEPTH, D_CONV), 0.01),
        "w_out": nrm(ks[17], (DEPTH, D_MIX, D_MODEL), D_MIX ** -0.5),
        "w_up": nrm(ks[18], (DEPTH, D_MODEL, D_FF), D_MODEL ** -0.5),
        "w_down": nrm(ks[19], (DEPTH, D_FF, D_MODEL), D_FF ** -0.5),
        "final_norm": 1.0 + nrm(ks[20], (D_MODEL,), 0.01),
    }


def reference(x_prompt, x_sample, c_prompt, c_sample, cache_kv_latent, cache_k_rope, state_conv,
              w_ada, b_ada, w_in, q_norm, w_uq, kv_norm, w_ukv, conv_w,
              attn_out_norm, conv_out_norm, w_out, w_up, w_down, final_norm):
    b_p, s_p, _ = x_prompt.shape
    b_s, s_s, _ = x_sample.shape
    past_len = cache_kv_latent.shape[2]
    cos_p, sin_p = rope_tables(jnp.arange(s_p))
    cos_s, sin_s = rope_tables(past_len + jnp.arange(s_s))
    zero_conv = jnp.zeros((b_p, CONV_W - 1, D_CONV), x_prompt.dtype)

    xp, xs = x_prompt, x_sample
    ckv_p, kpe_p, conv_p, ckv_s, kpe_s, conv_s = [], [], [], [], [], []
    for l in range(DEPTH):
        params = (w_ada[l], b_ada[l], w_in[l], q_norm[l], w_uq[l], kv_norm[l], w_ukv[l],
                  conv_w[l], attn_out_norm[l], conv_out_norm[l], w_out[l], w_up[l], w_down[l])
        xp, a, k, cs = trunk_layer(xp, c_prompt, cos_p, sin_p, None, None, zero_conv, *params)
        ckv_p.append(a); kpe_p.append(k); conv_p.append(cs)
        xs, a, k, cs = trunk_layer(xs, c_sample, cos_s, sin_s, cache_kv_latent[l],
                                   cache_k_rope[l], state_conv[l], *params)
        ckv_s.append(a); kpe_s.append(k); conv_s.append(cs)

    y_prompt = rms_norm(xp, final_norm)
    y_sample = rms_norm(xs, final_norm)
    return (y_prompt, y_sample,
            jnp.stack(ckv_p), jnp.stack(kpe_p), jnp.stack(conv_p),
            jnp.stack(ckv_s), jnp.stack(kpe_s), jnp.stack(conv_s))
```

```python
import functools
import math

import jax
import jax.numpy as jnp
from jax import lax
from jax.experimental import pallas as pl
from jax.experimental.pallas import tpu as pltpu

F32 = jnp.float32
BF16 = jnp.bfloat16

CHUNK = 64
N_HEADS = 8
NOPE_DIM = 128
ROPE_DIM = 64
V_DIM = 128
QK_DIM = NOPE_DIM + ROPE_DIM
Q_LORA = 512
KV_LORA = 512
D_ATTN = N_HEADS * V_DIM
CONV_W = 3
ROPE_BASE = 10000.0
EPS = 1e-6
ATTN_SCALE = QK_DIM ** -0.5

LANES = 128
HEAD_W = 2 * LANES
ROPE_W = LANES
CONV_PAD = 8
Q_SCALE = ATTN_SCALE * math.log2(math.e)
NEG_BIG = -1e30
VMEM_LIMIT = 56 * 1024 * 1024


def _cparams(n_axes):
    return pltpu.CompilerParams(dimension_semantics=("arbitrary",) * n_axes,
                                vmem_limit_bytes=VMEM_LIMIT)


def _rms(x):
    return x * lax.rsqrt(jnp.mean(x * x, axis=-1, keepdims=True) + EPS)


def _dot(a, b):
    return jnp.dot(a, b, preferred_element_type=F32)


def _dot_t(a, b):
    return lax.dot_general(a, b, (((1,), (1,)), ((), ())), preferred_element_type=F32)


def _mod_kernel(c_ref, w_ref, b_ref, o_ref):
    c = c_ref[...]
    s = (c * jax.nn.sigmoid(c)).astype(BF16)
    o_ref[0] = _dot(s, w_ref[0].astype(BF16)) + b_ref[0]


def _modulation(c_all, w_ada, b_ada, tn=1024):
    depth, d, n = w_ada.shape
    r = c_all.shape[0]
    return pl.pallas_call(
        _mod_kernel,
        grid=(depth, n // tn),
        in_specs=[pl.BlockSpec((r, d), lambda l, j: (0, 0)),
                  pl.BlockSpec((1, d, tn), lambda l, j: (l, 0, j)),
                  pl.BlockSpec((1, 1, tn), lambda l, j: (l, 0, j))],
        out_specs=pl.BlockSpec((1, r, tn), lambda l, j: (l, 0, j)),
        out_shape=jax.ShapeDtypeStruct((depth, r, n), F32),
        compiler_params=_cparams(2),
        name="modulation",
    )(c_all, w_ada, b_ada.reshape(depth, 1, n))


def _rope128(p, cos, s1, s2):
    return p * cos + pltpu.roll(p, 96, 1) * s1 + pltpu.roll(p, 32, 1) * s2


def _attn_in_common(x_ref, sh_ref, sc_ref, win_ref, qn_ref, wuq_ref, kvn_ref,
                    cos_ref, s1_ref, s2_ref):
    x = x_ref[...]
    h = (_rms(x) * (1.0 + sc_ref[...]) + sh_ref[...]).astype(BF16)
    z = _dot(h, win_ref[...])
    cos, s1, s2 = cos_ref[...], s1_ref[...], s2_ref[...]
    qn = (_rms(z[:, :Q_LORA]) * qn_ref[...]).astype(BF16)
    q = _dot(qn, wuq_ref[...]) * Q_SCALE
    ckv = _rms(z[:, Q_LORA:Q_LORA + KV_LORA]) * kvn_ref[...]
    kp = _rope128(z[:, Q_LORA + KV_LORA:], cos, s1, s2)
    rope = functools.partial(_rope128, cos=cos, s1=s1, s2=s2)
    return q, ckv, kp, rope


def _attn_in_prompt_kernel(x_ref, sh_ref, sc_ref, win_ref, qn_ref, wuq_ref, kvn_ref,
                           cos_ref, s1_ref, s2_ref, wuk_ref, wuv_ref,
                           q_out, k_out, v_out, ckv_out, kpe_out):
    q, ckv, kp, rope = _attn_in_common(x_ref, sh_ref, sc_ref, win_ref, qn_ref, wuq_ref,
                                       kvn_ref, cos_ref, s1_ref, s2_ref)
    ckv_out[...] = ckv
    kpe_out[...] = kp[:, :ROPE_DIM]
    ckb = ckv.astype(BF16)
    kn = _dot(ckb, wuk_ref[...])
    v_out[...] = _dot(ckb, wuv_ref[...]).astype(BF16)
    kpb = kp.astype(BF16)
    for hd in range(N_HEADS):
        c0 = hd * HEAD_W
        q_out[:, c0:c0 + NOPE_DIM] = q[:, c0:c0 + NOPE_DIM].astype(BF16)
        q_out[:, c0 + NOPE_DIM:c0 + HEAD_W] = rope(q[:, c0 + NOPE_DIM:c0 + HEAD_W]).astype(BF16)
        k_out[:, c0:c0 + NOPE_DIM] = kn[:, hd * NOPE_DIM:(hd + 1) * NOPE_DIM].astype(BF16)
        k_out[:, c0 + NOPE_DIM:c0 + HEAD_W] = kpb


def _attn_in_sample_kernel(x_ref, sh_ref, sc_ref, win_ref, qn_ref, wuq_ref, kvn_ref,
                           cos_ref, s1_ref, s2_ref, wukt_ref,
                           qa_out, qpe_out, ckv_out, kpe_out, *, nb, t):
    q, ckv, kp, rope = _attn_in_common(x_ref, sh_ref, sc_ref, win_ref, qn_ref, wuq_ref,
                                       kvn_ref, cos_ref, s1_ref, s2_ref)
    ckv_out[...] = ckv
    kpe_out[...] = kp[:, :ROPE_DIM]
    for hd in range(N_HEADS):
        c0 = hd * HEAD_W
        qa = _dot(q[:, c0:c0 + NOPE_DIM].astype(BF16), wukt_ref[hd])
        qa_out[:, hd] = qa.reshape(nb, t, KV_LORA).astype(BF16)
        qp = rope(q[:, c0 + NOPE_DIM:c0 + HEAD_W])
        qpe_out[:, hd] = qp.reshape(nb, t, ROPE_W).astype(BF16)


def _row_spec(tm, cols, tiles_per_batch):
    return pl.BlockSpec((tm, cols), lambda b, s: (b * tiles_per_batch + s, 0))


def _mod_spec(arr):
    return pl.BlockSpec((None,) + arr.shape[1:], lambda b, s: (b, 0, 0))


def _full_spec(arr):
    nd = arr.ndim
    return pl.BlockSpec(arr.shape, lambda b, s: (0,) * nd)


def _attn_in_prompt(x, sh, sc, lw, tabs, nbatch, seq, tm=256):
    m, d = x.shape
    nt = seq // tm
    cos, s1, s2 = tabs
    tab_spec = pl.BlockSpec((tm, ROPE_W), lambda b, s: (s, 0))
    ins = [x, sh, sc, lw["w_in_a"], lw["q_norm"], lw["w_uq"], lw["kv_norm"],
           cos, s1, s2, lw["w_uk"], lw["w_uv"]]
    specs = [_row_spec(tm, d, nt), _mod_spec(sh), _mod_spec(sc), _full_spec(lw["w_in_a"]),
             _full_spec(lw["q_norm"]), _full_spec(lw["w_uq"]), _full_spec(lw["kv_norm"]),
             tab_spec, tab_spec, tab_spec, _full_spec(lw["w_uk"]), _full_spec(lw["w_uv"])]
    outs = [jax.ShapeDtypeStruct((m, N_HEADS * HEAD_W), BF16),
            jax.ShapeDtypeStruct((m, N_HEADS * HEAD_W), BF16),
            jax.ShapeDtypeStruct((m, D_ATTN), BF16),
            jax.ShapeDtypeStruct((m, KV_LORA), F32),
            jax.ShapeDtypeStruct((m, ROPE_DIM), F32)]
    out_specs = [_row_spec(tm, N_HEADS * HEAD_W, nt), _row_spec(tm, N_HEADS * HEAD_W, nt),
                 _row_spec(tm, D_ATTN, nt), _row_spec(tm, KV_LORA, nt),
                 _row_spec(tm, ROPE_DIM, nt)]
    return pl.pallas_call(
        _attn_in_prompt_kernel, grid=(nbatch, nt), in_specs=specs, out_specs=out_specs,
        out_shape=outs, compiler_params=_cparams(2), name="attn_in_prompt",
    )(*ins)


def _attn_in_sample(x, sh, sc, lw, tabs, nb, t):
    m, d = x.shape
    cos, s1, s2 = tabs
    ins = [x, sh, sc, lw["w_in_a"], lw["q_norm"], lw["w_uq"], lw["kv_norm"],
           cos, s1, s2, lw["w_uk_t"]]
    specs = [_row_spec(m, d, 1), _mod_spec(sh), _mod_spec(sc)] + [_full_spec(a) for a in ins[3:]]
    outs = [jax.ShapeDtypeStruct((nb, N_HEADS, t, KV_LORA), BF16),
            jax.ShapeDtypeStruct((nb, N_HEADS, t, ROPE_W), BF16),
            jax.ShapeDtypeStruct((m, KV_LORA), F32),
            jax.ShapeDtypeStruct((m, ROPE_DIM), F32)]
    out_specs = [pl.BlockSpec((nb, N_HEADS, t, KV_LORA), lambda b, s: (0, 0, 0, 0)),
                 pl.BlockSpec((nb, N_HEADS, t, ROPE_W), lambda b, s: (0, 0, 0, 0)),
                 _row_spec(m, KV_LORA, 1), _row_spec(m, ROPE_DIM, 1)]
    return pl.pallas_call(
        functools.partial(_attn_in_sample_kernel, nb=nb, t=t), grid=(1, 1),
        in_specs=specs, out_specs=out_specs, out_shape=outs,
        compiler_params=_cparams(2), name="attn_in_sample",
    )(*ins)


def _conv_in_kernel(x_ref, sh_ref, sc_ref, w_ref, cw_ref, cn_ref, prev_ref,
                    o_ref, st_ref, buf, *, nb, t, dc):
    x = x_ref[...]
    h = (_rms(x) * (1.0 + sc_ref[...]) + sh_ref[...]).astype(BF16)
    z = _dot(h, w_ref[...])
    gate_b = z[:, :dc]
    cu = z[:, dc:2 * dc] * z[:, 2 * dc:]

    @pl.when(pl.program_id(1) == 0)
    def _():
        buf[:, CONV_PAD - 2:CONV_PAD, :] = prev_ref[...]

    buf[:, CONV_PAD:CONV_PAD + t, :] = cu.reshape(nb, t, dc)
    y = cw_ref[0:1, :].reshape(1, 1, dc) * buf[:, CONV_PAD - 2:CONV_PAD - 2 + t, :]
    for k in range(1, CONV_W):
        y = y + cw_ref[k:k + 1, :].reshape(1, 1, dc) * buf[:, CONV_PAD - 2 + k:CONV_PAD - 2 + k + t, :]
    last = buf[:, CONV_PAD + t - 2:CONV_PAD + t, :]
    st_ref[...] = last
    buf[:, CONV_PAD - 2:CONV_PAD, :] = last
    conv_o = gate_b * y.reshape(nb * t, dc)
    o_ref[...] = (_rms(conv_o) * cn_ref[...]).astype(BF16)


def _conv_in(x, sh, sc, lw, prev, nbatch, seq, nb, t):
    m, d = x.shape
    dc = lw["conv_w"].shape[-1]
    nt = seq // t
    ins = [x, sh, sc, lw["w_in_b"], lw["conv_w"], lw["conv_out_norm"], prev]
    specs = [_row_spec(nb * t, d, nt), _mod_spec(sh), _mod_spec(sc), _full_spec(lw["w_in_b"]),
             _full_spec(lw["conv_w"]), _full_spec(lw["conv_out_norm"]),
             pl.BlockSpec((nb, CONV_W - 1, dc), lambda b, s: (b, 0, 0))]
    outs = [jax.ShapeDtypeStruct((m, dc), BF16),
            jax.ShapeDtypeStruct(prev.shape, F32)]
    out_specs = [_row_spec(nb * t, dc, nt),
                 pl.BlockSpec((nb, CONV_W - 1, dc), lambda b, s: (b, 0, 0))]
    return pl.pallas_call(
        functools.partial(_conv_in_kernel, nb=nb, t=t, dc=dc),
        grid=(nbatch // nb, nt), in_specs=specs, out_specs=out_specs, out_shape=outs,
        scratch_shapes=[pltpu.VMEM((nb, CONV_PAD + t, dc), F32)],
        compiler_params=_cparams(2), name="conv_in",
    )(*ins)


def _prompt_attn_kernel(q_ref, k_ref, v_ref, o_ref, *, seq, tq):
    row_chunk = lax.broadcasted_iota(jnp.int32, (tq, tq), 0) // CHUNK
    col_chunk = lax.broadcasted_iota(jnp.int32, (tq, tq), 1) // CHUNK
    visible = col_chunk <= row_chunk
    for qi in range(seq // tq):
        lo, hi = qi * tq, (qi + 1) * tq
        q = q_ref[lo:hi, :]
        s_d = jnp.where(visible, _dot_t(q, k_ref[lo:hi, :]), NEG_BIG)
        m = jnp.max(s_d, axis=-1, keepdims=True)
        if qi > 0:
            s_p = _dot_t(q, k_ref[0:lo, :])
            m = jnp.maximum(m, jnp.max(s_p, axis=-1, keepdims=True))
        p_d = jnp.exp2(s_d - m)
        l = jnp.sum(p_d, axis=-1, keepdims=True)
        o = _dot(p_d.astype(BF16), v_ref[lo:hi, :])
        if qi > 0:
            p_p = jnp.exp2(s_p - m)
            l = l + jnp.sum(p_p, axis=-1, keepdims=True)
            o = o + _dot(p_p.astype(BF16), v_ref[0:lo, :])
        o_ref[lo:hi, :] = o / l


def _prompt_attention(q, k, v, nbatch, seq, tq=256):
    m = q.shape[0]
    return pl.pallas_call(
        functools.partial(_prompt_attn_kernel, seq=seq, tq=tq),
        grid=(nbatch, N_HEADS),
        in_specs=[pl.BlockSpec((seq, HEAD_W), lambda b, h: (b, h)),
                  pl.BlockSpec((seq, HEAD_W), lambda b, h: (b, h)),
                  pl.BlockSpec((seq, V_DIM), lambda b, h: (b, h))],
        out_specs=pl.BlockSpec((seq, V_DIM), lambda b, h: (b, h)),
        out_shape=jax.ShapeDtypeStruct((m, D_ATTN), F32),
        compiler_params=_cparams(2), name="prompt_attention",
    )(q, k, v)


def _sample_attn_kernel(qa_ref, qpe_ref, pc_ref, pk_ref, nc_ref, nk_ref, wuv_ref, o_ref, *, t):
    rows = N_HEADS * t
    qa = qa_ref[0].reshape(rows, KV_LORA)
    qp = qpe_ref[0].reshape(rows, ROPE_W)[:, :ROPE_DIM]
    pc = pc_ref[0].astype(BF16)
    pk = pk_ref[0].astype(BF16)
    nc = nc_ref[...].astype(BF16)
    nk = nk_ref[...].astype(BF16)
    s_p = _dot_t(qa, pc) + _dot_t(qp, pk)
    s_n = _dot_t(qa, nc) + _dot_t(qp, nk)
    m = jnp.maximum(jnp.max(s_p, axis=-1, keepdims=True), jnp.max(s_n, axis=-1, keepdims=True))
    p_p = jnp.exp2(s_p - m)
    p_n = jnp.exp2(s_n - m)
    l = jnp.sum(p_p, axis=-1, keepdims=True) + jnp.sum(p_n, axis=-1, keepdims=True)
    o_lat = (_dot(p_p.astype(BF16), pc) + _dot(p_n.astype(BF16), nc)) / l
    o_lat = o_lat.astype(BF16)
    for hd in range(N_HEADS):
        o_ref[:, hd * V_DIM:(hd + 1) * V_DIM] = _dot(o_lat[hd * t:(hd + 1) * t, :], wuv_ref[hd])


def _sample_attention(qa, qpe, past_ckv, past_kpe, new_ckv, new_kpe, w_uv_h, layer):
    nb, _, t, _ = qa.shape
    past = past_ckv.shape[2]
    return pl.pallas_call(
        functools.partial(_sample_attn_kernel, t=t),
        grid=(nb,),
        in_specs=[pl.BlockSpec((1, N_HEADS, t, KV_LORA), lambda b: (b, 0, 0, 0)),
                  pl.BlockSpec((1, N_HEADS, t, ROPE_W), lambda b: (b, 0, 0, 0)),
                  pl.BlockSpec((None, 1, past, KV_LORA), lambda b: (layer, b, 0, 0)),
                  pl.BlockSpec((None, 1, past, ROPE_DIM), lambda b: (layer, b, 0, 0)),
                  pl.BlockSpec((t, KV_LORA), lambda b: (b, 0)),
                  pl.BlockSpec((t, ROPE_DIM), lambda b: (b, 0)),
                  pl.BlockSpec(w_uv_h.shape, lambda b: (0, 0, 0))],
        out_specs=pl.BlockSpec((t, D_ATTN), lambda b: (b, 0)),
        out_shape=jax.ShapeDtypeStruct((nb * t, D_ATTN), F32),
        compiler_params=_cparams(1), name="sample_attention",
    )(qa, qpe, past_ckv, past_kpe, new_ckv, new_kpe, w_uv_h)


def _out_proj_kernel(x_ref, attn_ref, conv_ref, g_ref, an_ref, woa_ref, woc_ref, o_ref):
    a = (_rms(attn_ref[...]) * an_ref[...]).astype(BF16)
    mix = _dot(a, woa_ref[...]) + _dot(conv_ref[...], woc_ref[...])
    o_ref[...] = x_ref[...] + g_ref[...] * mix


def _out_proj(x, attn, conv_n, g, lw, nbatch, seq, tm):
    m, d = x.shape
    nt = seq // tm
    ins = [x, attn, conv_n, g, lw["attn_out_norm"], lw["w_out_a"], lw["w_out_c"]]
    specs = [_row_spec(tm, d, nt), _row_spec(tm, attn.shape[1], nt),
             _row_spec(tm, conv_n.shape[1], nt), _mod_spec(g),
             _full_spec(lw["attn_out_norm"]), _full_spec(lw["w_out_a"]), _full_spec(lw["w_out_c"])]
    return pl.pallas_call(
        _out_proj_kernel, grid=(nbatch, nt), in_specs=specs,
        out_specs=_row_spec(tm, d, nt), out_shape=jax.ShapeDtypeStruct((m, d), F32),
        compiler_params=_cparams(2), name="out_proj",
    )(*ins)


def _mlp_kernel(x_ref, sh_ref, sc_ref, g_ref, wup_ref, wdn_ref, fn_ref, o_ref, h_scr, *, nf, final):
    f = pl.program_id(2)

    @pl.when(f == 0)
    def _():
        h_scr[...] = (_rms(x_ref[...]) * (1.0 + sc_ref[...]) + sh_ref[...]).astype(BF16)

    up = _dot(h_scr[...], wup_ref[...])
    act = jnp.square(jnp.maximum(up, 0.0)).astype(BF16)
    down = _dot(act, wdn_ref[...])

    @pl.when(f == 0)
    def _():
        o_ref[...] = down

    @pl.when(f > 0)
    def _():
        o_ref[...] += down

    @pl.when(f == nf - 1)
    def _():
        y = x_ref[...] + g_ref[...] * o_ref[...]
        if final:
            y = _rms(y) * fn_ref[...]
        o_ref[...] = y


def _mlp(x, sh, sc, g, lw, final_norm, nbatch, seq, tm, tf, final):
    m, d = x.shape
    nt = seq // tm
    dff = lw["w_up"].shape[1]
    nf = dff // tf
    row = pl.BlockSpec((tm, d), lambda b, s, f: (b * nt + s, 0))
    mod = lambda a: pl.BlockSpec((None,) + a.shape[1:], lambda b, s, f: (b, 0, 0))
    return pl.pallas_call(
        functools.partial(_mlp_kernel, nf=nf, final=final),
        grid=(nbatch, nt, nf),
        in_specs=[row, mod(sh), mod(sc), mod(g),
                  pl.BlockSpec((d, tf), lambda b, s, f: (0, f)),
                  pl.BlockSpec((tf, d), lambda b, s, f: (f, 0)),
                  pl.BlockSpec(final_norm.shape, lambda b, s, f: (0, 0))],
        out_specs=row, out_shape=jax.ShapeDtypeStruct((m, d), F32),
        scratch_shapes=[pltpu.VMEM((tm, d), BF16)],
        compiler_params=_cparams(3), name="mlp",
    )(x, sh, sc, g, lw["w_up"], lw["w_down"], final_norm)


def _rope_tables(pos):
    half = ROPE_DIM // 2
    inv = ROPE_BASE ** (-jnp.arange(half, dtype=F32) / half)
    ang = pos.astype(F32)[:, None] * inv[None, :]
    cos, sin, zero = jnp.cos(ang), jnp.sin(ang), jnp.zeros_like(ang)
    return (jnp.concatenate([cos, cos, zero, zero], axis=-1),
            jnp.concatenate([-sin, zero, zero, zero], axis=-1),
            jnp.concatenate([zero, sin, zero, zero], axis=-1))


def _prep_weights(w_in, q_norm, w_uq, kv_norm, w_ukv, conv_w, attn_out_norm, conv_out_norm,
                  w_out, w_up, w_down):
    depth, d, _ = w_in.shape
    n_a = Q_LORA + KV_LORA + ROPE_DIM
    w_in_a = jnp.pad(w_in[:, :, :n_a], ((0, 0), (0, 0), (0, ROPE_W - ROPE_DIM))).astype(BF16)
    w_in_b = w_in[:, :, n_a:].astype(BF16)
    wq = w_uq.reshape(depth, Q_LORA, N_HEADS, QK_DIM)
    wq = jnp.pad(wq, ((0, 0), (0, 0), (0, 0), (0, HEAD_W - QK_DIM)))
    wq = wq.reshape(depth, Q_LORA, N_HEADS * HEAD_W).astype(BF16)
    wkv = w_ukv.reshape(depth, KV_LORA, N_HEADS, NOPE_DIM + V_DIM)
    w_uk = wkv[..., :NOPE_DIM]
    w_uv = wkv[..., NOPE_DIM:]
    return dict(
        w_in_a=w_in_a, w_in_b=w_in_b, w_uq=wq,
        w_uk=w_uk.reshape(depth, KV_LORA, N_HEADS * NOPE_DIM).astype(BF16),
        w_uv=w_uv.reshape(depth, KV_LORA, N_HEADS * V_DIM).astype(BF16),
        w_uk_t=jnp.transpose(w_uk, (0, 2, 3, 1)).astype(BF16),
        w_uv_h=jnp.transpose(w_uv, (0, 2, 1, 3)).astype(BF16),
        q_norm=q_norm[:, None, :], kv_norm=kv_norm[:, None, :],
        conv_w=conv_w, attn_out_norm=attn_out_norm[:, None, :],
        conv_out_norm=conv_out_norm[:, None, :],
        w_out_a=w_out[:, :D_ATTN].astype(BF16), w_out_c=w_out[:, D_ATTN:].astype(BF16),
        w_up=w_up.astype(BF16), w_down=w_down.astype(BF16))


def kernel(x_prompt, x_sample, c_prompt, c_sample, cache_kv_latent, cache_k_rope, state_conv,
           w_ada, b_ada, w_in, q_norm, w_uq, kv_norm, w_ukv, conv_w,
           attn_out_norm, conv_out_norm, w_out, w_up, w_down, final_norm):
    bp, sp, d = x_prompt.shape
    bs, ss, _ = x_sample.shape
    depth = w_ada.shape[0]
    past = cache_kv_latent.shape[2]
    dc = conv_w.shape[-1]
    ms = bs * ss

    weights = _prep_weights(w_in, q_norm, w_uq, kv_norm, w_ukv, conv_w, attn_out_norm,
                            conv_out_norm, w_out, w_up, w_down)
    fnorm = final_norm[None, :]
    tabs_p = _rope_tables(jnp.arange(sp))
    tabs_s = tuple(jnp.tile(tb, (bs, 1)) for tb in _rope_tables(past + jnp.arange(ss)))

    mod = _modulation(jnp.concatenate([c_prompt, c_sample], axis=0), w_ada, b_ada)
    mod = mod.reshape(depth, bp + bs, 6, d)
    zero_conv = jnp.zeros((bp, CONV_W - 1, dc), F32)

    xp = x_prompt.reshape(bp * sp, d)
    xs = x_sample.reshape(ms, d)
    outs = [[] for _ in range(6)]
    for l in range(depth):
        lw = {k: v[l] for k, v in weights.items()}
        last = l == depth - 1
        mp = [mod[l, :bp, i][:, None, :] for i in range(6)]
        msm = [jnp.repeat(mod[l, bp:, i], ss, axis=0)[None] for i in range(6)]

        q, k, v, ckv, kpe = _attn_in_prompt(xp, mp[0], mp[1], lw, tabs_p, bp, sp)
        conv_n, cstate = _conv_in(xp, mp[0], mp[1], lw, zero_conv, bp, sp, nb=1, t=512)
        attn = _prompt_attention(q, k, v, bp, sp)
        xp = _out_proj(xp, attn, conv_n, mp[2], lw, bp, sp, tm=512)
        xp = _mlp(xp, mp[3], mp[4], mp[5], lw, fnorm, bp, sp, tm=512, tf=512, final=last)
        outs[0].append(ckv.reshape(bp, sp, KV_LORA))
        outs[1].append(kpe.reshape(bp, sp, ROPE_DIM))
        outs[2].append(cstate)

        qa, qpe, ckv_s, kpe_s = _attn_in_sample(xs, msm[0], msm[1], lw, tabs_s, bs, ss)
        conv_ns, cstate_s = _conv_in(xs, msm[0], msm[1], lw, state_conv[l], bs, ss, nb=bs, t=ss)
        attn_s = _sample_attention(qa, qpe, cache_kv_latent, cache_k_rope, ckv_s, kpe_s,
                                   lw["w_uv_h"], l)
        xs = _out_proj(xs, attn_s, conv_ns, msm[2], lw, 1, ms, tm=ms)
        xs = _mlp(xs, msm[3], msm[4], msm[5], lw, fnorm, 1, ms, tm=ms, tf=512, final=last)
        outs[3].append(ckv_s.reshape(bs, ss, KV_LORA))
        outs[4].append(kpe_s.reshape(bs, ss, ROPE_DIM))
        outs[5].append(cstate_s)

    return (xp.reshape(bp, sp, d), xs.reshape(bs, ss, d),
            jnp.stack(outs[0]), jnp.stack(outs[1]), jnp.stack(outs[2]),
            jnp.stack(outs[3]), jnp.stack(outs[4]), jnp.stack(outs[5]))
```

```python
import functools
import math

import jax
import jax.numpy as jnp
from jax import lax
from jax.experimental import pallas as pl
from jax.experimental.pallas import tpu as pltpu

F32 = jnp.float32
BF16 = jnp.bfloat16

CHUNK = 64
N_HEADS = 8
NOPE_DIM = 128
ROPE_DIM = 64
V_DIM = 128
QK_DIM = NOPE_DIM + ROPE_DIM
Q_LORA = 512
KV_LORA = 512
D_ATTN = N_HEADS * V_DIM
CONV_W = 3
N_MOD = 6
ROPE_BASE = 10000.0
EPS = 1e-6
ATTN_SCALE = QK_DIM ** -0.5

LANES = 128
HEAD_W = 2 * LANES
ROPE_W = LANES
IN_A_W = Q_LORA + KV_LORA + ROPE_W
CONV_PAD = 8
Q_SCALE = ATTN_SCALE * math.log2(math.e)
NEG_BIG = -1e30
VMEM_LIMIT = 56 * 1024 * 1024


def _cparams(n_axes):
    return pltpu.CompilerParams(dimension_semantics=("arbitrary",) * n_axes,
                                vmem_limit_bytes=VMEM_LIMIT)


def _rms(x):
    return x * lax.rsqrt(jnp.mean(x * x, axis=-1, keepdims=True) + EPS)


def _dot(a, b):
    return jnp.dot(a, b, preferred_element_type=F32)


def _dot_t(a, b):
    return lax.dot_general(a, b, (((1,), (1,)), ((), ())), preferred_element_type=F32)


def _layer_spec(arr, layer, block=None, index=None):
    block = tuple(arr.shape[1:]) if block is None else tuple(block)
    index = (0,) * len(block) if index is None else tuple(index)
    return pl.BlockSpec((None,) + block, lambda *g: (layer,) + index)


def _rows(tm, cols, tiles_per_batch):
    return pl.BlockSpec((tm, cols), lambda b, s, *_: (b * tiles_per_batch + s, 0))


class _Mod:
    def __init__(self, arr, layer, per_batch):
        self.arr, self.layer, self.per_batch = arr, layer, per_batch

    def spec(self, role):
        layer, nb = self.layer, self.per_batch
        if nb:
            return pl.BlockSpec((None,) + self.arr.shape[1:],
                                lambda b, *_: ((layer * nb + b) * N_MOD + role, 0, 0))
        return pl.BlockSpec((None,) + self.arr.shape[1:],
                            lambda *_: (layer * N_MOD + role, 0, 0))


def _mod_kernel(c_ref, w_ref, b_ref, o_ref):
    c = c_ref[...]
    s = (c * jax.nn.sigmoid(c)).astype(BF16)
    o_ref[0] = _dot(s, w_ref[0].astype(BF16)) + b_ref[0]


def _modulation(c_all, w_ada, b_ada, tn=1024):
    depth, d, n = w_ada.shape
    r = c_all.shape[0]
    return pl.pallas_call(
        _mod_kernel,
        grid=(depth, n // tn),
        in_specs=[pl.BlockSpec((r, d), lambda l, j: (0, 0)),
                  pl.BlockSpec((1, d, tn), lambda l, j: (l, 0, j)),
                  pl.BlockSpec((1, 1, tn), lambda l, j: (l, 0, j))],
        out_specs=pl.BlockSpec((1, r, tn), lambda l, j: (l, 0, j)),
        out_shape=jax.ShapeDtypeStruct((depth, r, n), F32),
        compiler_params=_cparams(2),
        name="modulation",
    )(c_all, w_ada, b_ada.reshape(depth, 1, n))


def _rope128(p, cos, s1, s2):
    return p * cos + pltpu.roll(p, 96, 1) * s1 + pltpu.roll(p, 32, 1) * s2


def _attn_in_common(x_ref, sh_ref, sc_ref, win_ref, qn_ref, wuq_ref, kvn_ref,
                    cos_ref, s1_ref, s2_ref):
    x = x_ref[...]
    h = (_rms(x) * (1.0 + sc_ref[...]) + sh_ref[...]).astype(BF16)
    z = _dot(h, win_ref[...])
    cos, s1, s2 = cos_ref[...], s1_ref[...], s2_ref[...]
    qn = (_rms(z[:, :Q_LORA]) * qn_ref[...]).astype(BF16)
    q = _dot(qn, wuq_ref[...]) * Q_SCALE
    ckv = _rms(z[:, Q_LORA:Q_LORA + KV_LORA]) * kvn_ref[...]
    kp = _rope128(z[:, Q_LORA + KV_LORA:], cos, s1, s2)
    rope = functools.partial(_rope128, cos=cos, s1=s1, s2=s2)
    return q, ckv, kp, rope


def _attn_in_prompt_kernel(x_ref, sh_ref, sc_ref, win_ref, qn_ref, wuq_ref, kvn_ref,
                           cos_ref, s1_ref, s2_ref, wuk_ref, wuv_ref, *rest):
    q_out, k_out, v_out, ckv_out, kpe_out = rest[-5:]
    q, ckv, kp, rope = _attn_in_common(x_ref, sh_ref, sc_ref, win_ref, qn_ref, wuq_ref,
                                       kvn_ref, cos_ref, s1_ref, s2_ref)
    ckv_out[...] = ckv
    kpe_out[...] = kp[:, :ROPE_DIM]
    ckb = ckv.astype(BF16)
    kn = _dot(ckb, wuk_ref[...])
    v_out[...] = _dot(ckb, wuv_ref[...]).astype(BF16)
    kpb = kp.astype(BF16)
    for hd in range(N_HEADS):
        c0 = hd * HEAD_W
        q_out[:, c0:c0 + NOPE_DIM] = q[:, c0:c0 + NOPE_DIM].astype(BF16)
        q_out[:, c0 + NOPE_DIM:c0 + HEAD_W] = rope(q[:, c0 + NOPE_DIM:c0 + HEAD_W]).astype(BF16)
        k_out[:, c0:c0 + NOPE_DIM] = kn[:, hd * NOPE_DIM:(hd + 1) * NOPE_DIM].astype(BF16)
        k_out[:, c0 + NOPE_DIM:c0 + HEAD_W] = kpb


def _attn_in_sample_kernel(x_ref, sh_ref, sc_ref, win_ref, qn_ref, wuq_ref, kvn_ref,
                           cos_ref, s1_ref, s2_ref, wukt_ref,
                           qa_out, qpe_out, ckv_out, kpe_out, *, nb, t):
    q, ckv, kp, rope = _attn_in_common(x_ref, sh_ref, sc_ref, win_ref, qn_ref, wuq_ref,
                                       kvn_ref, cos_ref, s1_ref, s2_ref)
    ckv_out[...] = ckv
    kpe_out[...] = kp[:, :ROPE_DIM]
    for hd in range(N_HEADS):
        c0 = hd * HEAD_W
        qa = _dot(q[:, c0:c0 + NOPE_DIM].astype(BF16), wukt_ref[hd])
        qa_out[:, hd] = qa.reshape(nb, t, KV_LORA).astype(BF16)
        qp = rope(q[:, c0 + NOPE_DIM:c0 + HEAD_W])
        qpe_out[:, hd] = qp.reshape(nb, t, ROPE_W).astype(BF16)


def _attn_in_prompt(x, mod, w, layer, tabs, nbatch, seq, stacks, tm=256):
    m, d = x.shape
    depth = w["w_in"].shape[0]
    nt = seq // tm
    tab_spec = pl.BlockSpec((tm, ROPE_W), lambda b, s: (s, 0))
    ins = [x, mod.arr, mod.arr, w["w_in"], w["q_norm"], w["w_uq"], w["kv_norm"],
           *tabs, w["w_uk"], w["w_uv"]]
    specs = [_rows(tm, d, nt), mod.spec(0), mod.spec(1),
             _layer_spec(w["w_in"], layer, (d, IN_A_W)),
             _layer_spec(w["q_norm"], layer), _layer_spec(w["w_uq"], layer),
             _layer_spec(w["kv_norm"], layer), tab_spec, tab_spec, tab_spec,
             _layer_spec(w["w_uk"], layer), _layer_spec(w["w_uv"], layer)]
    aliases = {}
    if stacks is not None:
        aliases = {len(ins): 3, len(ins) + 1: 4}
        ins += list(stacks)
        specs += [pl.BlockSpec(memory_space=pl.ANY)] * 2
    outs = [jax.ShapeDtypeStruct((m, N_HEADS * HEAD_W), BF16),
            jax.ShapeDtypeStruct((m, N_HEADS * HEAD_W), BF16),
            jax.ShapeDtypeStruct((m, D_ATTN), BF16),
            jax.ShapeDtypeStruct((depth, m, KV_LORA), F32),
            jax.ShapeDtypeStruct((depth, m, ROPE_DIM), F32)]
    stack_spec = lambda cols: pl.BlockSpec((None, tm, cols), lambda b, s: (layer, b * nt + s, 0))
    out_specs = [_rows(tm, N_HEADS * HEAD_W, nt), _rows(tm, N_HEADS * HEAD_W, nt),
                 _rows(tm, D_ATTN, nt), stack_spec(KV_LORA), stack_spec(ROPE_DIM)]
    return pl.pallas_call(
        _attn_in_prompt_kernel, grid=(nbatch, nt), in_specs=specs, out_specs=out_specs,
        out_shape=outs, input_output_aliases=aliases,
        compiler_params=_cparams(2), name="attn_in_prompt",
    )(*ins)


def _attn_in_sample(x, mod, w, layer, tabs, nb, t):
    m, d = x.shape
    ins = [x, mod.arr, mod.arr, w["w_in"], w["q_norm"], w["w_uq"], w["kv_norm"],
           *tabs, w["w_uk_t"]]
    full = lambda a: pl.BlockSpec(a.shape, lambda *_: (0,) * a.ndim)
    specs = [_rows(m, d, 1), mod.spec(0), mod.spec(1),
             _layer_spec(w["w_in"], layer, (d, IN_A_W)),
             _layer_spec(w["q_norm"], layer), _layer_spec(w["w_uq"], layer),
             _layer_spec(w["kv_norm"], layer), full(tabs[0]), full(tabs[1]), full(tabs[2]),
             _layer_spec(w["w_uk_t"], layer)]
    outs = [jax.ShapeDtypeStruct((nb, N_HEADS, t, KV_LORA), BF16),
            jax.ShapeDtypeStruct((nb, N_HEADS, t, ROPE_W), BF16),
            jax.ShapeDtypeStruct((m, KV_LORA), F32),
            jax.ShapeDtypeStruct((m, ROPE_DIM), F32)]
    out_specs = [pl.BlockSpec((nb, N_HEADS, t, KV_LORA), lambda b, s: (0, 0, 0, 0)),
                 pl.BlockSpec((nb, N_HEADS, t, ROPE_W), lambda b, s: (0, 0, 0, 0)),
                 _rows(m, KV_LORA, 1), _rows(m, ROPE_DIM, 1)]
    return pl.pallas_call(
        functools.partial(_attn_in_sample_kernel, nb=nb, t=t), grid=(1, 1),
        in_specs=specs, out_specs=out_specs, out_shape=outs,
        compiler_params=_cparams(2), name="attn_in_sample",
    )(*ins)


def _conv_in_kernel(x_ref, sh_ref, sc_ref, w_ref, cw_ref, cn_ref, prev_ref,
                    o_ref, st_ref, buf, *, nb, t, dc):
    x = x_ref[...]
    h = (_rms(x) * (1.0 + sc_ref[...]) + sh_ref[...]).astype(BF16)
    z = _dot(h, w_ref[...])
    gate_b = z[:, :dc]
    cu = z[:, dc:2 * dc] * z[:, 2 * dc:]

    @pl.when(pl.program_id(1) == 0)
    def _():
        buf[:, CONV_PAD - 2:CONV_PAD, :] = prev_ref[...]

    buf[:, CONV_PAD:CONV_PAD + t, :] = cu.reshape(nb, t, dc)
    y = cw_ref[0:1, :].reshape(1, 1, dc) * buf[:, CONV_PAD - 2:CONV_PAD - 2 + t, :]
    for k in range(1, CONV_W):
        y = y + cw_ref[k:k + 1, :].reshape(1, 1, dc) * buf[:, CONV_PAD - 2 + k:CONV_PAD - 2 + k + t, :]
    last = buf[:, CONV_PAD + t - 2:CONV_PAD + t, :]
    st_ref[...] = last
    buf[:, CONV_PAD - 2:CONV_PAD, :] = last
    conv_o = gate_b * y.reshape(nb * t, dc)
    o_ref[...] = (_rms(conv_o) * cn_ref[...]).astype(BF16)


def _conv_in(x, mod, w, layer, prev, prev_layer, nbatch, seq, nb, t):
    m, d = x.shape
    dc = w["conv_w"].shape[-1]
    nt = seq // t
    ins = [x, mod.arr, mod.arr, w["w_in_b"], w["conv_w"], w["conv_out_norm"], prev]
    specs = [_rows(nb * t, d, nt), mod.spec(0), mod.spec(1), _layer_spec(w["w_in_b"], layer),
             _layer_spec(w["conv_w"], layer), _layer_spec(w["conv_out_norm"], layer),
             pl.BlockSpec((None, nb, CONV_W - 1, dc), lambda b, s: (prev_layer, b, 0, 0))]
    outs = [jax.ShapeDtypeStruct((m, dc), BF16),
            jax.ShapeDtypeStruct(prev.shape[1:], F32)]
    out_specs = [_rows(nb * t, dc, nt),
                 pl.BlockSpec((nb, CONV_W - 1, dc), lambda b, s: (b, 0, 0))]
    return pl.pallas_call(
        functools.partial(_conv_in_kernel, nb=nb, t=t, dc=dc),
        grid=(nbatch // nb, nt), in_specs=specs, out_specs=out_specs, out_shape=outs,
        scratch_shapes=[pltpu.VMEM((nb, CONV_PAD + t, dc), F32)],
        compiler_params=_cparams(2), name="conv_in",
    )(*ins)


def _prompt_attn_kernel(q_ref, k_ref, v_ref, o_ref, *, seq, tq):
    row_chunk = lax.broadcasted_iota(jnp.int32, (tq, tq), 0) // CHUNK
    col_chunk = lax.broadcasted_iota(jnp.int32, (tq, tq), 1) // CHUNK
    visible = col_chunk <= row_chunk
    for qi in range(seq // tq):
        lo, hi = qi * tq, (qi + 1) * tq
        q = q_ref[lo:hi, :]
        s_d = jnp.where(visible, _dot_t(q, k_ref[lo:hi, :]), NEG_BIG)
        m = jnp.max(s_d, axis=-1, keepdims=True)
        if qi > 0:
            s_p = _dot_t(q, k_ref[0:lo, :])
            m = jnp.maximum(m, jnp.max(s_p, axis=-1, keepdims=True))
        p_d = jnp.exp2(s_d - m)
        l = jnp.sum(p_d, axis=-1, keepdims=True)
        o = _dot(p_d.astype(BF16), v_ref[lo:hi, :])
        if qi > 0:
            p_p = jnp.exp2(s_p - m)
            l = l + jnp.sum(p_p, axis=-1, keepdims=True)
            o = o + _dot(p_p.astype(BF16), v_ref[0:lo, :])
        o_ref[lo:hi, :] = o / l


def _prompt_attention(q, k, v, nbatch, seq, tq=256):
    m = q.shape[0]
    return pl.pallas_call(
        functools.partial(_prompt_attn_kernel, seq=seq, tq=tq),
        grid=(nbatch, N_HEADS),
        in_specs=[pl.BlockSpec((seq, HEAD_W), lambda b, h: (b, h)),
                  pl.BlockSpec((seq, HEAD_W), lambda b, h: (b, h)),
                  pl.BlockSpec((seq, V_DIM), lambda b, h: (b, h))],
        out_specs=pl.BlockSpec((seq, V_DIM), lambda b, h: (b, h)),
        out_shape=jax.ShapeDtypeStruct((m, D_ATTN), F32),
        compiler_params=_cparams(2), name="prompt_attention",
    )(q, k, v)


def _sample_attn_kernel(qa_ref, qpe_ref, pc_ref, pk_ref, nc_ref, nk_ref, wuv_ref, o_ref, *, t):
    rows = N_HEADS * t
    qa = qa_ref[0].reshape(rows, KV_LORA)
    qp = qpe_ref[0].reshape(rows, ROPE_W)[:, :ROPE_DIM]
    pc = pc_ref[0].astype(BF16)
    pk = pk_ref[0].astype(BF16)
    nc = nc_ref[...].astype(BF16)
    nk = nk_ref[...].astype(BF16)
    s_p = _dot_t(qa, pc) + _dot_t(qp, pk)
    s_n = _dot_t(qa, nc) + _dot_t(qp, nk)
    m = jnp.maximum(jnp.max(s_p, axis=-1, keepdims=True), jnp.max(s_n, axis=-1, keepdims=True))
    p_p = jnp.exp2(s_p - m)
    p_n = jnp.exp2(s_n - m)
    l = jnp.sum(p_p, axis=-1, keepdims=True) + jnp.sum(p_n, axis=-1, keepdims=True)
    o_lat = (_dot(p_p.astype(BF16), pc) + _dot(p_n.astype(BF16), nc)) / l
    o_lat = o_lat.astype(BF16)
    for hd in range(N_HEADS):
        o_ref[:, hd * V_DIM:(hd + 1) * V_DIM] = _dot(o_lat[hd * t:(hd + 1) * t, :], wuv_ref[hd])


def _sample_attention(qa, qpe, past_ckv, past_kpe, new_ckv, new_kpe, w_uv_h, layer):
    nb, _, t, _ = qa.shape
    past = past_ckv.shape[2]
    return pl.pallas_call(
        functools.partial(_sample_attn_kernel, t=t),
        grid=(nb,),
        in_specs=[pl.BlockSpec((1, N_HEADS, t, KV_LORA), lambda b: (b, 0, 0, 0)),
                  pl.BlockSpec((1, N_HEADS, t, ROPE_W), lambda b: (b, 0, 0, 0)),
                  pl.BlockSpec((None, 1, past, KV_LORA), lambda b: (layer, b, 0, 0)),
                  pl.BlockSpec((None, 1, past, ROPE_DIM), lambda b: (layer, b, 0, 0)),
                  pl.BlockSpec((t, KV_LORA), lambda b: (b, 0)),
                  pl.BlockSpec((t, ROPE_DIM), lambda b: (b, 0)),
                  _layer_spec(w_uv_h, layer)],
        out_specs=pl.BlockSpec((t, D_ATTN), lambda b: (b, 0)),
        out_shape=jax.ShapeDtypeStruct((nb * t, D_ATTN), F32),
        compiler_params=_cparams(1), name="sample_attention",
    )(qa, qpe, past_ckv, past_kpe, new_ckv, new_kpe, w_uv_h)


def _out_proj_kernel(x_ref, attn_ref, conv_ref, g_ref, an_ref, woa_ref, woc_ref, o_ref):
    a = (_rms(attn_ref[...]) * an_ref[...]).astype(BF16)
    mix = _dot(a, woa_ref[...]) + _dot(conv_ref[...], woc_ref[...])
    o_ref[...] = x_ref[...] + g_ref[...] * mix


def _out_proj(x, attn, conv_n, mod, w, layer, nbatch, seq, tm):
    m, d = x.shape
    nt = seq // tm
    da, dc = attn.shape[1], conv_n.shape[1]
    ins = [x, attn, conv_n, mod.arr, w["attn_out_norm"], w["w_out"], w["w_out"]]
    specs = [_rows(tm, d, nt), _rows(tm, da, nt), _rows(tm, dc, nt), mod.spec(2),
             _layer_spec(w["attn_out_norm"], layer),
             _layer_spec(w["w_out"], layer, (da, d), (0, 0)),
             _layer_spec(w["w_out"], layer, (dc, d), (da // dc, 0))]
    return pl.pallas_call(
        _out_proj_kernel, grid=(nbatch, nt), in_specs=specs,
        out_specs=_rows(tm, d, nt), out_shape=jax.ShapeDtypeStruct((m, d), F32),
        compiler_params=_cparams(2), name="out_proj",
    )(*ins)


def _mlp_kernel(x_ref, sh_ref, sc_ref, g_ref, wup_ref, wdn_ref, fn_ref, o_ref, h_scr, *, nf, final):
    f = pl.program_id(2)

    @pl.when(f == 0)
    def _():
        h_scr[...] = (_rms(x_ref[...]) * (1.0 + sc_ref[...]) + sh_ref[...]).astype(BF16)
        o_ref[...] = jnp.zeros_like(o_ref)

    up = _dot(h_scr[...], wup_ref[...])
    act = jnp.square(jnp.maximum(up, 0.0)).astype(BF16)
    o_ref[...] += _dot(act, wdn_ref[...])

    @pl.when(f == nf - 1)
    def _():
        y = x_ref[...] + g_ref[...] * o_ref[...]
        if final:
            y = _rms(y) * fn_ref[...]
        o_ref[...] = y


def _mlp(x, mod, w, layer, final_norm, nbatch, seq, tm, tf, final):
    m, d = x.shape
    nt = seq // tm
    dff = w["w_up"].shape[-1]
    nf = dff // tf
    return pl.pallas_call(
        functools.partial(_mlp_kernel, nf=nf, final=final),
        grid=(nbatch, nt, nf),
        in_specs=[_rows(tm, d, nt), mod.spec(3), mod.spec(4), mod.spec(5),
                  pl.BlockSpec((None, d, tf), lambda b, s, f: (layer, 0, f)),
                  pl.BlockSpec((None, tf, d), lambda b, s, f: (layer, f, 0)),
                  pl.BlockSpec(final_norm.shape, lambda b, s, f: (0, 0))],
        out_specs=_rows(tm, d, nt), out_shape=jax.ShapeDtypeStruct((m, d), F32),
        scratch_shapes=[pltpu.VMEM((tm, d), BF16)],
        compiler_params=_cparams(3), name="mlp",
    )(x, mod.arr, mod.arr, mod.arr, w["w_up"], w["w_down"], final_norm)


def _rope_tables(pos):
    half = ROPE_DIM // 2
    inv = ROPE_BASE ** (-jnp.arange(half, dtype=F32) / half)
    ang = pos.astype(F32)[:, None] * inv[None, :]
    cos, sin, zero = jnp.cos(ang), jnp.sin(ang), jnp.zeros_like(ang)
    return (jnp.concatenate([cos, cos, zero, zero], axis=-1),
            jnp.concatenate([-sin, zero, zero, zero], axis=-1),
            jnp.concatenate([zero, sin, zero, zero], axis=-1))


def _prep_weights(w_in, q_norm, w_uq, kv_norm, w_ukv, conv_w, attn_out_norm, conv_out_norm,
                  w_out, w_up, w_down):
    depth = w_in.shape[0]
    w_in = w_in.astype(BF16)
    wq = w_uq.astype(BF16).reshape(depth, Q_LORA, N_HEADS, QK_DIM)
    wq = jnp.pad(wq, ((0, 0), (0, 0), (0, 0), (0, HEAD_W - QK_DIM)))
    wkv = w_ukv.astype(BF16).reshape(depth, KV_LORA, N_HEADS, NOPE_DIM + V_DIM)
    w_uk = wkv[..., :NOPE_DIM]
    w_uv = wkv[..., NOPE_DIM:]
    return dict(
        w_in=w_in,
        w_in_b=w_in[:, :, Q_LORA + KV_LORA + ROPE_DIM:],
        w_uq=wq.reshape(depth, Q_LORA, N_HEADS * HEAD_W),
        w_uk=w_uk.reshape(depth, KV_LORA, N_HEADS * NOPE_DIM),
        w_uv=w_uv.reshape(depth, KV_LORA, N_HEADS * V_DIM),
        w_uk_t=jnp.transpose(w_uk, (0, 2, 3, 1)),
        w_uv_h=jnp.transpose(w_uv, (0, 2, 1, 3)),
        q_norm=q_norm[:, None, :], kv_norm=kv_norm[:, None, :],
        conv_w=conv_w, attn_out_norm=attn_out_norm[:, None, :],
        conv_out_norm=conv_out_norm[:, None, :],
        w_out=w_out.astype(BF16), w_up=w_up.astype(BF16), w_down=w_down.astype(BF16))


def kernel(x_prompt, x_sample, c_prompt, c_sample, cache_kv_latent, cache_k_rope, state_conv,
           w_ada, b_ada, w_in, q_norm, w_uq, kv_norm, w_ukv, conv_w,
           attn_out_norm, conv_out_norm, w_out, w_up, w_down, final_norm):
    bp, sp, d = x_prompt.shape
    bs, ss, _ = x_sample.shape
    depth = w_ada.shape[0]
    past = cache_kv_latent.shape[2]
    dc = conv_w.shape[-1]
    ms = bs * ss

    w = _prep_weights(w_in, q_norm, w_uq, kv_norm, w_ukv, conv_w, attn_out_norm,
                      conv_out_norm, w_out, w_up, w_down)
    fnorm = final_norm[None, :]
    tabs_p = _rope_tables(jnp.arange(sp))
    tabs_s = tuple(jnp.tile(tb, (bs, 1)) for tb in _rope_tables(past + jnp.arange(ss)))

    mod = _modulation(jnp.concatenate([c_prompt, c_sample], axis=0), w_ada, b_ada)
    mod = mod.reshape(depth, bp + bs, N_MOD, d)
    mod_p = mod[:, :bp].reshape(depth * bp * N_MOD, 1, d)
    mod_s = jnp.repeat(jnp.swapaxes(mod[:, bp:], 1, 2), ss, axis=2).reshape(depth * N_MOD, ms, d)
    zero_conv = jnp.zeros((1, bp, CONV_W - 1, dc), F32)

    xp = x_prompt.reshape(bp * sp, d)
    xs = x_sample.reshape(ms, d)
    stacks = None
    outs = [[] for _ in range(4)]
    for l in range(depth):
        last = l == depth - 1
        mp = _Mod(mod_p, l, bp)
        msm = _Mod(mod_s, l, 0)

        q, k, v, *stacks = _attn_in_prompt(xp, mp, w, l, tabs_p, bp, sp, stacks)
        conv_n, cstate = _conv_in(xp, mp, w, l, zero_conv, 0, bp, sp, nb=1, t=512)
        attn = _prompt_attention(q, k, v, bp, sp)
        xp = _out_proj(xp, attn, conv_n, mp, w, l, bp, sp, tm=512)
        xp = _mlp(xp, mp, w, l, fnorm, bp, sp, tm=1024, tf=512, final=last)
        outs[0].append(cstate)

        qa, qpe, ckv_s, kpe_s = _attn_in_sample(xs, msm, w, l, tabs_s, bs, ss)
        conv_ns, cstate_s = _conv_in(xs, msm, w, l, state_conv, l, bs, ss, nb=bs, t=ss)
        attn_s = _sample_attention(qa, qpe, cache_kv_latent, cache_k_rope, ckv_s, kpe_s,
                                   w["w_uv_h"], l)
        xs = _out_proj(xs, attn_s, conv_ns, msm, w, l, 1, ms, tm=ms)
        xs = _mlp(xs, msm, w, l, fnorm, 1, ms, tm=ms, tf=512, final=last)
        outs[1].append(ckv_s.reshape(bs, ss, KV_LORA))
        outs[2].append(kpe_s.reshape(bs, ss, ROPE_DIM))
        outs[3].append(cstate_s)

    return (xp.reshape(bp, sp, d), xs.reshape(bs, ss, d),
            stacks[0].reshape(depth, bp, sp, KV_LORA), stacks[1].reshape(depth, bp, sp, ROPE_DIM),
            jnp.stack(outs[0]), jnp.stack(outs[1]), jnp.stack(outs[2]), jnp.stack(outs[3]))
```

```python
import functools
import math

import jax
import jax.numpy as jnp
from jax import lax
from jax.experimental import pallas as pl
from jax.experimental.pallas import tpu as pltpu

F32 = jnp.float32
BF16 = jnp.bfloat16

CHUNK = 64
N_HEADS = 8
NOPE_DIM = 128
ROPE_DIM = 64
V_DIM = 128
QK_DIM = NOPE_DIM + ROPE_DIM
Q_LORA = 512
KV_LORA = 512
D_ATTN = N_HEADS * V_DIM
CONV_W = 3
N_MOD = 6
ROPE_BASE = 10000.0
EPS = 1e-6
ATTN_SCALE = QK_DIM ** -0.5

LANES = 128
HEAD_W = 2 * LANES
ROPE_W = LANES
IN_A_W = Q_LORA + KV_LORA + ROPE_W
CONV_PAD = 8
Q_SCALE = ATTN_SCALE * math.log2(math.e)
NEG_BIG = -1e30
VMEM_LIMIT = 60000 * 1024


def _cparams(n_axes):
    return pltpu.CompilerParams(dimension_semantics=("arbitrary",) * n_axes,
                                vmem_limit_bytes=VMEM_LIMIT)


def _rms(x):
    return x * lax.rsqrt(jnp.mean(x * x, axis=-1, keepdims=True) + EPS)


def _dot(a, b):
    return jnp.dot(a, b, preferred_element_type=F32)


def _dot_t(a, b):
    return lax.dot_general(a, b, (((1,), (1,)), ((), ())), preferred_element_type=F32)


def _layer_spec(arr, layer, block=None, index=None):
    block = tuple(arr.shape[1:]) if block is None else tuple(block)
    index = (0,) * len(block) if index is None else tuple(index)
    return pl.BlockSpec((None,) + block, lambda *g: (layer,) + index,
                        pipeline_mode=pl.Buffered(1))


def _rows(tm, cols, tiles_per_batch):
    return pl.BlockSpec((tm, cols), lambda b, s, *_: (b * tiles_per_batch + s, 0))


class _Mod:
    def __init__(self, arr, layer, per_batch):
        self.arr, self.layer, self.per_batch = arr, layer, per_batch

    def spec(self, role):
        layer, nb = self.layer, self.per_batch
        if nb:
            return pl.BlockSpec((None,) + self.arr.shape[1:],
                                lambda b, *_: ((layer * nb + b) * N_MOD + role, 0, 0))
        return pl.BlockSpec((None,) + self.arr.shape[1:],
                            lambda *_: (layer * N_MOD + role, 0, 0))


def _mod_kernel(c_ref, w_ref, b_ref, o_ref):
    c = c_ref[...]
    s = (c * jax.nn.sigmoid(c)).astype(BF16)
    o_ref[0] = _dot(s, w_ref[0].astype(BF16)) + b_ref[0]


def _modulation(c_all, w_ada, b_ada, tn=1024):
    depth, d, n = w_ada.shape
    r = c_all.shape[0]
    return pl.pallas_call(
        _mod_kernel,
        grid=(depth, n // tn),
        in_specs=[pl.BlockSpec((r, d), lambda l, j: (0, 0)),
                  pl.BlockSpec((1, d, tn), lambda l, j: (l, 0, j)),
                  pl.BlockSpec((1, 1, tn), lambda l, j: (l, 0, j))],
        out_specs=pl.BlockSpec((1, r, tn), lambda l, j: (l, 0, j)),
        out_shape=jax.ShapeDtypeStruct((depth, r, n), F32),
        compiler_params=_cparams(2),
        name="modulation",
    )(c_all, w_ada, b_ada.reshape(depth, 1, n))


def _rope128(p, cos, s1, s2):
    return p * cos + pltpu.roll(p, 96, 1) * s1 + pltpu.roll(p, 32, 1) * s2


def _attn_in_common(rs, x_ref, sh_ref, sc_ref, win_ref, qn_ref, wuq_ref, kvn_ref,
                    cos_ref, s1_ref, s2_ref):
    x = x_ref[rs, :]
    h = (_rms(x) * (1.0 + sc_ref[...]) + sh_ref[...]).astype(BF16)
    z = _dot(h, win_ref[...])
    cos, s1, s2 = cos_ref[rs, :], s1_ref[rs, :], s2_ref[rs, :]
    qn = (_rms(z[:, :Q_LORA]) * qn_ref[...]).astype(BF16)
    q = _dot(qn, wuq_ref[...]) * Q_SCALE
    ckv = _rms(z[:, Q_LORA:Q_LORA + KV_LORA]) * kvn_ref[...]
    kp = _rope128(z[:, Q_LORA + KV_LORA:], cos, s1, s2)
    rope = functools.partial(_rope128, cos=cos, s1=s1, s2=s2)
    return q, ckv, kp, rope


def _attn_in_prompt_kernel(x_ref, sh_ref, sc_ref, win_ref, qn_ref, wuq_ref, kvn_ref,
                           cos_ref, s1_ref, s2_ref, wuk_ref, wuv_ref, *rest, nsub):
    q_out, k_out, v_out, ckv_out, kpe_out = rest[-5:]
    rows = x_ref.shape[0] // nsub
    lane = lax.broadcasted_iota(jnp.int32, (rows, LANES), 1)
    ones_col = jnp.where(lane == 0, 1.0, 0.0).astype(BF16)
    for j in range(nsub):
        rs = slice(j * rows, (j + 1) * rows)
        q, ckv, kp, rope = _attn_in_common(rs, x_ref, sh_ref, sc_ref, win_ref, qn_ref, wuq_ref,
                                           kvn_ref, cos_ref, s1_ref, s2_ref)
        ckv_out[rs, :] = ckv
        kpe_out[:, rs] = kp.T[:ROPE_DIM, :]
        ckb = ckv.astype(BF16)
        kn = _dot(ckb, wuk_ref[...])
        v = _dot(ckb, wuv_ref[...]).astype(BF16)
        kpb = kp.astype(BF16)
        for hd in range(N_HEADS):
            c0 = hd * HEAD_W
            v_out[rs, c0:c0 + V_DIM] = v[:, hd * V_DIM:(hd + 1) * V_DIM]
            v_out[rs, c0 + V_DIM:c0 + HEAD_W] = ones_col
            q_out[rs, c0:c0 + NOPE_DIM] = q[:, c0:c0 + NOPE_DIM].astype(BF16)
            q_out[rs, c0 + NOPE_DIM:c0 + HEAD_W] = rope(q[:, c0 + NOPE_DIM:c0 + HEAD_W]).astype(BF16)
            k_out[rs, c0:c0 + NOPE_DIM] = kn[:, hd * NOPE_DIM:(hd + 1) * NOPE_DIM].astype(BF16)
            k_out[rs, c0 + NOPE_DIM:c0 + HEAD_W] = kpb


def _attn_in_sample_kernel(x_ref, sh_ref, sc_ref, win_ref, qn_ref, wuq_ref, kvn_ref,
                           cos_ref, s1_ref, s2_ref, wukt_ref,
                           qa_out, qpe_out, ckv_out, kpe_out, *, nb, t):
    q, ckv, kp, rope = _attn_in_common(slice(None), x_ref, sh_ref, sc_ref, win_ref, qn_ref,
                                       wuq_ref, kvn_ref, cos_ref, s1_ref, s2_ref)
    ckv_out[...] = ckv
    kpe_out[...] = kp[:, :ROPE_DIM]
    for hd in range(N_HEADS):
        c0 = hd * HEAD_W
        qa = _dot(q[:, c0:c0 + NOPE_DIM].astype(BF16), wukt_ref[hd])
        qa_out[:, hd] = qa.reshape(nb, t, KV_LORA).astype(BF16)
        qp = rope(q[:, c0 + NOPE_DIM:c0 + HEAD_W])
        qpe_out[:, hd] = qp.reshape(nb, t, ROPE_W).astype(BF16)


def _attn_in_prompt(x, mod, w, layer, tabs, nbatch, seq, stacks, tm=512, nsub=2):
    m, d = x.shape
    depth = w["w_in"].shape[0]
    nt = seq // tm
    tab_spec = pl.BlockSpec((tm, ROPE_W), lambda b, s: (s, 0))
    ins = [x, mod.arr, mod.arr, w["w_in"], w["q_norm"], w["w_uq"], w["kv_norm"],
           *tabs, w["w_uk"], w["w_uv"]]
    specs = [_rows(tm, d, nt), mod.spec(0), mod.spec(1),
             _layer_spec(w["w_in"], layer, (d, IN_A_W)),
             _layer_spec(w["q_norm"], layer), _layer_spec(w["w_uq"], layer),
             _layer_spec(w["kv_norm"], layer), tab_spec, tab_spec, tab_spec,
             _layer_spec(w["w_uk"], layer), _layer_spec(w["w_uv"], layer)]
    aliases = {}
    if stacks is not None:
        aliases = {len(ins): 3, len(ins) + 1: 4}
        ins += list(stacks)
        specs += [pl.BlockSpec(memory_space=pl.ANY)] * 2
    outs = [jax.ShapeDtypeStruct((m, N_HEADS * HEAD_W), BF16),
            jax.ShapeDtypeStruct((m, N_HEADS * HEAD_W), BF16),
            jax.ShapeDtypeStruct((m, N_HEADS * HEAD_W), BF16),
            jax.ShapeDtypeStruct((depth, m, KV_LORA), F32),
            jax.ShapeDtypeStruct((depth, nbatch, ROPE_DIM, seq), F32)]
    out_specs = [_rows(tm, N_HEADS * HEAD_W, nt), _rows(tm, N_HEADS * HEAD_W, nt),
                 _rows(tm, N_HEADS * HEAD_W, nt),
                 pl.BlockSpec((None, tm, KV_LORA), lambda b, s: (layer, b * nt + s, 0)),
                 pl.BlockSpec((None, None, ROPE_DIM, tm), lambda b, s: (layer, b, 0, s))]
    return pl.pallas_call(
        functools.partial(_attn_in_prompt_kernel, nsub=nsub), grid=(nbatch, nt),
        in_specs=specs, out_specs=out_specs, out_shape=outs, input_output_aliases=aliases,
        compiler_params=_cparams(2), name="attn_in_prompt",
    )(*ins)


def _attn_in_sample(x, mod, w, layer, tabs, nb, t):
    m, d = x.shape
    ins = [x, mod.arr, mod.arr, w["w_in"], w["q_norm"], w["w_uq"], w["kv_norm"],
           *tabs, w["w_uk_t"]]
    full = lambda a: pl.BlockSpec(a.shape, lambda *_: (0,) * a.ndim)
    specs = [_rows(m, d, 1), mod.spec(0), mod.spec(1),
             _layer_spec(w["w_in"], layer, (d, IN_A_W)),
             _layer_spec(w["q_norm"], layer), _layer_spec(w["w_uq"], layer),
             _layer_spec(w["kv_norm"], layer), full(tabs[0]), full(tabs[1]), full(tabs[2]),
             _layer_spec(w["w_uk_t"], layer)]
    outs = [jax.ShapeDtypeStruct((nb, N_HEADS, t, KV_LORA), BF16),
            jax.ShapeDtypeStruct((nb, N_HEADS, t, ROPE_W), BF16),
            jax.ShapeDtypeStruct((m, KV_LORA), F32),
            jax.ShapeDtypeStruct((m, ROPE_DIM), F32)]
    out_specs = [pl.BlockSpec((nb, N_HEADS, t, KV_LORA), lambda b, s: (0, 0, 0, 0)),
                 pl.BlockSpec((nb, N_HEADS, t, ROPE_W), lambda b, s: (0, 0, 0, 0)),
                 _rows(m, KV_LORA, 1), _rows(m, ROPE_DIM, 1)]
    return pl.pallas_call(
        functools.partial(_attn_in_sample_kernel, nb=nb, t=t), grid=(1, 1),
        in_specs=specs, out_specs=out_specs, out_shape=outs,
        compiler_params=_cparams(2), name="attn_in_sample",
    )(*ins)


def _conv_in_kernel(x_ref, sh_ref, sc_ref, w_ref, cw_ref, cn_ref, prev_ref,
                    o_ref, st_ref, buf, *, nb, t, dc, nsub):
    @pl.when(pl.program_id(1) == 0)
    def _():
        buf[:, CONV_PAD - 2:CONV_PAD, :] = prev_ref[...]

    tg = t // nsub
    for j in range(nsub):
        p0 = j * tg
        rs = slice(p0, p0 + tg) if nsub > 1 else slice(None)
        x = x_ref[rs, :]
        h = (_rms(x) * (1.0 + sc_ref[...]) + sh_ref[...]).astype(BF16)
        z = _dot(h, w_ref[...])
        gate_b = z[:, :dc]
        cu = z[:, dc:2 * dc] * z[:, 2 * dc:]
        buf[:, CONV_PAD + p0:CONV_PAD + p0 + tg, :] = cu.reshape(nb, tg, dc)
        base = CONV_PAD - (CONV_W - 1) + p0
        y = cw_ref[0:1, :].reshape(1, 1, dc) * buf[:, base:base + tg, :]
        for k in range(1, CONV_W):
            y = y + cw_ref[k:k + 1, :].reshape(1, 1, dc) * buf[:, base + k:base + k + tg, :]
        conv_o = gate_b * y.reshape(nb * tg, dc)
        o_ref[rs, :] = (_rms(conv_o) * cn_ref[...]).astype(BF16)
    last = buf[:, CONV_PAD + t - 2:CONV_PAD + t, :]
    st_ref[...] = last
    buf[:, CONV_PAD - 2:CONV_PAD, :] = last


def _conv_in(x, mod, w, layer, prev, prev_layer, nbatch, seq, nb, t, nsub=1):
    m, d = x.shape
    dc = w["conv_w"].shape[-1]
    nt = seq // t
    ins = [x, mod.arr, mod.arr, w["w_in_b"], w["conv_w"], w["conv_out_norm"], prev]
    specs = [_rows(nb * t, d, nt), mod.spec(0), mod.spec(1), _layer_spec(w["w_in_b"], layer),
             _layer_spec(w["conv_w"], layer), _layer_spec(w["conv_out_norm"], layer),
             pl.BlockSpec((None, nb, CONV_W - 1, dc), lambda b, s: (prev_layer, b, 0, 0))]
    outs = [jax.ShapeDtypeStruct((m, dc), BF16),
            jax.ShapeDtypeStruct(prev.shape[1:], F32)]
    out_specs = [_rows(nb * t, dc, nt),
                 pl.BlockSpec((nb, CONV_W - 1, dc), lambda b, s: (b, 0, 0))]
    return pl.pallas_call(
        functools.partial(_conv_in_kernel, nb=nb, t=t, dc=dc, nsub=nsub),
        grid=(nbatch // nb, nt), in_specs=specs, out_specs=out_specs, out_shape=outs,
        scratch_shapes=[pltpu.VMEM((nb, CONV_PAD + t, dc), F32)],
        compiler_params=_cparams(2), name="conv_in",
    )(*ins)


def _prompt_attn_kernel(q_ref, k_ref, v_ref, o_ref, s_buf, p_buf, *, seq, tq):
    row_chunk = lax.broadcasted_iota(jnp.int32, (tq, tq), 0) // CHUNK
    col_chunk = lax.broadcasted_iota(jnp.int32, (tq, tq), 1) // CHUNK
    visible = col_chunk <= row_chunk
    nq = seq // tq

    def scores(qi):
        hi = (qi + 1) * tq
        s_buf[qi % 2, :, 0:hi] = _dot_t(q_ref[qi * tq:hi, :], k_ref[0:hi, :])

    scores(0)
    for qi in range(nq):
        lo, hi = qi * tq, (qi + 1) * tq
        if qi + 1 < nq:
            scores(qi + 1)
        s_d = jnp.where(visible, s_buf[qi % 2, :, lo:hi], NEG_BIG)
        m = jnp.max(s_d, axis=-1, keepdims=True)
        if qi > 0:
            m = jnp.maximum(m, jnp.max(s_buf[qi % 2, :, 0:lo], axis=-1, keepdims=True))
            p_buf[:, 0:lo] = jnp.exp2(s_buf[qi % 2, :, 0:lo] - m).astype(BF16)
        p_buf[:, lo:hi] = jnp.exp2(s_d - m).astype(BF16)
        oe = _dot(p_buf[:, 0:hi], v_ref[0:hi, :])
        o_ref[lo:hi, :] = oe[:, :V_DIM] / oe[:, V_DIM:V_DIM + 1]


def _prompt_attention(q, k, v, nbatch, seq, tq=256):
    m = q.shape[0]
    return pl.pallas_call(
        functools.partial(_prompt_attn_kernel, seq=seq, tq=tq),
        grid=(nbatch, N_HEADS),
        in_specs=[pl.BlockSpec((seq, HEAD_W), lambda b, h: (b, h)),
                  pl.BlockSpec((seq, HEAD_W), lambda b, h: (b, h)),
                  pl.BlockSpec((seq, HEAD_W), lambda b, h: (b, h))],
        out_specs=pl.BlockSpec((seq, V_DIM), lambda b, h: (b, h)),
        out_shape=jax.ShapeDtypeStruct((m, D_ATTN), F32),
        scratch_shapes=[pltpu.VMEM((2, tq, seq), F32), pltpu.VMEM((tq, seq), BF16)],
        compiler_params=_cparams(2), name="prompt_attention",
    )(q, k, v)


def _sample_attn_kernel(qa_ref, qpe_ref, pc_ref, pk_ref, nc_ref, nk_ref, wuv_ref, o_ref, *, t):
    rows = N_HEADS * t
    qa = qa_ref[0].reshape(rows, KV_LORA)
    qp = qpe_ref[0].reshape(rows, ROPE_W)[:, :ROPE_DIM]
    pc = pc_ref[0].astype(BF16)
    pk_t = pk_ref[0].astype(BF16)
    nc = nc_ref[...].astype(BF16)
    nk = nk_ref[...].astype(BF16)
    s_p = _dot_t(qa, pc) + _dot(qp, pk_t)
    s_n = _dot_t(qa, nc) + _dot_t(qp, nk)
    m = jnp.maximum(jnp.max(s_p, axis=-1, keepdims=True), jnp.max(s_n, axis=-1, keepdims=True))
    p_p = jnp.exp2(s_p - m)
    p_n = jnp.exp2(s_n - m)
    l = jnp.sum(p_p, axis=-1, keepdims=True) + jnp.sum(p_n, axis=-1, keepdims=True)
    o_lat = (_dot(p_p.astype(BF16), pc) + _dot(p_n.astype(BF16), nc)) / l
    o_lat = o_lat.astype(BF16)
    for hd in range(N_HEADS):
        o_ref[:, hd * V_DIM:(hd + 1) * V_DIM] = _dot(o_lat[hd * t:(hd + 1) * t, :], wuv_ref[hd])


def _sample_attention(qa, qpe, past_ckv, past_kpe_t, new_ckv, new_kpe, w_uv_h, layer):
    nb, _, t, _ = qa.shape
    past = past_ckv.shape[2]
    return pl.pallas_call(
        functools.partial(_sample_attn_kernel, t=t),
        grid=(nb,),
        in_specs=[pl.BlockSpec((1, N_HEADS, t, KV_LORA), lambda b: (b, 0, 0, 0)),
                  pl.BlockSpec((1, N_HEADS, t, ROPE_W), lambda b: (b, 0, 0, 0)),
                  pl.BlockSpec((None, 1, past, KV_LORA), lambda b: (layer, b, 0, 0)),
                  pl.BlockSpec((None, 1, ROPE_DIM, past), lambda b: (layer, b, 0, 0)),
                  pl.BlockSpec((t, KV_LORA), lambda b: (b, 0)),
                  pl.BlockSpec((t, ROPE_DIM), lambda b: (b, 0)),
                  _layer_spec(w_uv_h, layer)],
        out_specs=pl.BlockSpec((t, D_ATTN), lambda b: (b, 0)),
        out_shape=jax.ShapeDtypeStruct((nb * t, D_ATTN), F32),
        compiler_params=_cparams(1), name="sample_attention",
    )(qa, qpe, past_ckv, past_kpe_t, new_ckv, new_kpe, w_uv_h)


def _out_proj_kernel(x_ref, attn_ref, conv_ref, g_ref, shm_ref, scm_ref, an_ref, woa_ref, woc_ref,
                     o_ref, h_ref, *, nsub):
    rows = x_ref.shape[0] // nsub
    for j in range(nsub):
        rs = slice(j * rows, (j + 1) * rows) if nsub > 1 else slice(None)
        a = (_rms(attn_ref[rs, :]) * an_ref[...]).astype(BF16)
        mix = _dot(a, woa_ref[...]) + _dot(conv_ref[rs, :], woc_ref[...])
        y = x_ref[rs, :] + g_ref[...] * mix
        o_ref[rs, :] = y
        h_ref[rs, :] = (_rms(y) * (1.0 + scm_ref[...]) + shm_ref[...]).astype(BF16)


def _out_proj(x, attn, conv_n, mod, w, layer, nbatch, seq, tm, nsub=1):
    m, d = x.shape
    nt = seq // tm
    da, dc = attn.shape[1], conv_n.shape[1]
    ins = [x, attn, conv_n, mod.arr, mod.arr, mod.arr, w["attn_out_norm"], w["w_out"], w["w_out"]]
    specs = [_rows(tm, d, nt), _rows(tm, da, nt), _rows(tm, dc, nt),
             mod.spec(2), mod.spec(3), mod.spec(4),
             _layer_spec(w["attn_out_norm"], layer),
             _layer_spec(w["w_out"], layer, (da, d), (0, 0)),
             _layer_spec(w["w_out"], layer, (dc, d), (da // dc, 0))]
    return pl.pallas_call(
        functools.partial(_out_proj_kernel, nsub=nsub), grid=(nbatch, nt), in_specs=specs,
        out_specs=[_rows(tm, d, nt), _rows(tm, d, nt)],
        out_shape=[jax.ShapeDtypeStruct((m, d), F32), jax.ShapeDtypeStruct((m, d), BF16)],
        compiler_params=_cparams(2), name="out_proj",
    )(*ins)


def _mlp_kernel(x_ref, h_ref, g_ref, wup_ref, wdn_ref, fn_ref, o_ref, *, nf, final):
    f = pl.program_id(2)

    def partial_down():
        up = _dot(h_ref[...], wup_ref[...])
        act = jnp.square(jnp.maximum(up, 0.0)).astype(BF16)
        return _dot(act, wdn_ref[...])

    @pl.when(f == 0)
    def _():
        o_ref[...] = partial_down()

    @pl.when(f > 0)
    def _():
        o_ref[...] += partial_down()

    @pl.when(f == nf - 1)
    def _():
        y = x_ref[...] + g_ref[...] * o_ref[...]
        if final:
            y = _rms(y) * fn_ref[...]
        o_ref[...] = y


def _mlp(x, h, mod, w, layer, final_norm, nbatch, seq, tm, tf, final):
    m, d = x.shape
    nt = seq // tm
    dff = w["w_up"].shape[-1]
    nf = dff // tf
    return pl.pallas_call(
        functools.partial(_mlp_kernel, nf=nf, final=final),
        grid=(nbatch, nt, nf),
        in_specs=[_rows(tm, d, nt), _rows(tm, d, nt), mod.spec(5),
                  pl.BlockSpec((None, d, tf), lambda b, s, f: (layer, 0, f)),
                  pl.BlockSpec((None, tf, d), lambda b, s, f: (layer, f, 0)),
                  pl.BlockSpec(final_norm.shape, lambda b, s, f: (0, 0))],
        out_specs=_rows(tm, d, nt), out_shape=jax.ShapeDtypeStruct((m, d), F32),
        compiler_params=_cparams(3), name="mlp",
    )(x, h, mod.arr, w["w_up"], w["w_down"], final_norm)


def _rope_tables(pos):
    half = ROPE_DIM // 2
    inv = ROPE_BASE ** (-jnp.arange(half, dtype=F32) / half)
    ang = pos.astype(F32)[:, None] * inv[None, :]
    cos, sin, zero = jnp.cos(ang), jnp.sin(ang), jnp.zeros_like(ang)
    return (jnp.concatenate([cos, cos, zero, zero], axis=-1),
            jnp.concatenate([-sin, zero, zero, zero], axis=-1),
            jnp.concatenate([zero, sin, zero, zero], axis=-1))


def _prep_weights(w_in, q_norm, w_uq, kv_norm, w_ukv, conv_w, attn_out_norm, conv_out_norm,
                  w_out, w_up, w_down):
    depth = w_in.shape[0]
    w_in = w_in.astype(BF16)
    wq = w_uq.astype(BF16).reshape(depth, Q_LORA, N_HEADS, QK_DIM)
    wq = jnp.pad(wq, ((0, 0), (0, 0), (0, 0), (0, HEAD_W - QK_DIM)))
    wkv = w_ukv.astype(BF16).reshape(depth, KV_LORA, N_HEADS, NOPE_DIM + V_DIM)
    w_uk = wkv[..., :NOPE_DIM]
    w_uv = wkv[..., NOPE_DIM:]
    return dict(
        w_in=w_in,
        w_in_b=w_in[:, :, Q_LORA + KV_LORA + ROPE_DIM:],
        w_uq=wq.reshape(depth, Q_LORA, N_HEADS * HEAD_W),
        w_uk=w_uk.reshape(depth, KV_LORA, N_HEADS * NOPE_DIM),
        w_uv=w_uv.reshape(depth, KV_LORA, N_HEADS * V_DIM),
        w_uk_t=jnp.transpose(w_uk, (0, 2, 3, 1)),
        w_uv_h=jnp.transpose(w_uv, (0, 2, 1, 3)),
        q_norm=q_norm[:, None, :], kv_norm=kv_norm[:, None, :],
        conv_w=conv_w, attn_out_norm=attn_out_norm[:, None, :],
        conv_out_norm=conv_out_norm[:, None, :],
        w_out=w_out.astype(BF16), w_up=w_up.astype(BF16), w_down=w_down.astype(BF16))


def kernel(x_prompt, x_sample, c_prompt, c_sample, cache_kv_latent, cache_k_rope, state_conv,
           w_ada, b_ada, w_in, q_norm, w_uq, kv_norm, w_ukv, conv_w,
           attn_out_norm, conv_out_norm, w_out, w_up, w_down, final_norm):
    bp, sp, d = x_prompt.shape
    bs, ss, _ = x_sample.shape
    depth = w_ada.shape[0]
    past = cache_kv_latent.shape[2]
    dc = conv_w.shape[-1]
    ms = bs * ss

    w = _prep_weights(w_in, q_norm, w_uq, kv_norm, w_ukv, conv_w, attn_out_norm,
                      conv_out_norm, w_out, w_up, w_down)
    fnorm = final_norm[None, :]
    tabs_p = _rope_tables(jnp.arange(sp))
    tabs_s = tuple(jnp.tile(tb, (bs, 1)) for tb in _rope_tables(past + jnp.arange(ss)))

    mod = _modulation(jnp.concatenate([c_prompt, c_sample], axis=0), w_ada, b_ada)
    mod = mod.reshape(depth, bp + bs, N_MOD, d)
    mod_p = mod[:, :bp].reshape(depth * bp * N_MOD, 1, d)
    mod_s = jnp.repeat(jnp.swapaxes(mod[:, bp:], 1, 2), ss, axis=2).reshape(depth * N_MOD, ms, d)
    zero_conv = jnp.zeros((1, bp, CONV_W - 1, dc), F32)

    xp = x_prompt.reshape(bp * sp, d)
    xs = x_sample.reshape(ms, d)
    past_kpe_t = jnp.swapaxes(cache_k_rope, 2, 3)
    stacks = None
    outs = [[] for _ in range(4)]
    for l in range(depth):
        last = l == depth - 1
        mp = _Mod(mod_p, l, bp)
        msm = _Mod(mod_s, l, 0)

        q, k, v, *stacks = _attn_in_prompt(xp, mp, w, l, tabs_p, bp, sp, stacks)
        conv_n, cstate = _conv_in(xp, mp, w, l, zero_conv, 0, bp, sp, nb=1, t=512, nsub=2)
        attn = _prompt_attention(q, k, v, bp, sp)
        xp, hp = _out_proj(xp, attn, conv_n, mp, w, l, bp, sp, tm=512, nsub=2)
        xp = _mlp(xp, hp, mp, w, l, fnorm, bp, sp, tm=1024, tf=512, final=last)
        outs[0].append(cstate)

        qa, qpe, ckv_s, kpe_s = _attn_in_sample(xs, msm, w, l, tabs_s, bs, ss)
        conv_ns, cstate_s = _conv_in(xs, msm, w, l, state_conv, l, bs, ss, nb=bs, t=ss)
        attn_s = _sample_attention(qa, qpe, cache_kv_latent, past_kpe_t, ckv_s, kpe_s,
                                   w["w_uv_h"], l)
        xs, hs = _out_proj(xs, attn_s, conv_ns, msm, w, l, 1, ms, tm=ms)
        xs = _mlp(xs, hs, msm, w, l, fnorm, 1, ms, tm=ms, tf=512, final=last)
        outs[1].append(ckv_s.reshape(bs, ss, KV_LORA))
        outs[2].append(kpe_s.reshape(bs, ss, ROPE_DIM))
        outs[3].append(cstate_s)

    return (xp.reshape(bp, sp, d), xs.reshape(bs, ss, d),
            stacks[0].reshape(depth, bp, sp, KV_LORA), jnp.swapaxes(stacks[1], 2, 3),
            jnp.stack(outs[0]), jnp.stack(outs[1]), jnp.stack(outs[2]), jnp.stack(outs[3]))
```

```python
import functools
import math

import jax
import jax.numpy as jnp
from jax import lax
from jax.experimental import pallas as pl
from jax.experimental.pallas import tpu as pltpu

F32 = jnp.float32
BF16 = jnp.bfloat16

CHUNK = 64
N_HEADS = 8
NOPE_DIM = 128
ROPE_DIM = 64
V_DIM = 128
QK_DIM = NOPE_DIM + ROPE_DIM
Q_LORA = 512
KV_LORA = 512
D_ATTN = N_HEADS * V_DIM
CONV_W = 3
N_MOD = 6
ROPE_BASE = 10000.0
EPS = 1e-6
ATTN_SCALE = QK_DIM ** -0.5

LANES = 128
HEAD_W = 2 * LANES
ROPE_W = LANES
IN_A_W = Q_LORA + KV_LORA + ROPE_W
CONV_PAD = 8
Q_SCALE = ATTN_SCALE * math.log2(math.e)
NEG_BIG = -1e30
VMEM_LIMIT = 60000 * 1024


def _cparams(n_axes):
    return pltpu.CompilerParams(dimension_semantics=("arbitrary",) * n_axes,
                                vmem_limit_bytes=VMEM_LIMIT)


def _rms(x):
    return x * lax.rsqrt(jnp.mean(x * x, axis=-1, keepdims=True) + EPS)


def _dot(a, b):
    return jnp.dot(a, b, preferred_element_type=F32)


def _dot_t(a, b):
    return lax.dot_general(a, b, (((1,), (1,)), ((), ())), preferred_element_type=F32)


def _layer_spec(arr, layer, block=None, index=None):
    block = tuple(arr.shape[1:]) if block is None else tuple(block)
    index = (0,) * len(block) if index is None else tuple(index)
    return pl.BlockSpec((None,) + block, lambda *g: (layer,) + index,
                        pipeline_mode=pl.Buffered(1))


def _rows(tm, cols, tiles_per_batch):
    return pl.BlockSpec((tm, cols), lambda b, s, *_: (b * tiles_per_batch + s, 0))


class _Mod:
    def __init__(self, arr, layer, per_batch):
        self.arr, self.layer, self.per_batch = arr, layer, per_batch

    def spec(self, role):
        layer, nb = self.layer, self.per_batch
        if nb:
            return pl.BlockSpec((None,) + self.arr.shape[1:],
                                lambda b, *_: ((layer * nb + b) * N_MOD + role, 0, 0))
        return pl.BlockSpec((None,) + self.arr.shape[1:],
                            lambda *_: (layer * N_MOD + role, 0, 0))


def _mod_kernel(c_ref, w_ref, b_ref, o_ref):
    c = c_ref[...]
    s = (c * jax.nn.sigmoid(c)).astype(BF16)
    o_ref[0] = _dot(s, w_ref[0].astype(BF16)) + b_ref[0]


def _modulation(c_all, w_ada, b_ada, tn=1024):
    depth, d, n = w_ada.shape
    r = c_all.shape[0]
    return pl.pallas_call(
        _mod_kernel,
        grid=(depth, n // tn),
        in_specs=[pl.BlockSpec((r, d), lambda l, j: (0, 0)),
                  pl.BlockSpec((1, d, tn), lambda l, j: (l, 0, j)),
                  pl.BlockSpec((1, 1, tn), lambda l, j: (l, 0, j))],
        out_specs=pl.BlockSpec((1, r, tn), lambda l, j: (l, 0, j)),
        out_shape=jax.ShapeDtypeStruct((depth, r, n), F32),
        compiler_params=_cparams(2),
        name="modulation",
    )(c_all, w_ada, b_ada.reshape(depth, 1, n))


def _rope128(p, cos, s1, s2):
    return p * cos + pltpu.roll(p, 96, 1) * s1 + pltpu.roll(p, 32, 1) * s2


def _attn_in_common(rs, x_ref, sh_ref, sc_ref, win_ref, qn_ref, wuq_ref, kvn_ref,
                    cos_ref, s1_ref, s2_ref):
    x = x_ref[rs, :]
    h = (_rms(x) * (1.0 + sc_ref[...]) + sh_ref[...]).astype(BF16)
    z = _dot(h, win_ref[...])
    cos, s1, s2 = cos_ref[rs, :], s1_ref[rs, :], s2_ref[rs, :]
    qn = (_rms(z[:, :Q_LORA]) * qn_ref[...]).astype(BF16)
    q = _dot(qn, wuq_ref[...]) * Q_SCALE
    ckv = _rms(z[:, Q_LORA:Q_LORA + KV_LORA]) * kvn_ref[...]
    kp = _rope128(z[:, Q_LORA + KV_LORA:], cos, s1, s2)
    rope = functools.partial(_rope128, cos=cos, s1=s1, s2=s2)
    return q, ckv, kp, rope


def _attn_in_prompt_kernel(x_ref, sh_ref, sc_ref, win_ref, qn_ref, wuq_ref, kvn_ref,
                           cos_ref, s1_ref, s2_ref, wuk_ref, wuv_ref, *rest, nsub):
    q_out, k_out, v_out, ckv_out, kpe_out = rest[-5:]
    rows = x_ref.shape[0] // nsub
    lane = lax.broadcasted_iota(jnp.int32, (rows, LANES), 1)
    ones_col = jnp.where(lane == 0, 1.0, 0.0).astype(BF16)
    for j in range(nsub):
        rs = slice(j * rows, (j + 1) * rows)
        q, ckv, kp, rope = _attn_in_common(rs, x_ref, sh_ref, sc_ref, win_ref, qn_ref, wuq_ref,
                                           kvn_ref, cos_ref, s1_ref, s2_ref)
        ckv_out[rs, :] = ckv
        kpe_out[:, rs] = kp.T[:ROPE_DIM, :]
        ckb = ckv.astype(BF16)
        kn = _dot(ckb, wuk_ref[...])
        v = _dot(ckb, wuv_ref[...]).astype(BF16)
        kpb = kp.astype(BF16)
        for hd in range(N_HEADS):
            c0 = hd * HEAD_W
            v_out[rs, c0:c0 + V_DIM] = v[:, hd * V_DIM:(hd + 1) * V_DIM]
            v_out[rs, c0 + V_DIM:c0 + HEAD_W] = ones_col
            q_out[rs, c0:c0 + NOPE_DIM] = q[:, c0:c0 + NOPE_DIM].astype(BF16)
            q_out[rs, c0 + NOPE_DIM:c0 + HEAD_W] = rope(q[:, c0 + NOPE_DIM:c0 + HEAD_W]).astype(BF16)
            k_out[rs, c0:c0 + NOPE_DIM] = kn[:, hd * NOPE_DIM:(hd + 1) * NOPE_DIM].astype(BF16)
            k_out[rs, c0 + NOPE_DIM:c0 + HEAD_W] = kpb


def _attn_in_sample_kernel(x_ref, sh_ref, sc_ref, win_ref, qn_ref, wuq_ref, kvn_ref,
                           cos_ref, s1_ref, s2_ref, wukt_ref,
                           qa_out, qpe_out, ckv_out, kpe_out, *, nb, t):
    q, ckv, kp, rope = _attn_in_common(slice(None), x_ref, sh_ref, sc_ref, win_ref, qn_ref,
                                       wuq_ref, kvn_ref, cos_ref, s1_ref, s2_ref)
    ckv_out[...] = ckv
    kpe_out[...] = kp[:, :ROPE_DIM]
    for hd in range(N_HEADS):
        c0 = hd * HEAD_W
        qa = _dot(q[:, c0:c0 + NOPE_DIM].astype(BF16), wukt_ref[hd])
        qa_out[:, hd] = qa.reshape(nb, t, KV_LORA).astype(BF16)
        qp = rope(q[:, c0 + NOPE_DIM:c0 + HEAD_W])
        qpe_out[:, hd] = qp.reshape(nb, t, ROPE_W).astype(BF16)


def _attn_in_prompt(x, mod, w, layer, tabs, nbatch, seq, stacks, tm=512, nsub=2):
    m, d = x.shape
    depth = w["w_in"].shape[0]
    nt = seq // tm
    tab_spec = pl.BlockSpec((tm, ROPE_W), lambda b, s: (s, 0))
    ins = [x, mod.arr, mod.arr, w["w_in"], w["q_norm"], w["w_uq"], w["kv_norm"],
           *tabs, w["w_uk"], w["w_uv"]]
    specs = [_rows(tm, d, nt), mod.spec(0), mod.spec(1),
             _layer_spec(w["w_in"], layer, (d, IN_A_W)),
             _layer_spec(w["q_norm"], layer), _layer_spec(w["w_uq"], layer),
             _layer_spec(w["kv_norm"], layer), tab_spec, tab_spec, tab_spec,
             _layer_spec(w["w_uk"], layer), _layer_spec(w["w_uv"], layer)]
    aliases = {}
    if stacks is not None:
        aliases = {len(ins): 3, len(ins) + 1: 4}
        ins += list(stacks)
        specs += [pl.BlockSpec(memory_space=pl.ANY)] * 2
    outs = [jax.ShapeDtypeStruct((m, N_HEADS * HEAD_W), BF16),
            jax.ShapeDtypeStruct((m, N_HEADS * HEAD_W), BF16),
            jax.ShapeDtypeStruct((m, N_HEADS * HEAD_W), BF16),
            jax.ShapeDtypeStruct((depth, m, KV_LORA), F32),
            jax.ShapeDtypeStruct((depth, nbatch, ROPE_DIM, seq), F32)]
    out_specs = [_rows(tm, N_HEADS * HEAD_W, nt), _rows(tm, N_HEADS * HEAD_W, nt),
                 _rows(tm, N_HEADS * HEAD_W, nt),
                 pl.BlockSpec((None, tm, KV_LORA), lambda b, s: (layer, b * nt + s, 0)),
                 pl.BlockSpec((None, None, ROPE_DIM, tm), lambda b, s: (layer, b, 0, s))]
    return pl.pallas_call(
        functools.partial(_attn_in_prompt_kernel, nsub=nsub), grid=(nbatch, nt),
        in_specs=specs, out_specs=out_specs, out_shape=outs, input_output_aliases=aliases,
        compiler_params=_cparams(2), name="attn_in_prompt",
    )(*ins)


def _attn_in_sample(x, mod, w, layer, tabs, nb, t):
    m, d = x.shape
    ins = [x, mod.arr, mod.arr, w["w_in"], w["q_norm"], w["w_uq"], w["kv_norm"],
           *tabs, w["w_uk_t"]]
    full = lambda a: pl.BlockSpec(a.shape, lambda *_: (0,) * a.ndim)
    specs = [_rows(m, d, 1), mod.spec(0), mod.spec(1),
             _layer_spec(w["w_in"], layer, (d, IN_A_W)),
             _layer_spec(w["q_norm"], layer), _layer_spec(w["w_uq"], layer),
             _layer_spec(w["kv_norm"], layer), full(tabs[0]), full(tabs[1]), full(tabs[2]),
             _layer_spec(w["w_uk_t"], layer)]
    outs = [jax.ShapeDtypeStruct((nb, N_HEADS, t, KV_LORA), BF16),
            jax.ShapeDtypeStruct((nb, N_HEADS, t, ROPE_W), BF16),
            jax.ShapeDtypeStruct((m, KV_LORA), F32),
            jax.ShapeDtypeStruct((m, ROPE_DIM), F32)]
    out_specs = [pl.BlockSpec((nb, N_HEADS, t, KV_LORA), lambda b, s: (0, 0, 0, 0)),
                 pl.BlockSpec((nb, N_HEADS, t, ROPE_W), lambda b, s: (0, 0, 0, 0)),
                 _rows(m, KV_LORA, 1), _rows(m, ROPE_DIM, 1)]
    return pl.pallas_call(
        functools.partial(_attn_in_sample_kernel, nb=nb, t=t), grid=(1, 1),
        in_specs=specs, out_specs=out_specs, out_shape=outs,
        compiler_params=_cparams(2), name="attn_in_sample",
    )(*ins)


def _conv_in_kernel(x_ref, sh_ref, sc_ref, w_ref, cw_ref, cn_ref, prev_ref,
                    o_ref, st_ref, buf, *, nb, t, dc, nsub):
    @pl.when(pl.program_id(1) == 0)
    def _():
        buf[:, CONV_PAD - 2:CONV_PAD, :] = prev_ref[...]

    tg = t // nsub
    for j in range(nsub):
        p0 = j * tg
        rs = slice(p0, p0 + tg) if nsub > 1 else slice(None)
        x = x_ref[rs, :]
        h = (_rms(x) * (1.0 + sc_ref[...]) + sh_ref[...]).astype(BF16)
        z = _dot(h, w_ref[...])
        gate_b = z[:, :dc]
        cu = z[:, dc:2 * dc] * z[:, 2 * dc:]
        buf[:, CONV_PAD + p0:CONV_PAD + p0 + tg, :] = cu.reshape(nb, tg, dc)
        base = CONV_PAD - (CONV_W - 1) + p0
        y = cw_ref[0:1, :].reshape(1, 1, dc) * buf[:, base:base + tg, :]
        for k in range(1, CONV_W):
            y = y + cw_ref[k:k + 1, :].reshape(1, 1, dc) * buf[:, base + k:base + k + tg, :]
        conv_o = gate_b * y.reshape(nb * tg, dc)
        o_ref[rs, :] = (_rms(conv_o) * cn_ref[...]).astype(BF16)
    last = buf[:, CONV_PAD + t - 2:CONV_PAD + t, :]
    st_ref[...] = last
    buf[:, CONV_PAD - 2:CONV_PAD, :] = last


def _conv_in(x, mod, w, layer, prev, prev_layer, nbatch, seq, nb, t, nsub=1):
    m, d = x.shape
    dc = w["conv_w"].shape[-1]
    nt = seq // t
    ins = [x, mod.arr, mod.arr, w["w_in_b"], w["conv_w"], w["conv_out_norm"], prev]
    specs = [_rows(nb * t, d, nt), mod.spec(0), mod.spec(1), _layer_spec(w["w_in_b"], layer),
             _layer_spec(w["conv_w"], layer), _layer_spec(w["conv_out_norm"], layer),
             pl.BlockSpec((None, nb, CONV_W - 1, dc), lambda b, s: (prev_layer, b, 0, 0))]
    outs = [jax.ShapeDtypeStruct((m, dc), BF16),
            jax.ShapeDtypeStruct(prev.shape[1:], F32)]
    out_specs = [_rows(nb * t, dc, nt),
                 pl.BlockSpec((nb, CONV_W - 1, dc), lambda b, s: (b, 0, 0))]
    return pl.pallas_call(
        functools.partial(_conv_in_kernel, nb=nb, t=t, dc=dc, nsub=nsub),
        grid=(nbatch // nb, nt), in_specs=specs, out_specs=out_specs, out_shape=outs,
        scratch_shapes=[pltpu.VMEM((nb, CONV_PAD + t, dc), F32)],
        compiler_params=_cparams(2), name="conv_in",
    )(*ins)


def _prompt_attn_kernel(q_ref, k_ref, v_ref, o_ref, s_buf, p_buf, *, seq, tq):
    row_chunk = lax.broadcasted_iota(jnp.int32, (tq, tq), 0) // CHUNK
    col_chunk = lax.broadcasted_iota(jnp.int32, (tq, tq), 1) // CHUNK
    visible = col_chunk <= row_chunk
    nq = seq // tq

    def scores(qi):
        hi = (qi + 1) * tq
        s_buf[qi % 2, :, 0:hi] = _dot_t(q_ref[qi * tq:hi, :], k_ref[0:hi, :])

    scores(0)
    for qi in range(nq):
        lo, hi = qi * tq, (qi + 1) * tq
        if qi + 1 < nq:
            scores(qi + 1)
        s_d = jnp.where(visible, s_buf[qi % 2, :, lo:hi], NEG_BIG)
        m = jnp.max(s_d, axis=-1, keepdims=True)
        if qi > 0:
            m = jnp.maximum(m, jnp.max(s_buf[qi % 2, :, 0:lo], axis=-1, keepdims=True))
            p_buf[:, 0:lo] = jnp.exp2(s_buf[qi % 2, :, 0:lo] - m).astype(BF16)
        p_buf[:, lo:hi] = jnp.exp2(s_d - m).astype(BF16)
        oe = _dot(p_buf[:, 0:hi], v_ref[0:hi, :])
        o_ref[lo:hi, :] = oe[:, :V_DIM] / oe[:, V_DIM:V_DIM + 1]


def _prompt_attention(q, k, v, nbatch, seq, tq=256):
    m = q.shape[0]
    return pl.pallas_call(
        functools.partial(_prompt_attn_kernel, seq=seq, tq=tq),
        grid=(nbatch, N_HEADS),
        in_specs=[pl.BlockSpec((seq, HEAD_W), lambda b, h: (b, h)),
                  pl.BlockSpec((seq, HEAD_W), lambda b, h: (b, h)),
                  pl.BlockSpec((seq, HEAD_W), lambda b, h: (b, h))],
        out_specs=pl.BlockSpec((seq, V_DIM), lambda b, h: (b, h)),
        out_shape=jax.ShapeDtypeStruct((m, D_ATTN), F32),
        scratch_shapes=[pltpu.VMEM((2, tq, seq), F32), pltpu.VMEM((tq, seq), BF16)],
        compiler_params=_cparams(2), name="prompt_attention",
    )(q, k, v)


def _sample_attn_kernel(qa_ref, qpe_ref, pc_ref, pk_ref, nc_ref, nk_ref, wuv_ref, o_ref, *, t):
    rows = N_HEADS * t
    qa = qa_ref[0].reshape(rows, KV_LORA)
    qp = qpe_ref[0].reshape(rows, ROPE_W)[:, :ROPE_DIM]
    pc = pc_ref[0].astype(BF16)
    pk_t = pk_ref[0].astype(BF16)
    nc = nc_ref[...].astype(BF16)
    nk = nk_ref[...].astype(BF16)
    s_p = _dot_t(qa, pc) + _dot(qp, pk_t)
    s_n = _dot_t(qa, nc) + _dot_t(qp, nk)
    m = jnp.maximum(jnp.max(s_p, axis=-1, keepdims=True), jnp.max(s_n, axis=-1, keepdims=True))
    p_p = jnp.exp2(s_p - m)
    p_n = jnp.exp2(s_n - m)
    l = jnp.sum(p_p, axis=-1, keepdims=True) + jnp.sum(p_n, axis=-1, keepdims=True)
    o_lat = (_dot(p_p.astype(BF16), pc) + _dot(p_n.astype(BF16), nc)) / l
    o_lat = o_lat.astype(BF16)
    for hd in range(N_HEADS):
        o_ref[:, hd * V_DIM:(hd + 1) * V_DIM] = _dot(o_lat[hd * t:(hd + 1) * t, :], wuv_ref[hd])


def _sample_attention(qa, qpe, past_ckv, past_kpe_t, new_ckv, new_kpe, w_uv_h, layer):
    nb, _, t, _ = qa.shape
    past = past_ckv.shape[2]
    return pl.pallas_call(
        functools.partial(_sample_attn_kernel, t=t),
        grid=(nb,),
        in_specs=[pl.BlockSpec((1, N_HEADS, t, KV_LORA), lambda b: (b, 0, 0, 0)),
                  pl.BlockSpec((1, N_HEADS, t, ROPE_W), lambda b: (b, 0, 0, 0)),
                  pl.BlockSpec((None, 1, past, KV_LORA), lambda b: (layer, b, 0, 0)),
                  pl.BlockSpec((None, 1, ROPE_DIM, past), lambda b: (layer, b, 0, 0)),
                  pl.BlockSpec((t, KV_LORA), lambda b: (b, 0)),
                  pl.BlockSpec((t, ROPE_DIM), lambda b: (b, 0)),
                  _layer_spec(w_uv_h, layer)],
        out_specs=pl.BlockSpec((t, D_ATTN), lambda b: (b, 0)),
        out_shape=jax.ShapeDtypeStruct((nb * t, D_ATTN), F32),
        compiler_params=_cparams(1), name="sample_attention",
    )(qa, qpe, past_ckv, past_kpe_t, new_ckv, new_kpe, w_uv_h)


def _out_proj_kernel(x_ref, attn_ref, conv_ref, g_ref, shm_ref, scm_ref, an_ref, woa_ref, woc_ref,
                     o_ref, h_ref, *bf16_w_out, nsub):
    if bf16_w_out:
        wo_ref, = bf16_w_out
        da = woa_ref.shape[0]
        wo_ref[0:da, :] = woa_ref[...].astype(BF16)
        wo_ref[da:, :] = woc_ref[...].astype(BF16)
        woa, woc = wo_ref[0:da, :], wo_ref[da:, :]
    else:
        woa, woc = woa_ref[...], woc_ref[...]
    rows = x_ref.shape[0] // nsub
    for j in range(nsub):
        rs = slice(j * rows, (j + 1) * rows) if nsub > 1 else slice(None)
        a = (_rms(attn_ref[rs, :]) * an_ref[...]).astype(BF16)
        mix = _dot(a, woa) + _dot(conv_ref[rs, :], woc)
        y = x_ref[rs, :] + g_ref[...] * mix
        o_ref[rs, :] = y
        h_ref[rs, :] = (_rms(y) * (1.0 + scm_ref[...]) + shm_ref[...]).astype(BF16)


def _out_proj(x, attn, conv_n, mod, w, w_out, layer, nbatch, seq, tm, nsub=1):
    m, d = x.shape
    nt = seq // tm
    da, dc = attn.shape[1], conv_n.shape[1]
    ins = [x, attn, conv_n, mod.arr, mod.arr, mod.arr, w["attn_out_norm"], w_out, w_out]
    specs = [_rows(tm, d, nt), _rows(tm, da, nt), _rows(tm, dc, nt),
             mod.spec(2), mod.spec(3), mod.spec(4),
             _layer_spec(w["attn_out_norm"], mod.layer)]
    out_specs = [_rows(tm, d, nt), _rows(tm, d, nt)]
    out_shape = [jax.ShapeDtypeStruct((m, d), F32), jax.ShapeDtypeStruct((m, d), BF16)]
    if layer is None:
        specs += [pl.BlockSpec((da, d), lambda b, s: (0, 0), pipeline_mode=pl.Buffered(1)),
                  pl.BlockSpec((dc, d), lambda b, s: (da // dc, 0), pipeline_mode=pl.Buffered(1))]
    else:
        assert nbatch * nt == 1
        specs += [_layer_spec(w_out, layer, (da, d), (0, 0)),
                  _layer_spec(w_out, layer, (dc, d), (da // dc, 0))]
        out_specs.append(pl.BlockSpec((da + dc, d), lambda b, s: (0, 0)))
        out_shape.append(jax.ShapeDtypeStruct((da + dc, d), BF16))
    return pl.pallas_call(
        functools.partial(_out_proj_kernel, nsub=nsub), grid=(nbatch, nt), in_specs=specs,
        out_specs=out_specs, out_shape=out_shape,
        compiler_params=_cparams(2), name="out_proj",
    )(*ins)


def _mlp_kernel(x_ref, h_ref, g_ref, wup_ref, wdn_ref, fn_ref, o_ref, *bf16_w_out, nf, final):
    f = pl.program_id(2)
    if bf16_w_out:
        wu_ref, wd_ref = bf16_w_out
        wu_ref[...] = wup_ref[...].astype(BF16)
        wd_ref[...] = wdn_ref[...].astype(BF16)
    else:
        wu_ref, wd_ref = wup_ref, wdn_ref

    def partial_down():
        up = _dot(h_ref[...], wu_ref[...])
        act = jnp.square(jnp.maximum(up, 0.0)).astype(BF16)
        return _dot(act, wd_ref[...])

    @pl.when(f == 0)
    def _():
        o_ref[...] = partial_down()

    @pl.when(f > 0)
    def _():
        o_ref[...] += partial_down()

    @pl.when(f == nf - 1)
    def _():
        y = x_ref[...] + g_ref[...] * o_ref[...]
        if final:
            y = _rms(y) * fn_ref[...]
        o_ref[...] = y


def _mlp(x, h, mod, w_up, w_down, layer, final_norm, nbatch, seq, tm, tf, final):
    m, d = x.shape
    nt = seq // tm
    dff = w_up.shape[-1]
    nf = dff // tf
    out_specs = [_rows(tm, d, nt)]
    out_shape = [jax.ShapeDtypeStruct((m, d), F32)]
    if layer is None:
        w_specs = [pl.BlockSpec((d, tf), lambda b, s, f: (0, f)),
                   pl.BlockSpec((tf, d), lambda b, s, f: (f, 0))]
    else:
        assert nbatch * nt == 1
        w_specs = [pl.BlockSpec((None, d, tf), lambda b, s, f: (layer, 0, f)),
                   pl.BlockSpec((None, tf, d), lambda b, s, f: (layer, f, 0))]
        out_specs += [pl.BlockSpec((d, tf), lambda b, s, f: (0, f)),
                      pl.BlockSpec((tf, d), lambda b, s, f: (f, 0))]
        out_shape += [jax.ShapeDtypeStruct((d, dff), BF16), jax.ShapeDtypeStruct((dff, d), BF16)]
    return pl.pallas_call(
        functools.partial(_mlp_kernel, nf=nf, final=final),
        grid=(nbatch, nt, nf),
        in_specs=[_rows(tm, d, nt), _rows(tm, d, nt), mod.spec(5), *w_specs,
                  pl.BlockSpec(final_norm.shape, lambda b, s, f: (0, 0))],
        out_specs=out_specs, out_shape=out_shape,
        compiler_params=_cparams(3), name="mlp",
    )(x, h, mod.arr, w_up, w_down, final_norm)


def _rope_tables(pos):
    half = ROPE_DIM // 2
    inv = ROPE_BASE ** (-jnp.arange(half, dtype=F32) / half)
    ang = pos.astype(F32)[:, None] * inv[None, :]
    cos, sin, zero = jnp.cos(ang), jnp.sin(ang), jnp.zeros_like(ang)
    return (jnp.concatenate([cos, cos, zero, zero], axis=-1),
            jnp.concatenate([-sin, zero, zero, zero], axis=-1),
            jnp.concatenate([zero, sin, zero, zero], axis=-1))


def _prep_weights(w_in, q_norm, w_uq, kv_norm, w_ukv, conv_w, attn_out_norm, conv_out_norm,
                  w_out, w_up, w_down):
    depth = w_in.shape[0]
    w_in = w_in.astype(BF16)
    wq = w_uq.astype(BF16).reshape(depth, Q_LORA, N_HEADS, QK_DIM)
    wq = jnp.pad(wq, ((0, 0), (0, 0), (0, 0), (0, HEAD_W - QK_DIM)))
    wkv = w_ukv.astype(BF16).reshape(depth, KV_LORA, N_HEADS, NOPE_DIM + V_DIM)
    w_uk = wkv[..., :NOPE_DIM]
    w_uv = wkv[..., NOPE_DIM:]
    return dict(
        w_in=w_in,
        w_in_b=w_in[:, :, Q_LORA + KV_LORA + ROPE_DIM:],
        w_uq=wq.reshape(depth, Q_LORA, N_HEADS * HEAD_W),
        w_uk=w_uk.reshape(depth, KV_LORA, N_HEADS * NOPE_DIM),
        w_uv=w_uv.reshape(depth, KV_LORA, N_HEADS * V_DIM),
        w_uk_t=jnp.transpose(w_uk, (0, 2, 3, 1)),
        w_uv_h=jnp.transpose(w_uv, (0, 2, 1, 3)),
        q_norm=q_norm[:, None, :], kv_norm=kv_norm[:, None, :],
        conv_w=conv_w, attn_out_norm=attn_out_norm[:, None, :],
        conv_out_norm=conv_out_norm[:, None, :],
        w_out=w_out, w_up=w_up, w_down=w_down)


def kernel(x_prompt, x_sample, c_prompt, c_sample, cache_kv_latent, cache_k_rope, state_conv,
           w_ada, b_ada, w_in, q_norm, w_uq, kv_norm, w_ukv, conv_w,
           attn_out_norm, conv_out_norm, w_out, w_up, w_down, final_norm):
    bp, sp, d = x_prompt.shape
    bs, ss, _ = x_sample.shape
    depth = w_ada.shape[0]
    past = cache_kv_latent.shape[2]
    dc = conv_w.shape[-1]
    ms = bs * ss

    w = _prep_weights(w_in, q_norm, w_uq, kv_norm, w_ukv, conv_w, attn_out_norm,
                      conv_out_norm, w_out, w_up, w_down)
    fnorm = final_norm[None, :]
    tabs_p = _rope_tables(jnp.arange(sp))
    tabs_s = tuple(jnp.tile(tb, (bs, 1)) for tb in _rope_tables(past + jnp.arange(ss)))

    mod = _modulation(jnp.concatenate([c_prompt, c_sample], axis=0), w_ada, b_ada)
    mod = mod.reshape(depth, bp + bs, N_MOD, d)
    mod_p = mod[:, :bp].reshape(depth * bp * N_MOD, 1, d)
    mod_s = jnp.repeat(jnp.swapaxes(mod[:, bp:], 1, 2), ss, axis=2).reshape(depth * N_MOD, ms, d)
    zero_conv = jnp.zeros((1, bp, CONV_W - 1, dc), F32)

    xp = x_prompt.reshape(bp * sp, d)
    xs = x_sample.reshape(ms, d)
    past_kpe_t = jnp.swapaxes(cache_k_rope, 2, 3)
    stacks = None
    outs = [[] for _ in range(4)]
    for l in range(depth):
        last = l == depth - 1
        mp = _Mod(mod_p, l, bp)
        msm = _Mod(mod_s, l, 0)

        qa, qpe, ckv_s, kpe_s = _attn_in_sample(xs, msm, w, l, tabs_s, bs, ss)
        conv_ns, cstate_s = _conv_in(xs, msm, w, l, state_conv, l, bs, ss, nb=bs, t=ss)
        attn_s = _sample_attention(qa, qpe, cache_kv_latent, past_kpe_t, ckv_s, kpe_s,
                                   w["w_uv_h"], l)
        xs, hs, w_out_l = _out_proj(xs, attn_s, conv_ns, msm, w, w["w_out"], l, 1, ms, tm=ms)
        xs, w_up_l, w_down_l = _mlp(xs, hs, msm, w["w_up"], w["w_down"], l, fnorm, 1, ms,
                                    tm=ms, tf=512, final=last)
        outs[1].append(ckv_s.reshape(bs, ss, KV_LORA))
        outs[2].append(kpe_s.reshape(bs, ss, ROPE_DIM))
        outs[3].append(cstate_s)

        q, k, v, *stacks = _attn_in_prompt(xp, mp, w, l, tabs_p, bp, sp, stacks)
        conv_n, cstate = _conv_in(xp, mp, w, l, zero_conv, 0, bp, sp, nb=1, t=512, nsub=2)
        attn = _prompt_attention(q, k, v, bp, sp)
        xp, hp = _out_proj(xp, attn, conv_n, mp, w, w_out_l, None, bp, sp, tm=512, nsub=2)
        xp, = _mlp(xp, hp, mp, w_up_l, w_down_l, None, fnorm, bp, sp, tm=1024, tf=512, final=last)
        outs[0].append(cstate)

    return (xp.reshape(bp, sp, d), xs.reshape(bs, ss, d),
            stacks[0].reshape(depth, bp, sp, KV_LORA), jnp.swapaxes(stacks[1], 2, 3),
            jnp.stack(outs[0]), jnp.stack(outs[1]), jnp.stack(outs[2]), jnp.stack(outs[3]))
```

```python
import functools
import math

import jax
import jax.numpy as jnp
from jax import lax
from jax.experimental import pallas as pl
from jax.experimental.pallas import tpu as pltpu

F32 = jnp.float32
BF16 = jnp.bfloat16

CHUNK = 64
N_HEADS = 8
NOPE_DIM = 128
ROPE_DIM = 64
V_DIM = 128
QK_DIM = NOPE_DIM + ROPE_DIM
Q_LORA = 512
KV_LORA = 512
D_ATTN = N_HEADS * V_DIM
CONV_W = 3
N_MOD = 6
ROPE_BASE = 10000.0
EPS = 1e-6
ATTN_SCALE = QK_DIM ** -0.5

LANES = 128
HEAD_W = 2 * LANES
ROPE_W = LANES
IN_A_W = Q_LORA + KV_LORA + ROPE_W
CONV_PAD = 8
Q_SCALE = ATTN_SCALE * math.log2(math.e)
NEG_BIG = -1e30
VMEM_LIMIT = 60000 * 1024


def _cparams(n_axes):
    return pltpu.CompilerParams(dimension_semantics=("arbitrary",) * n_axes,
                                vmem_limit_bytes=VMEM_LIMIT)


def _rms(x):
    return x * lax.rsqrt(jnp.mean(x * x, axis=-1, keepdims=True) + EPS)


def _dot(a, b):
    return jnp.dot(a, b, preferred_element_type=F32)


def _dot_t(a, b):
    return lax.dot_general(a, b, (((1,), (1,)), ((), ())), preferred_element_type=F32)


def _layer_spec(arr, layer, block=None, index=None):
    block = tuple(arr.shape[1:]) if block is None else tuple(block)
    index = (0,) * len(block) if index is None else tuple(index)
    return pl.BlockSpec((None,) + block, lambda *g: (layer,) + index,
                        pipeline_mode=pl.Buffered(1))


def _rows(tm, cols, tiles_per_batch):
    return pl.BlockSpec((tm, cols), lambda b, s, *_: (b * tiles_per_batch + s, 0))


class _Mod:
    def __init__(self, arr, layer, per_batch):
        self.arr, self.layer, self.per_batch = arr, layer, per_batch

    def spec(self, role):
        layer, nb = self.layer, self.per_batch
        if nb:
            return pl.BlockSpec((None,) + self.arr.shape[1:],
                                lambda b, *_: ((layer * nb + b) * N_MOD + role, 0, 0))
        return pl.BlockSpec((None,) + self.arr.shape[1:],
                            lambda *_: (layer * N_MOD + role, 0, 0))


def _mod_kernel(c_ref, w_ref, b_ref, o_ref):
    c = c_ref[...]
    s = (c * jax.nn.sigmoid(c)).astype(BF16)
    o_ref[0] = _dot(s, w_ref[0].astype(BF16)) + b_ref[0]


def _modulation(c_all, w_ada, b_ada, tn=1024):
    depth, d, n = w_ada.shape
    r = c_all.shape[0]
    return pl.pallas_call(
        _mod_kernel,
        grid=(depth, n // tn),
        in_specs=[pl.BlockSpec((r, d), lambda l, j: (0, 0)),
                  pl.BlockSpec((1, d, tn), lambda l, j: (l, 0, j)),
                  pl.BlockSpec((1, 1, tn), lambda l, j: (l, 0, j))],
        out_specs=pl.BlockSpec((1, r, tn), lambda l, j: (l, 0, j)),
        out_shape=jax.ShapeDtypeStruct((depth, r, n), F32),
        compiler_params=_cparams(2),
        name="modulation",
    )(c_all, w_ada, b_ada.reshape(depth, 1, n))


def _rope128(p, cos, s1, s2):
    return p * cos + pltpu.roll(p, 96, 1) * s1 + pltpu.roll(p, 32, 1) * s2


def _attn_in_common(rs, x_ref, sh_ref, sc_ref, win_ref, qn_ref, wuq_ref, kvn_ref,
                    cos_ref, s1_ref, s2_ref):
    x = x_ref[rs, :]
    h = (_rms(x) * (1.0 + sc_ref[...]) + sh_ref[...]).astype(BF16)
    z = _dot(h, win_ref[...])
    cos, s1, s2 = cos_ref[rs, :], s1_ref[rs, :], s2_ref[rs, :]
    qn = (_rms(z[:, :Q_LORA]) * qn_ref[...]).astype(BF16)
    q = _dot(qn, wuq_ref[...]) * Q_SCALE
    ckv = _rms(z[:, Q_LORA:Q_LORA + KV_LORA]) * kvn_ref[...]
    kp = _rope128(z[:, Q_LORA + KV_LORA:], cos, s1, s2)
    rope = functools.partial(_rope128, cos=cos, s1=s1, s2=s2)
    return q, ckv, kp, rope


def _attn_in_prompt_kernel(x_ref, sh_ref, sc_ref, win_ref, qn_ref, wuq_ref, kvn_ref,
                           cos_ref, s1_ref, s2_ref, wuk_ref, wuv_ref, *rest, nsub, first):
    q_out, k_out, v_out, ckv_out, kpe_out = rest[-5:]
    if first:
        ckv_out[1:] = jnp.zeros((ckv_out.shape[0] - 1,) + ckv_out.shape[1:], F32)
        kpe_out[1:] = jnp.zeros((kpe_out.shape[0] - 1,) + kpe_out.shape[1:], F32)
        ckv_out, kpe_out = ckv_out.at[0], kpe_out.at[0]
    rows = x_ref.shape[0] // nsub
    lane = lax.broadcasted_iota(jnp.int32, (rows, LANES), 1)
    ones_col = jnp.where(lane == 0, 1.0, 0.0).astype(BF16)
    for j in range(nsub):
        rs = slice(j * rows, (j + 1) * rows)
        q, ckv, kp, rope = _attn_in_common(rs, x_ref, sh_ref, sc_ref, win_ref, qn_ref, wuq_ref,
                                           kvn_ref, cos_ref, s1_ref, s2_ref)
        ckv_out[rs, :] = ckv
        kpe_out[:, rs] = kp.T[:ROPE_DIM, :]
        ckb = ckv.astype(BF16)
        kn = _dot(ckb, wuk_ref[...])
        v = _dot(ckb, wuv_ref[...]).astype(BF16)
        kpb = kp.astype(BF16)
        for hd in range(N_HEADS):
            c0 = hd * HEAD_W
            v_out[rs, c0:c0 + V_DIM] = v[:, hd * V_DIM:(hd + 1) * V_DIM]
            v_out[rs, c0 + V_DIM:c0 + HEAD_W] = ones_col
            q_out[rs, c0:c0 + NOPE_DIM] = q[:, c0:c0 + NOPE_DIM].astype(BF16)
            q_out[rs, c0 + NOPE_DIM:c0 + HEAD_W] = rope(q[:, c0 + NOPE_DIM:c0 + HEAD_W]).astype(BF16)
            k_out[rs, c0:c0 + NOPE_DIM] = kn[:, hd * NOPE_DIM:(hd + 1) * NOPE_DIM].astype(BF16)
            k_out[rs, c0 + NOPE_DIM:c0 + HEAD_W] = kpb


def _attn_in_sample_kernel(x_ref, sh_ref, sc_ref, win_ref, qn_ref, wuq_ref, kvn_ref,
                           cos_ref, s1_ref, s2_ref, wukt_ref,
                           qa_out, qpe_out, ckv_out, kpe_out, *, nb, t):
    q, ckv, kp, rope = _attn_in_common(slice(None), x_ref, sh_ref, sc_ref, win_ref, qn_ref,
                                       wuq_ref, kvn_ref, cos_ref, s1_ref, s2_ref)
    ckv_out[...] = ckv
    kpe_out[...] = kp[:, :ROPE_DIM]
    for hd in range(N_HEADS):
        c0 = hd * HEAD_W
        qa = _dot(q[:, c0:c0 + NOPE_DIM].astype(BF16), wukt_ref[hd])
        qa_out[:, hd] = qa.reshape(nb, t, KV_LORA).astype(BF16)
        qp = rope(q[:, c0 + NOPE_DIM:c0 + HEAD_W])
        qpe_out[:, hd] = qp.reshape(nb, t, ROPE_W).astype(BF16)


def _attn_in_prompt(x, mod, w, layer, tabs, nbatch, seq, stacks, tm=512, nsub=2):
    m, d = x.shape
    depth = w["w_in"].shape[0]
    nt = seq // tm
    tab_spec = pl.BlockSpec((tm, ROPE_W), lambda b, s: (s, 0))
    ins = [x, mod.arr, mod.arr, w["w_in"], w["q_norm"], w["w_uq"], w["kv_norm"],
           *tabs, w["w_uk"], w["w_uv"]]
    specs = [_rows(tm, d, nt), mod.spec(0), mod.spec(1),
             _layer_spec(w["w_in"], layer, (d, IN_A_W)),
             _layer_spec(w["q_norm"], layer), _layer_spec(w["w_uq"], layer),
             _layer_spec(w["kv_norm"], layer), tab_spec, tab_spec, tab_spec,
             _layer_spec(w["w_uk"], layer), _layer_spec(w["w_uv"], layer)]
    aliases = {}
    if stacks is not None:
        aliases = {len(ins): 3, len(ins) + 1: 4}
        ins += list(stacks)
        specs += [pl.BlockSpec(memory_space=pl.ANY)] * 2
    outs = [jax.ShapeDtypeStruct((m, N_HEADS * HEAD_W), BF16),
            jax.ShapeDtypeStruct((m, N_HEADS * HEAD_W), BF16),
            jax.ShapeDtypeStruct((m, N_HEADS * HEAD_W), BF16),
            jax.ShapeDtypeStruct((depth, m, KV_LORA), F32),
            jax.ShapeDtypeStruct((depth, nbatch, ROPE_DIM, seq), F32)]
    first = stacks is None
    lsel = (depth, 0) if first else (None, layer)
    out_specs = [_rows(tm, N_HEADS * HEAD_W, nt), _rows(tm, N_HEADS * HEAD_W, nt),
                 _rows(tm, N_HEADS * HEAD_W, nt),
                 pl.BlockSpec((lsel[0], tm, KV_LORA), lambda b, s: (lsel[1], b * nt + s, 0)),
                 pl.BlockSpec((lsel[0], None, ROPE_DIM, tm), lambda b, s: (lsel[1], b, 0, s))]
    return pl.pallas_call(
        functools.partial(_attn_in_prompt_kernel, nsub=nsub, first=first), grid=(nbatch, nt),
        in_specs=specs, out_specs=out_specs, out_shape=outs, input_output_aliases=aliases,
        compiler_params=_cparams(2), name="attn_in_prompt",
    )(*ins)


def _attn_in_sample(x, mod, w, layer, tabs, nb, t):
    m, d = x.shape
    ins = [x, mod.arr, mod.arr, w["w_in"], w["q_norm"], w["w_uq"], w["kv_norm"],
           *tabs, w["w_uk_t"]]
    full = lambda a: pl.BlockSpec(a.shape, lambda *_: (0,) * a.ndim)
    specs = [_rows(m, d, 1), mod.spec(0), mod.spec(1),
             _layer_spec(w["w_in"], layer, (d, IN_A_W)),
             _layer_spec(w["q_norm"], layer), _layer_spec(w["w_uq"], layer),
             _layer_spec(w["kv_norm"], layer), full(tabs[0]), full(tabs[1]), full(tabs[2]),
             _layer_spec(w["w_uk_t"], layer)]
    outs = [jax.ShapeDtypeStruct((nb, N_HEADS, t, KV_LORA), BF16),
            jax.ShapeDtypeStruct((nb, N_HEADS, t, ROPE_W), BF16),
            jax.ShapeDtypeStruct((m, KV_LORA), F32),
            jax.ShapeDtypeStruct((m, ROPE_DIM), F32)]
    out_specs = [pl.BlockSpec((nb, N_HEADS, t, KV_LORA), lambda b, s: (0, 0, 0, 0)),
                 pl.BlockSpec((nb, N_HEADS, t, ROPE_W), lambda b, s: (0, 0, 0, 0)),
                 _rows(m, KV_LORA, 1), _rows(m, ROPE_DIM, 1)]
    return pl.pallas_call(
        functools.partial(_attn_in_sample_kernel, nb=nb, t=t), grid=(1, 1),
        in_specs=specs, out_specs=out_specs, out_shape=outs,
        compiler_params=_cparams(2), name="attn_in_sample",
    )(*ins)


def _conv_in_kernel(x_ref, sh_ref, sc_ref, w_ref, cw_ref, cn_ref, prev_ref,
                    o_ref, st_ref, buf, *, nb, t, dc, nsub):
    @pl.when(pl.program_id(1) == 0)
    def _():
        buf[:, CONV_PAD - 2:CONV_PAD, :] = prev_ref[...]

    tg = t // nsub
    for j in range(nsub):
        p0 = j * tg
        rs = slice(p0, p0 + tg) if nsub > 1 else slice(None)
        x = x_ref[rs, :]
        h = (_rms(x) * (1.0 + sc_ref[...]) + sh_ref[...]).astype(BF16)
        z = _dot(h, w_ref[...])
        gate_b = z[:, :dc]
        cu = z[:, dc:2 * dc] * z[:, 2 * dc:]
        buf[:, CONV_PAD + p0:CONV_PAD + p0 + tg, :] = cu.reshape(nb, tg, dc)
        base = CONV_PAD - (CONV_W - 1) + p0
        y = cw_ref[0:1, :].reshape(1, 1, dc) * buf[:, base:base + tg, :]
        for k in range(1, CONV_W):
            y = y + cw_ref[k:k + 1, :].reshape(1, 1, dc) * buf[:, base + k:base + k + tg, :]
        conv_o = gate_b * y.reshape(nb * tg, dc)
        o_ref[rs, :] = (_rms(conv_o) * cn_ref[...]).astype(BF16)
    last = buf[:, CONV_PAD + t - 2:CONV_PAD + t, :]
    st_ref[...] = last
    buf[:, CONV_PAD - 2:CONV_PAD, :] = last


def _conv_in(x, mod, w, layer, prev, prev_layer, nbatch, seq, nb, t, nsub=1):
    m, d = x.shape
    dc = w["conv_w"].shape[-1]
    nt = seq // t
    ins = [x, mod.arr, mod.arr, w["w_in_b"], w["conv_w"], w["conv_out_norm"], prev]
    specs = [_rows(nb * t, d, nt), mod.spec(0), mod.spec(1), _layer_spec(w["w_in_b"], layer),
             _layer_spec(w["conv_w"], layer), _layer_spec(w["conv_out_norm"], layer),
             pl.BlockSpec((None, nb, CONV_W - 1, dc), lambda b, s: (prev_layer, b, 0, 0))]
    outs = [jax.ShapeDtypeStruct((m, dc), BF16),
            jax.ShapeDtypeStruct(prev.shape[1:], F32)]
    out_specs = [_rows(nb * t, dc, nt),
                 pl.BlockSpec((nb, CONV_W - 1, dc), lambda b, s: (b, 0, 0))]
    return pl.pallas_call(
        functools.partial(_conv_in_kernel, nb=nb, t=t, dc=dc, nsub=nsub),
        grid=(nbatch // nb, nt), in_specs=specs, out_specs=out_specs, out_shape=outs,
        scratch_shapes=[pltpu.VMEM((nb, CONV_PAD + t, dc), F32)],
        compiler_params=_cparams(2), name="conv_in",
    )(*ins)


def _prompt_attn_kernel(q_ref, k_ref, v_ref, o_ref, s_buf, p_buf, *, seq, tq, heads):
    row_chunk = lax.broadcasted_iota(jnp.int32, (tq, tq), 0) // CHUNK
    col_chunk = lax.broadcasted_iota(jnp.int32, (tq, tq), 1) // CHUNK
    visible = col_chunk <= row_chunk
    tiles = [(hd, qi) for hd in range(heads) for qi in range(seq // tq)]

    def scores(i):
        hd, qi = tiles[i]
        hi = (qi + 1) * tq
        cols = slice(hd * HEAD_W, (hd + 1) * HEAD_W)
        s_buf[i % 2, :, 0:hi] = _dot_t(q_ref[qi * tq:hi, cols], k_ref[0:hi, cols])

    scores(0)
    for i, (hd, qi) in enumerate(tiles):
        lo, hi = qi * tq, (qi + 1) * tq
        if i + 1 < len(tiles):
            scores(i + 1)
        s_d = jnp.where(visible, s_buf[i % 2, :, lo:hi], NEG_BIG)
        m = jnp.max(s_d, axis=-1, keepdims=True)
        if qi > 0:
            m = jnp.maximum(m, jnp.max(s_buf[i % 2, :, 0:lo], axis=-1, keepdims=True))
            p_buf[:, 0:lo] = jnp.exp2(s_buf[i % 2, :, 0:lo] - m).astype(BF16)
        p_buf[:, lo:hi] = jnp.exp2(s_d - m).astype(BF16)
        oe = _dot(p_buf[:, 0:hi], v_ref[0:hi, hd * HEAD_W:(hd + 1) * HEAD_W])
        o_ref[lo:hi, hd * V_DIM:(hd + 1) * V_DIM] = oe[:, :V_DIM] / oe[:, V_DIM:V_DIM + 1]


def _prompt_attention(q, k, v, nbatch, seq, tq=256, heads=2):
    m = q.shape[0]
    qkv_spec = pl.BlockSpec((seq, heads * HEAD_W), lambda b, h: (b, h))
    return pl.pallas_call(
        functools.partial(_prompt_attn_kernel, seq=seq, tq=tq, heads=heads),
        grid=(nbatch, N_HEADS // heads),
        in_specs=[qkv_spec, qkv_spec, qkv_spec],
        out_specs=pl.BlockSpec((seq, heads * V_DIM), lambda b, h: (b, h)),
        out_shape=jax.ShapeDtypeStruct((m, D_ATTN), F32),
        scratch_shapes=[pltpu.VMEM((2, tq, seq), F32), pltpu.VMEM((tq, seq), BF16)],
        compiler_params=_cparams(2), name="prompt_attention",
    )(q, k, v)


def _sample_attn_kernel(qa_ref, qpe_ref, pc_ref, pk_ref, nc_ref, nk_ref, wuv_ref, o_ref, *, t):
    rows = N_HEADS * t
    qa = qa_ref[0].reshape(rows, KV_LORA)
    qp = qpe_ref[0].reshape(rows, ROPE_W)[:, :ROPE_DIM]
    pc = pc_ref[0].astype(BF16)
    pk_t = pk_ref[0].astype(BF16)
    nc = nc_ref[...].astype(BF16)
    nk = nk_ref[...].astype(BF16)
    s_p = _dot_t(qa, pc) + _dot(qp, pk_t)
    s_n = _dot_t(qa, nc) + _dot_t(qp, nk)
    m = jnp.maximum(jnp.max(s_p, axis=-1, keepdims=True), jnp.max(s_n, axis=-1, keepdims=True))
    p_p = jnp.exp2(s_p - m)
    p_n = jnp.exp2(s_n - m)
    l = jnp.sum(p_p, axis=-1, keepdims=True) + jnp.sum(p_n, axis=-1, keepdims=True)
    o_lat = (_dot(p_p.astype(BF16), pc) + _dot(p_n.astype(BF16), nc)) / l
    o_lat = o_lat.astype(BF16)
    for hd in range(N_HEADS):
        o_ref[:, hd * V_DIM:(hd + 1) * V_DIM] = _dot(o_lat[hd * t:(hd + 1) * t, :], wuv_ref[hd])


def _sample_attention(qa, qpe, past_ckv, past_kpe_t, new_ckv, new_kpe, w_uv_h, layer):
    nb, _, t, _ = qa.shape
    past = past_ckv.shape[2]
    return pl.pallas_call(
        functools.partial(_sample_attn_kernel, t=t),
        grid=(nb,),
        in_specs=[pl.BlockSpec((1, N_HEADS, t, KV_LORA), lambda b: (b, 0, 0, 0)),
                  pl.BlockSpec((1, N_HEADS, t, ROPE_W), lambda b: (b, 0, 0, 0)),
                  pl.BlockSpec((None, 1, past, KV_LORA), lambda b: (layer, b, 0, 0)),
                  pl.BlockSpec((None, 1, ROPE_DIM, past), lambda b: (layer, b, 0, 0)),
                  pl.BlockSpec((t, KV_LORA), lambda b: (b, 0)),
                  pl.BlockSpec((t, ROPE_DIM), lambda b: (b, 0)),
                  _layer_spec(w_uv_h, layer)],
        out_specs=pl.BlockSpec((t, D_ATTN), lambda b: (b, 0)),
        out_shape=jax.ShapeDtypeStruct((nb * t, D_ATTN), F32),
        compiler_params=_cparams(1), name="sample_attention",
    )(qa, qpe, past_ckv, past_kpe_t, new_ckv, new_kpe, w_uv_h)


def _out_proj_kernel(x_ref, attn_ref, conv_ref, g_ref, shm_ref, scm_ref, an_ref, woa_ref, woc_ref,
                     o_ref, h_ref, *bf16_w_out, nsub):
    if bf16_w_out:
        wo_ref, = bf16_w_out
        da = woa_ref.shape[0]
        wo_ref[0:da, :] = woa_ref[...].astype(BF16)
        wo_ref[da:, :] = woc_ref[...].astype(BF16)
        woa, woc = wo_ref[0:da, :], wo_ref[da:, :]
    else:
        woa, woc = woa_ref[...], woc_ref[...]
    rows = x_ref.shape[0] // nsub
    for j in range(nsub):
        rs = slice(j * rows, (j + 1) * rows) if nsub > 1 else slice(None)
        a = (_rms(attn_ref[rs, :]) * an_ref[...]).astype(BF16)
        mix = _dot(a, woa) + _dot(conv_ref[rs, :], woc)
        y = x_ref[rs, :] + g_ref[...] * mix
        o_ref[rs, :] = y
        h_ref[rs, :] = (_rms(y) * (1.0 + scm_ref[...]) + shm_ref[...]).astype(BF16)


def _out_proj(x, attn, conv_n, mod, w, w_out, layer, nbatch, seq, tm, nsub=1):
    m, d = x.shape
    nt = seq // tm
    da, dc = attn.shape[1], conv_n.shape[1]
    ins = [x, attn, conv_n, mod.arr, mod.arr, mod.arr, w["attn_out_norm"], w_out, w_out]
    specs = [_rows(tm, d, nt), _rows(tm, da, nt), _rows(tm, dc, nt),
             mod.spec(2), mod.spec(3), mod.spec(4),
             _layer_spec(w["attn_out_norm"], mod.layer)]
    out_specs = [_rows(tm, d, nt), _rows(tm, d, nt)]
    out_shape = [jax.ShapeDtypeStruct((m, d), F32), jax.ShapeDtypeStruct((m, d), BF16)]
    if layer is None:
        specs += [pl.BlockSpec((da, d), lambda b, s: (0, 0), pipeline_mode=pl.Buffered(1)),
                  pl.BlockSpec((dc, d), lambda b, s: (da // dc, 0), pipeline_mode=pl.Buffered(1))]
    else:
        assert nbatch * nt == 1
        specs += [_layer_spec(w_out, layer, (da, d), (0, 0)),
                  _layer_spec(w_out, layer, (dc, d), (da // dc, 0))]
        out_specs.append(pl.BlockSpec((da + dc, d), lambda b, s: (0, 0)))
        out_shape.append(jax.ShapeDtypeStruct((da + dc, d), BF16))
    return pl.pallas_call(
        functools.partial(_out_proj_kernel, nsub=nsub), grid=(nbatch, nt), in_specs=specs,
        out_specs=out_specs, out_shape=out_shape,
        compiler_params=_cparams(2), name="out_proj",
    )(*ins)


def _mlp_kernel(x_ref, h_ref, g_ref, wup_ref, wdn_ref, fn_ref, o_ref, *extra, nf, final, cast,
                x_rows):
    f = pl.program_id(2)
    if cast:
        wu_ref, wd_ref = extra
        wu_ref[...] = wup_ref[...].astype(BF16)
        wd_ref[...] = wdn_ref[...].astype(BF16)
    else:
        wu_ref, wd_ref = wup_ref, wdn_ref
    if x_rows:
        x_buf, x_sem = extra
        tile = pl.program_id(0) * pl.num_programs(1) + pl.program_id(1)
        row0 = pl.multiple_of(tile * x_rows, x_rows)
        x_copy = pltpu.make_async_copy(x_ref.at[pl.ds(row0, x_rows), :], x_buf, x_sem)

        @pl.when(f == 0)
        def _():
            x_copy.start()

    def partial_down():
        up = _dot(h_ref[...], wu_ref[...])
        act = jnp.square(jnp.maximum(up, 0.0)).astype(BF16)
        return _dot(act, wd_ref[...])

    @pl.when(f == 0)
    def _():
        o_ref[...] = partial_down()

    @pl.when(f > 0)
    def _():
        o_ref[...] += partial_down()

    @pl.when(f == nf - 1)
    def _():
        if x_rows:
            x_copy.wait()
            x = x_buf[...]
        else:
            x = x_ref[...]
        y = x + g_ref[...] * o_ref[...]
        if final:
            y = _rms(y) * fn_ref[...]
        o_ref[...] = y


def _mlp(x, h, mod, w_up, w_down, layer, final_norm, nbatch, seq, tm, tf, final):
    m, d = x.shape
    nt = seq // tm
    dff = w_up.shape[-1]
    nf = dff // tf
    cast = layer is not None
    out_specs = [_rows(tm, d, nt)]
    out_shape = [jax.ShapeDtypeStruct((m, d), F32)]
    scratch = []
    if cast:
        assert nbatch * nt == 1
        x_spec = _rows(tm, d, nt)
        w_specs = [pl.BlockSpec((None, d, tf), lambda b, s, f: (layer, 0, f)),
                   pl.BlockSpec((None, tf, d), lambda b, s, f: (layer, f, 0))]
        out_specs += [pl.BlockSpec((d, tf), lambda b, s, f: (0, f)),
                      pl.BlockSpec((tf, d), lambda b, s, f: (f, 0))]
        out_shape += [jax.ShapeDtypeStruct((d, dff), BF16), jax.ShapeDtypeStruct((dff, d), BF16)]
    else:
        x_spec = pl.BlockSpec(memory_space=pl.ANY)
        w_specs = [pl.BlockSpec((d, tf), lambda b, s, f: (0, f)),
                   pl.BlockSpec((tf, d), lambda b, s, f: (f, 0))]
        scratch = [pltpu.VMEM((tm, d), F32), pltpu.SemaphoreType.DMA(())]
    return pl.pallas_call(
        functools.partial(_mlp_kernel, nf=nf, final=final, cast=cast,
                          x_rows=0 if cast else tm),
        grid=(nbatch, nt, nf),
        in_specs=[x_spec, _rows(tm, d, nt), mod.spec(5), *w_specs,
                  pl.BlockSpec(final_norm.shape, lambda b, s, f: (0, 0))],
        out_specs=out_specs, out_shape=out_shape, scratch_shapes=scratch,
        compiler_params=_cparams(3), name="mlp",
    )(x, h, mod.arr, w_up, w_down, final_norm)


def _rope_tables(pos):
    half = ROPE_DIM // 2
    inv = ROPE_BASE ** (-jnp.arange(half, dtype=F32) / half)
    ang = pos.astype(F32)[:, None] * inv[None, :]
    cos, sin, zero = jnp.cos(ang), jnp.sin(ang), jnp.zeros_like(ang)
    return (jnp.concatenate([cos, cos, zero, zero], axis=-1),
            jnp.concatenate([-sin, zero, zero, zero], axis=-1),
            jnp.concatenate([zero, sin, zero, zero], axis=-1))


def _prep_weights(w_in, q_norm, w_uq, kv_norm, w_ukv, conv_w, attn_out_norm, conv_out_norm,
                  w_out, w_up, w_down):
    depth = w_in.shape[0]
    w_in = w_in.astype(BF16)
    wq = w_uq.astype(BF16).reshape(depth, Q_LORA, N_HEADS, QK_DIM)
    wq = jnp.pad(wq, ((0, 0), (0, 0), (0, 0), (0, HEAD_W - QK_DIM)))
    wkv = w_ukv.astype(BF16).reshape(depth, KV_LORA, N_HEADS, NOPE_DIM + V_DIM)
    w_uk = wkv[..., :NOPE_DIM]
    w_uv = wkv[..., NOPE_DIM:]
    return dict(
        w_in=w_in,
        w_in_b=w_in[:, :, Q_LORA + KV_LORA + ROPE_DIM:],
        w_uq=wq.reshape(depth, Q_LORA, N_HEADS * HEAD_W),
        w_uk=w_uk.reshape(depth, KV_LORA, N_HEADS * NOPE_DIM),
        w_uv=w_uv.reshape(depth, KV_LORA, N_HEADS * V_DIM),
        w_uk_t=jnp.transpose(w_uk, (0, 2, 3, 1)),
        w_uv_h=jnp.transpose(w_uv, (0, 2, 1, 3)),
        q_norm=q_norm[:, None, :], kv_norm=kv_norm[:, None, :],
        conv_w=conv_w, attn_out_norm=attn_out_norm[:, None, :],
        conv_out_norm=conv_out_norm[:, None, :],
        w_out=w_out, w_up=w_up, w_down=w_down)


def kernel(x_prompt, x_sample, c_prompt, c_sample, cache_kv_latent, cache_k_rope, state_conv,
           w_ada, b_ada, w_in, q_norm, w_uq, kv_norm, w_ukv, conv_w,
           attn_out_norm, conv_out_norm, w_out, w_up, w_down, final_norm):
    bp, sp, d = x_prompt.shape
    bs, ss, _ = x_sample.shape
    depth = w_ada.shape[0]
    past = cache_kv_latent.shape[2]
    dc = conv_w.shape[-1]
    ms = bs * ss

    w = _prep_weights(w_in, q_norm, w_uq, kv_norm, w_ukv, conv_w, attn_out_norm,
                      conv_out_norm, w_out, w_up, w_down)
    fnorm = final_norm[None, :]
    tabs_p = _rope_tables(jnp.arange(sp))
    tabs_s = tuple(jnp.tile(tb, (bs, 1)) for tb in _rope_tables(past + jnp.arange(ss)))

    mod = _modulation(jnp.concatenate([c_prompt, c_sample], axis=0), w_ada, b_ada)
    mod = mod.reshape(depth, bp + bs, N_MOD, d)
    mod_p = mod[:, :bp].reshape(depth * bp * N_MOD, 1, d)
    mod_s = jnp.repeat(jnp.swapaxes(mod[:, bp:], 1, 2), ss, axis=2).reshape(depth * N_MOD, ms, d)
    zero_conv = jnp.zeros((1, bp, CONV_W - 1, dc), F32)

    xp = x_prompt.reshape(bp * sp, d)
    xs = x_sample.reshape(ms, d)
    past_kpe_t = jnp.swapaxes(cache_k_rope, 2, 3)
    stacks = None
    outs = [[] for _ in range(4)]
    for l in range(depth):
        last = l == depth - 1
        mp = _Mod(mod_p, l, bp)
        msm = _Mod(mod_s, l, 0)

        qa, qpe, ckv_s, kpe_s = _attn_in_sample(xs, msm, w, l, tabs_s, bs, ss)
        conv_ns, cstate_s = _conv_in(xs, msm, w, l, state_conv, l, bs, ss, nb=bs, t=ss)
        attn_s = _sample_attention(qa, qpe, cache_kv_latent, past_kpe_t, ckv_s, kpe_s,
                                   w["w_uv_h"], l)
        xs, hs, w_out_l = _out_proj(xs, attn_s, conv_ns, msm, w, w["w_out"], l, 1, ms, tm=ms)
        xs, w_up_l, w_down_l = _mlp(xs, hs, msm, w["w_up"], w["w_down"], l, fnorm, 1, ms,
                                    tm=ms, tf=512, final=last)
        outs[1].append(ckv_s.reshape(bs, ss, KV_LORA))
        outs[2].append(kpe_s.reshape(bs, ss, ROPE_DIM))
        outs[3].append(cstate_s)

        q, k, v, *stacks = _attn_in_prompt(xp, mp, w, l, tabs_p, bp, sp, stacks)
        conv_n, cstate = _conv_in(xp, mp, w, l, zero_conv, 0, bp, sp, nb=1, t=512, nsub=2)
        attn = _prompt_attention(q, k, v, bp, sp)
        xp, hp = _out_proj(xp, attn, conv_n, mp, w, w_out_l, None, bp, sp, tm=512, nsub=2)
        xp, = _mlp(xp, hp, mp, w_up_l, w_down_l, None, fnorm, bp, sp, tm=1024, tf=1024, final=last)
        outs[0].append(cstate)

    return (xp.reshape(bp, sp, d), xs.reshape(bs, ss, d),
            stacks[0].reshape(depth, bp, sp, KV_LORA), jnp.swapaxes(stacks[1], 2, 3),
            jnp.stack(outs[0]), jnp.stack(outs[1]), jnp.stack(outs[2]), jnp.stack(outs[3]))
```

```python
import functools
import math

import jax
import jax.numpy as jnp
from jax import lax
from jax.experimental import pallas as pl
from jax.experimental.pallas import tpu as pltpu

F32 = jnp.float32
BF16 = jnp.bfloat16

CHUNK = 64
N_HEADS = 8
NOPE_DIM = 128
ROPE_DIM = 64
V_DIM = 128
QK_DIM = NOPE_DIM + ROPE_DIM
Q_LORA = 512
KV_LORA = 512
D_ATTN = N_HEADS * V_DIM
CONV_W = 3
N_MOD = 6
ROPE_BASE = 10000.0
EPS = 1e-6
ATTN_SCALE = QK_DIM ** -0.5

LANES = 128
HEAD_W = 2 * LANES
ROPE_W = LANES
IN_A_W = Q_LORA + KV_LORA + ROPE_W
CONV_PAD = 8
Q_SCALE = ATTN_SCALE * math.log2(math.e)
NEG_BIG = -1e30
VMEM_LIMIT = 60000 * 1024


def _cparams(n_axes):
    return pltpu.CompilerParams(dimension_semantics=("arbitrary",) * n_axes,
                                vmem_limit_bytes=VMEM_LIMIT)


def _rms(x):
    return x * lax.rsqrt(jnp.mean(x * x, axis=-1, keepdims=True) + EPS)


def _dot(a, b):
    return jnp.dot(a, b, preferred_element_type=F32)


def _dot_t(a, b):
    return lax.dot_general(a, b, (((1,), (1,)), ((), ())), preferred_element_type=F32)


def _layer_spec(arr, layer, block=None, index=None):
    block = tuple(arr.shape[1:]) if block is None else tuple(block)
    index = (0,) * len(block) if index is None else tuple(index)
    return pl.BlockSpec((None,) + block, lambda *g: (layer,) + index,
                        pipeline_mode=pl.Buffered(1))


def _rows(tm, cols, tiles_per_batch):
    return pl.BlockSpec((tm, cols), lambda b, s, *_: (b * tiles_per_batch + s, 0))


class _Mod:
    def __init__(self, arr, layer, per_batch):
        self.arr, self.layer, self.per_batch = arr, layer, per_batch

    def spec(self, role):
        layer, nb = self.layer, self.per_batch
        if nb:
            return pl.BlockSpec((None,) + self.arr.shape[1:],
                                lambda b, *_: ((layer * nb + b) * N_MOD + role, 0, 0))
        return pl.BlockSpec((None,) + self.arr.shape[1:],
                            lambda *_: (layer * N_MOD + role, 0, 0))


def _mod_kernel(c_ref, w_ref, b_ref, o_ref):
    c = c_ref[...]
    s = (c * jax.nn.sigmoid(c)).astype(BF16)
    o_ref[0] = _dot(s, w_ref[0].astype(BF16)) + b_ref[0]


def _modulation(c_all, w_ada, b_ada, tn=1024):
    depth, d, n = w_ada.shape
    r = c_all.shape[0]
    return pl.pallas_call(
        _mod_kernel,
        grid=(depth, n // tn),
        in_specs=[pl.BlockSpec((r, d), lambda l, j: (0, 0)),
                  pl.BlockSpec((1, d, tn), lambda l, j: (l, 0, j)),
                  pl.BlockSpec((1, 1, tn), lambda l, j: (l, 0, j))],
        out_specs=pl.BlockSpec((1, r, tn), lambda l, j: (l, 0, j)),
        out_shape=jax.ShapeDtypeStruct((depth, r, n), F32),
        compiler_params=_cparams(2),
        name="modulation",
    )(c_all, w_ada, b_ada.reshape(depth, 1, n))


def _rope128(p, cos, s1, s2):
    return p * cos + pltpu.roll(p, 96, 1) * s1 + pltpu.roll(p, 32, 1) * s2


def _adaln(x_ref, rs, sh_ref, sc_ref):
    return (_rms(x_ref[rs, :]) * (1.0 + sc_ref[...]) + sh_ref[...]).astype(BF16)


def _attn_in_common(h, rs, win_ref, qn_ref, wuq_ref, kvn_ref, cos_ref, s1_ref, s2_ref):
    z = _dot(h, win_ref[...])
    cos, s1, s2 = cos_ref[rs, :], s1_ref[rs, :], s2_ref[rs, :]
    qn = (_rms(z[:, :Q_LORA]) * qn_ref[...]).astype(BF16)
    q = _dot(qn, wuq_ref[...]) * Q_SCALE
    ckv = _rms(z[:, Q_LORA:Q_LORA + KV_LORA]) * kvn_ref[...]
    kp = _rope128(z[:, Q_LORA + KV_LORA:], cos, s1, s2)
    rope = functools.partial(_rope128, cos=cos, s1=s1, s2=s2)
    return q, ckv, kp, rope


def _conv_group(h, p0, tg, nb, w_ref, cw_ref, cn_ref, buf):
    dc = cw_ref.shape[-1]
    z = _dot(h, w_ref[...])
    gate_b = z[:, :dc]
    cu = z[:, dc:2 * dc] * z[:, 2 * dc:]
    buf[:, CONV_PAD + p0:CONV_PAD + p0 + tg, :] = cu.reshape(nb, tg, dc)
    base = CONV_PAD - (CONV_W - 1) + p0
    y = cw_ref[0:1, :].reshape(1, 1, dc) * buf[:, base:base + tg, :]
    for k in range(1, CONV_W):
        y = y + cw_ref[k:k + 1, :].reshape(1, 1, dc) * buf[:, base + k:base + k + tg, :]
    conv_o = gate_b * y.reshape(nb * tg, dc)
    return (_rms(conv_o) * cn_ref[...]).astype(BF16)


def _conv_carry(buf, st_ref, t):
    last = buf[:, CONV_PAD + t - 2:CONV_PAD + t, :]
    st_ref[...] = last
    buf[:, CONV_PAD - 2:CONV_PAD, :] = last


def _mixer_in_prompt_kernel(x_ref, sh_ref, sc_ref, win_ref, qn_ref, wuq_ref, kvn_ref,
                            cos_ref, s1_ref, s2_ref, wuk_ref, wuv_ref,
                            winb_ref, cw_ref, cn_ref, prev_ref, *rest, nsub, first):
    q_out, k_out, v_out, ckv_out, kpe_out, conv_out, st_ref, buf = rest[-8:]

    @pl.when(pl.program_id(1) == 0)
    def _():
        buf[:, CONV_PAD - 2:CONV_PAD, :] = prev_ref[...]

    if first:
        ckv_out[1:] = jnp.zeros((ckv_out.shape[0] - 1,) + ckv_out.shape[1:], F32)
        kpe_out[1:] = jnp.zeros((kpe_out.shape[0] - 1,) + kpe_out.shape[1:], F32)
        ckv_out, kpe_out = ckv_out.at[0], kpe_out.at[0]
    rows = x_ref.shape[0] // nsub
    lane = lax.broadcasted_iota(jnp.int32, (rows, LANES), 1)
    ones_col = jnp.where(lane == 0, 1.0, 0.0).astype(BF16)
    for j in range(nsub):
        rs = slice(j * rows, (j + 1) * rows)
        h = _adaln(x_ref, rs, sh_ref, sc_ref)
        conv_out[rs, :] = _conv_group(h, j * rows, rows, 1, winb_ref, cw_ref, cn_ref, buf)
        q, ckv, kp, rope = _attn_in_common(h, rs, win_ref, qn_ref, wuq_ref, kvn_ref,
                                           cos_ref, s1_ref, s2_ref)
        ckv_out[rs, :] = ckv
        kpe_out[:, rs] = kp.T[:ROPE_DIM, :]
        ckb = ckv.astype(BF16)
        kn = _dot(ckb, wuk_ref[...])
        v = _dot(ckb, wuv_ref[...]).astype(BF16)
        kpb = kp.astype(BF16)
        for hd in range(N_HEADS):
            c0 = hd * HEAD_W
            v_out[rs, c0:c0 + V_DIM] = v[:, hd * V_DIM:(hd + 1) * V_DIM]
            v_out[rs, c0 + V_DIM:c0 + HEAD_W] = ones_col
            q_out[rs, c0:c0 + NOPE_DIM] = q[:, c0:c0 + NOPE_DIM].astype(BF16)
            q_out[rs, c0 + NOPE_DIM:c0 + HEAD_W] = rope(q[:, c0 + NOPE_DIM:c0 + HEAD_W]).astype(BF16)
            k_out[rs, c0:c0 + NOPE_DIM] = kn[:, hd * NOPE_DIM:(hd + 1) * NOPE_DIM].astype(BF16)
            k_out[rs, c0 + NOPE_DIM:c0 + HEAD_W] = kpb
    _conv_carry(buf, st_ref, x_ref.shape[0])


def _attn_in_sample_kernel(x_ref, sh_ref, sc_ref, win_ref, qn_ref, wuq_ref, kvn_ref,
                           cos_ref, s1_ref, s2_ref, wukt_ref,
                           qa_out, qpe_out, ckv_out, kpe_out, *, nb, t):
    h = _adaln(x_ref, slice(None), sh_ref, sc_ref)
    q, ckv, kp, rope = _attn_in_common(h, slice(None), win_ref, qn_ref, wuq_ref, kvn_ref,
                                       cos_ref, s1_ref, s2_ref)
    ckv_out[...] = ckv
    kpe_out[...] = kp[:, :ROPE_DIM]
    for hd in range(N_HEADS):
        c0 = hd * HEAD_W
        qa = _dot(q[:, c0:c0 + NOPE_DIM].astype(BF16), wukt_ref[hd])
        qa_out[:, hd] = qa.reshape(nb, t, KV_LORA).astype(BF16)
        qp = rope(q[:, c0 + NOPE_DIM:c0 + HEAD_W])
        qpe_out[:, hd] = qp.reshape(nb, t, ROPE_W).astype(BF16)


def _mixer_in_prompt(x, mod, w, layer, tabs, prev, nbatch, seq, stacks, tm, nsub):
    m, d = x.shape
    depth = w["w_in"].shape[0]
    dc = w["conv_w"].shape[-1]
    nt = seq // tm
    tab_spec = pl.BlockSpec((tm, ROPE_W), lambda b, s: (s, 0))
    ins = [x, mod.arr, mod.arr, w["w_in"], w["q_norm"], w["w_uq"], w["kv_norm"],
           *tabs, w["w_uk"], w["w_uv"], w["w_in_b"], w["conv_w"], w["conv_out_norm"], prev]
    specs = [_rows(tm, d, nt), mod.spec(0), mod.spec(1),
             _layer_spec(w["w_in"], layer, (d, IN_A_W)),
             _layer_spec(w["q_norm"], layer), _layer_spec(w["w_uq"], layer),
             _layer_spec(w["kv_norm"], layer), tab_spec, tab_spec, tab_spec,
             _layer_spec(w["w_uk"], layer), _layer_spec(w["w_uv"], layer),
             _layer_spec(w["w_in_b"], layer), _layer_spec(w["conv_w"], layer),
             _layer_spec(w["conv_out_norm"], layer),
             pl.BlockSpec((None, 1, CONV_W - 1, dc), lambda b, s: (0, b, 0, 0))]
    aliases = {}
    if stacks is not None:
        aliases = {len(ins): 3, len(ins) + 1: 4}
        ins += list(stacks)
        specs += [pl.BlockSpec(memory_space=pl.ANY)] * 2
    outs = [jax.ShapeDtypeStruct((m, N_HEADS * HEAD_W), BF16),
            jax.ShapeDtypeStruct((m, N_HEADS * HEAD_W), BF16),
            jax.ShapeDtypeStruct((m, N_HEADS * HEAD_W), BF16),
            jax.ShapeDtypeStruct((depth, m, KV_LORA), F32),
            jax.ShapeDtypeStruct((depth, nbatch, ROPE_DIM, seq), F32),
            jax.ShapeDtypeStruct((m, dc), BF16),
            jax.ShapeDtypeStruct((nbatch, CONV_W - 1, dc), F32)]
    first = stacks is None
    lsel = (depth, 0) if first else (None, layer)
    out_specs = [_rows(tm, N_HEADS * HEAD_W, nt), _rows(tm, N_HEADS * HEAD_W, nt),
                 _rows(tm, N_HEADS * HEAD_W, nt),
                 pl.BlockSpec((lsel[0], tm, KV_LORA), lambda b, s: (lsel[1], b * nt + s, 0)),
                 pl.BlockSpec((lsel[0], None, ROPE_DIM, tm), lambda b, s: (lsel[1], b, 0, s)),
                 _rows(tm, dc, nt),
                 pl.BlockSpec((1, CONV_W - 1, dc), lambda b, s: (b, 0, 0))]
    return pl.pallas_call(
        functools.partial(_mixer_in_prompt_kernel, nsub=nsub, first=first), grid=(nbatch, nt),
        in_specs=specs, out_specs=out_specs, out_shape=outs, input_output_aliases=aliases,
        scratch_shapes=[pltpu.VMEM((1, CONV_PAD + tm, dc), F32)],
        compiler_params=_cparams(2), name="mixer_in_prompt",
    )(*ins)


def _attn_in_sample(x, mod, w, layer, tabs, nb, t):
    m, d = x.shape
    ins = [x, mod.arr, mod.arr, w["w_in"], w["q_norm"], w["w_uq"], w["kv_norm"],
           *tabs, w["w_uk_t"]]
    full = lambda a: pl.BlockSpec(a.shape, lambda *_: (0,) * a.ndim)
    specs = [_rows(m, d, 1), mod.spec(0), mod.spec(1),
             _layer_spec(w["w_in"], layer, (d, IN_A_W)),
             _layer_spec(w["q_norm"], layer), _layer_spec(w["w_uq"], layer),
             _layer_spec(w["kv_norm"], layer), full(tabs[0]), full(tabs[1]), full(tabs[2]),
             _layer_spec(w["w_uk_t"], layer)]
    outs = [jax.ShapeDtypeStruct((nb, N_HEADS, t, KV_LORA), BF16),
            jax.ShapeDtypeStruct((nb, N_HEADS, t, ROPE_W), BF16),
            jax.ShapeDtypeStruct((m, KV_LORA), F32),
            jax.ShapeDtypeStruct((m, ROPE_DIM), F32)]
    out_specs = [pl.BlockSpec((nb, N_HEADS, t, KV_LORA), lambda b, s: (0, 0, 0, 0)),
                 pl.BlockSpec((nb, N_HEADS, t, ROPE_W), lambda b, s: (0, 0, 0, 0)),
                 _rows(m, KV_LORA, 1), _rows(m, ROPE_DIM, 1)]
    return pl.pallas_call(
        functools.partial(_attn_in_sample_kernel, nb=nb, t=t), grid=(1, 1),
        in_specs=specs, out_specs=out_specs, out_shape=outs,
        compiler_params=_cparams(2), name="attn_in_sample",
    )(*ins)


def _conv_in_kernel(x_ref, sh_ref, sc_ref, w_ref, cw_ref, cn_ref, prev_ref,
                    o_ref, st_ref, buf, *, nb, t):
    @pl.when(pl.program_id(1) == 0)
    def _():
        buf[:, CONV_PAD - 2:CONV_PAD, :] = prev_ref[...]

    h = _adaln(x_ref, slice(None), sh_ref, sc_ref)
    o_ref[...] = _conv_group(h, 0, t, nb, w_ref, cw_ref, cn_ref, buf)
    _conv_carry(buf, st_ref, t)


def _conv_in(x, mod, w, layer, prev, prev_layer, nbatch, seq, nb, t):
    m, d = x.shape
    dc = w["conv_w"].shape[-1]
    nt = seq // t
    ins = [x, mod.arr, mod.arr, w["w_in_b"], w["conv_w"], w["conv_out_norm"], prev]
    specs = [_rows(nb * t, d, nt), mod.spec(0), mod.spec(1), _layer_spec(w["w_in_b"], layer),
             _layer_spec(w["conv_w"], layer), _layer_spec(w["conv_out_norm"], layer),
             pl.BlockSpec((None, nb, CONV_W - 1, dc), lambda b, s: (prev_layer, b, 0, 0))]
    outs = [jax.ShapeDtypeStruct((m, dc), BF16),
            jax.ShapeDtypeStruct(prev.shape[1:], F32)]
    out_specs = [_rows(nb * t, dc, nt),
                 pl.BlockSpec((nb, CONV_W - 1, dc), lambda b, s: (b, 0, 0))]
    return pl.pallas_call(
        functools.partial(_conv_in_kernel, nb=nb, t=t),
        grid=(nbatch // nb, nt), in_specs=specs, out_specs=out_specs, out_shape=outs,
        scratch_shapes=[pltpu.VMEM((nb, CONV_PAD + t, dc), F32)],
        compiler_params=_cparams(2), name="conv_in",
    )(*ins)


def _prompt_attn_kernel(q_ref, k_ref, v_ref, o_ref, s_buf, p_buf, *, seq, tq, heads):
    row_chunk = lax.broadcasted_iota(jnp.int32, (tq, tq), 0) // CHUNK
    col_chunk = lax.broadcasted_iota(jnp.int32, (tq, tq), 1) // CHUNK
    visible = col_chunk <= row_chunk
    tiles = [(hd, qi) for hd in range(heads) for qi in range(seq // tq)]

    def scores(i):
        hd, qi = tiles[i]
        hi = (qi + 1) * tq
        cols = slice(hd * HEAD_W, (hd + 1) * HEAD_W)
        s_buf[i % 2, :, 0:hi] = _dot_t(q_ref[qi * tq:hi, cols], k_ref[0:hi, cols])

    scores(0)
    for i, (hd, qi) in enumerate(tiles):
        lo, hi = qi * tq, (qi + 1) * tq
        if i + 1 < len(tiles):
            scores(i + 1)
        s_d = jnp.where(visible, s_buf[i % 2, :, lo:hi], NEG_BIG)
        m = jnp.max(s_d, axis=-1, keepdims=True)
        if qi > 0:
            m = jnp.maximum(m, jnp.max(s_buf[i % 2, :, 0:lo], axis=-1, keepdims=True))
            p_buf[:, 0:lo] = jnp.exp2(s_buf[i % 2, :, 0:lo] - m).astype(BF16)
        p_buf[:, lo:hi] = jnp.exp2(s_d - m).astype(BF16)
        oe = _dot(p_buf[:, 0:hi], v_ref[0:hi, hd * HEAD_W:(hd + 1) * HEAD_W])
        o_ref[lo:hi, hd * V_DIM:(hd + 1) * V_DIM] = oe[:, :V_DIM] / oe[:, V_DIM:V_DIM + 1]


def _prompt_attention(q, k, v, nbatch, seq, tq=256, heads=2):
    m = q.shape[0]
    qkv_spec = pl.BlockSpec((seq, heads * HEAD_W), lambda b, h: (b, h))
    return pl.pallas_call(
        functools.partial(_prompt_attn_kernel, seq=seq, tq=tq, heads=heads),
        grid=(nbatch, N_HEADS // heads),
        in_specs=[qkv_spec, qkv_spec, qkv_spec],
        out_specs=pl.BlockSpec((seq, heads * V_DIM), lambda b, h: (b, h)),
        out_shape=jax.ShapeDtypeStruct((m, D_ATTN), F32),
        scratch_shapes=[pltpu.VMEM((2, tq, seq), F32), pltpu.VMEM((tq, seq), BF16)],
        compiler_params=_cparams(2), name="prompt_attention",
    )(q, k, v)


def _sample_attn_kernel(qa_ref, qpe_ref, pc_ref, pk_ref, nc_ref, nk_ref, wuv_ref, o_ref, *, t):
    rows = N_HEADS * t
    qa = qa_ref[0].reshape(rows, KV_LORA)
    qp = qpe_ref[0].reshape(rows, ROPE_W)[:, :ROPE_DIM]
    pc = pc_ref[0].astype(BF16)
    pk_t = pk_ref[0].astype(BF16)
    nc = nc_ref[...].astype(BF16)
    nk = nk_ref[...].astype(BF16)
    s_p = _dot_t(qa, pc) + _dot(qp, pk_t)
    s_n = _dot_t(qa, nc) + _dot_t(qp, nk)
    m = jnp.maximum(jnp.max(s_p, axis=-1, keepdims=True), jnp.max(s_n, axis=-1, keepdims=True))
    p_p = jnp.exp2(s_p - m)
    p_n = jnp.exp2(s_n - m)
    l = jnp.sum(p_p, axis=-1, keepdims=True) + jnp.sum(p_n, axis=-1, keepdims=True)
    o_lat = (_dot(p_p.astype(BF16), pc) + _dot(p_n.astype(BF16), nc)) / l
    o_lat = o_lat.astype(BF16)
    for hd in range(N_HEADS):
        o_ref[:, hd * V_DIM:(hd + 1) * V_DIM] = _dot(o_lat[hd * t:(hd + 1) * t, :], wuv_ref[hd])


def _sample_attention(qa, qpe, past_ckv, past_kpe_t, new_ckv, new_kpe, w_uv_h, layer):
    nb, _, t, _ = qa.shape
    past = past_ckv.shape[2]
    return pl.pallas_call(
        functools.partial(_sample_attn_kernel, t=t),
        grid=(nb,),
        in_specs=[pl.BlockSpec((1, N_HEADS, t, KV_LORA), lambda b: (b, 0, 0, 0)),
                  pl.BlockSpec((1, N_HEADS, t, ROPE_W), lambda b: (b, 0, 0, 0)),
                  pl.BlockSpec((None, 1, past, KV_LORA), lambda b: (layer, b, 0, 0)),
                  pl.BlockSpec((None, 1, ROPE_DIM, past), lambda b: (layer, b, 0, 0)),
                  pl.BlockSpec((t, KV_LORA), lambda b: (b, 0)),
                  pl.BlockSpec((t, ROPE_DIM), lambda b: (b, 0)),
                  _layer_spec(w_uv_h, layer)],
        out_specs=pl.BlockSpec((t, D_ATTN), lambda b: (b, 0)),
        out_shape=jax.ShapeDtypeStruct((nb * t, D_ATTN), F32),
        compiler_params=_cparams(1), name="sample_attention",
    )(qa, qpe, past_ckv, past_kpe_t, new_ckv, new_kpe, w_uv_h)


def _out_proj_kernel(x_ref, attn_ref, conv_ref, g_ref, shm_ref, scm_ref, an_ref, woa_ref, woc_ref,
                     o_ref, h_ref, *bf16_w_out, nsub):
    if bf16_w_out:
        wo_ref, = bf16_w_out
        da = woa_ref.shape[0]
        wo_ref[0:da, :] = woa_ref[...].astype(BF16)
        wo_ref[da:, :] = woc_ref[...].astype(BF16)
        woa, woc = wo_ref[0:da, :], wo_ref[da:, :]
    else:
        woa, woc = woa_ref[...], woc_ref[...]
    rows = x_ref.shape[0] // nsub
    for j in range(nsub):
        rs = slice(j * rows, (j + 1) * rows) if nsub > 1 else slice(None)
        a = (_rms(attn_ref[rs, :]) * an_ref[...]).astype(BF16)
        mix = _dot(a, woa) + _dot(conv_ref[rs, :], woc)
        y = x_ref[rs, :] + g_ref[...] * mix
        o_ref[rs, :] = y
        h_ref[rs, :] = (_rms(y) * (1.0 + scm_ref[...]) + shm_ref[...]).astype(BF16)


def _out_proj(x, attn, conv_n, mod, w, w_out, layer, nbatch, seq, tm, nsub=1):
    m, d = x.shape
    nt = seq // tm
    da, dc = attn.shape[1], conv_n.shape[1]
    ins = [x, attn, conv_n, mod.arr, mod.arr, mod.arr, w["attn_out_norm"], w_out, w_out]
    specs = [_rows(tm, d, nt), _rows(tm, da, nt), _rows(tm, dc, nt),
             mod.spec(2), mod.spec(3), mod.spec(4),
             _layer_spec(w["attn_out_norm"], mod.layer)]
    out_specs = [_rows(tm, d, nt), _rows(tm, d, nt)]
    out_shape = [jax.ShapeDtypeStruct((m, d), F32), jax.ShapeDtypeStruct((m, d), BF16)]
    if layer is None:
        specs += [pl.BlockSpec((da, d), lambda b, s: (0, 0), pipeline_mode=pl.Buffered(1)),
                  pl.BlockSpec((dc, d), lambda b, s: (da // dc, 0), pipeline_mode=pl.Buffered(1))]
    else:
        assert nbatch * nt == 1
        specs += [_layer_spec(w_out, layer, (da, d), (0, 0)),
                  _layer_spec(w_out, layer, (dc, d), (da // dc, 0))]
        out_specs.append(pl.BlockSpec((da + dc, d), lambda b, s: (0, 0)))
        out_shape.append(jax.ShapeDtypeStruct((da + dc, d), BF16))
    return pl.pallas_call(
        functools.partial(_out_proj_kernel, nsub=nsub), grid=(nbatch, nt), in_specs=specs,
        out_specs=out_specs, out_shape=out_shape,
        compiler_params=_cparams(2), name="out_proj",
    )(*ins)


def _mlp_kernel(x_ref, h_ref, g_ref, wup_ref, wdn_ref, fn_ref, o_ref, *extra, nf, final, cast,
                x_rows):
    f = pl.program_id(2)
    if cast:
        wu_ref, wd_ref = extra
        wu_ref[...] = wup_ref[...].astype(BF16)
        wd_ref[...] = wdn_ref[...].astype(BF16)
    else:
        wu_ref, wd_ref = wup_ref, wdn_ref
    if x_rows:
        x_buf, x_sem = extra
        tile = pl.program_id(0) * pl.num_programs(1) + pl.program_id(1)
        row0 = pl.multiple_of(tile * x_rows, x_rows)
        x_copy = pltpu.make_async_copy(x_ref.at[pl.ds(row0, x_rows), :], x_buf, x_sem)

        @pl.when(f == 0)
        def _():
            x_copy.start()

    def partial_down():
        up = _dot(h_ref[...], wu_ref[...])
        act = jnp.square(jnp.maximum(up, 0.0)).astype(BF16)
        return _dot(act, wd_ref[...])

    @pl.when(f == 0)
    def _():
        o_ref[...] = partial_down()

    @pl.when(f > 0)
    def _():
        o_ref[...] += partial_down()

    @pl.when(f == nf - 1)
    def _():
        if x_rows:
            x_copy.wait()
            x = x_buf[...]
        else:
            x = x_ref[...]
        y = x + g_ref[...] * o_ref[...]
        if final:
            y = _rms(y) * fn_ref[...]
        o_ref[...] = y


def _mlp(x, h, mod, w_up, w_down, layer, final_norm, nbatch, seq, tm, tf, final):
    m, d = x.shape
    nt = seq // tm
    dff = w_up.shape[-1]
    nf = dff // tf
    cast = layer is not None
    out_specs = [_rows(tm, d, nt)]
    out_shape = [jax.ShapeDtypeStruct((m, d), F32)]
    scratch = []
    if cast:
        assert nbatch * nt == 1
        x_spec = _rows(tm, d, nt)
        w_specs = [pl.BlockSpec((None, d, tf), lambda b, s, f: (layer, 0, f)),
                   pl.BlockSpec((None, tf, d), lambda b, s, f: (layer, f, 0))]
        out_specs += [pl.BlockSpec((d, tf), lambda b, s, f: (0, f)),
                      pl.BlockSpec((tf, d), lambda b, s, f: (f, 0))]
        out_shape += [jax.ShapeDtypeStruct((d, dff), BF16), jax.ShapeDtypeStruct((dff, d), BF16)]
    else:
        x_spec = pl.BlockSpec(memory_space=pl.ANY)
        w_specs = [pl.BlockSpec((d, tf), lambda b, s, f: (0, f)),
                   pl.BlockSpec((tf, d), lambda b, s, f: (f, 0))]
        scratch = [pltpu.VMEM((tm, d), F32), pltpu.SemaphoreType.DMA(())]
    return pl.pallas_call(
        functools.partial(_mlp_kernel, nf=nf, final=final, cast=cast,
                          x_rows=0 if cast else tm),
        grid=(nbatch, nt, nf),
        in_specs=[x_spec, _rows(tm, d, nt), mod.spec(5), *w_specs,
                  pl.BlockSpec(final_norm.shape, lambda b, s, f: (0, 0))],
        out_specs=out_specs, out_shape=out_shape, scratch_shapes=scratch,
        compiler_params=_cparams(3), name="mlp",
    )(x, h, mod.arr, w_up, w_down, final_norm)


def _rope_tables(pos):
    half = ROPE_DIM // 2
    inv = ROPE_BASE ** (-jnp.arange(half, dtype=F32) / half)
    ang = pos.astype(F32)[:, None] * inv[None, :]
    cos, sin, zero = jnp.cos(ang), jnp.sin(ang), jnp.zeros_like(ang)
    return (jnp.concatenate([cos, cos, zero, zero], axis=-1),
            jnp.concatenate([-sin, zero, zero, zero], axis=-1),
            jnp.concatenate([zero, sin, zero, zero], axis=-1))


def _prep_weights(w_in, q_norm, w_uq, kv_norm, w_ukv, conv_w, attn_out_norm, conv_out_norm,
                  w_out, w_up, w_down):
    depth = w_in.shape[0]
    w_in = w_in.astype(BF16)
    wq = w_uq.astype(BF16).reshape(depth, Q_LORA, N_HEADS, QK_DIM)
    wq = jnp.pad(wq, ((0, 0), (0, 0), (0, 0), (0, HEAD_W - QK_DIM)))
    wkv = w_ukv.astype(BF16).reshape(depth, KV_LORA, N_HEADS, NOPE_DIM + V_DIM)
    w_uk = wkv[..., :NOPE_DIM]
    w_uv = wkv[..., NOPE_DIM:]
    return dict(
        w_in=w_in,
        w_in_b=w_in[:, :, Q_LORA + KV_LORA + ROPE_DIM:],
        w_uq=wq.reshape(depth, Q_LORA, N_HEADS * HEAD_W),
        w_uk=w_uk.reshape(depth, KV_LORA, N_HEADS * NOPE_DIM),
        w_uv=w_uv.reshape(depth, KV_LORA, N_HEADS * V_DIM),
        w_uk_t=jnp.transpose(w_uk, (0, 2, 3, 1)),
        w_uv_h=jnp.transpose(w_uv, (0, 2, 1, 3)),
        q_norm=q_norm[:, None, :], kv_norm=kv_norm[:, None, :],
        conv_w=conv_w, attn_out_norm=attn_out_norm[:, None, :],
        conv_out_norm=conv_out_norm[:, None, :],
        w_out=w_out, w_up=w_up, w_down=w_down)


def kernel(x_prompt, x_sample, c_prompt, c_sample, cache_kv_latent, cache_k_rope, state_conv,
           w_ada, b_ada, w_in, q_norm, w_uq, kv_norm, w_ukv, conv_w,
           attn_out_norm, conv_out_norm, w_out, w_up, w_down, final_norm):
    bp, sp, d = x_prompt.shape
    bs, ss, _ = x_sample.shape
    depth = w_ada.shape[0]
    past = cache_kv_latent.shape[2]
    dc = conv_w.shape[-1]
    ms = bs * ss

    w = _prep_weights(w_in, q_norm, w_uq, kv_norm, w_ukv, conv_w, attn_out_norm,
                      conv_out_norm, w_out, w_up, w_down)
    fnorm = final_norm[None, :]
    tabs_p = _rope_tables(jnp.arange(sp))
    tabs_s = tuple(jnp.tile(tb, (bs, 1)) for tb in _rope_tables(past + jnp.arange(ss)))

    mod = _modulation(jnp.concatenate([c_prompt, c_sample], axis=0), w_ada, b_ada)
    mod = mod.reshape(depth, bp + bs, N_MOD, d)
    mod_p = mod[:, :bp].reshape(depth * bp * N_MOD, 1, d)
    mod_s = jnp.repeat(jnp.swapaxes(mod[:, bp:], 1, 2), ss, axis=2).reshape(depth * N_MOD, ms, d)
    zero_conv = jnp.zeros((1, bp, CONV_W - 1, dc), F32)

    xp = x_prompt.reshape(bp * sp, d)
    xs = x_sample.reshape(ms, d)
    past_kpe_t = jnp.swapaxes(cache_k_rope, 2, 3)
    stacks = None
    outs = [[] for _ in range(4)]
    for l in range(depth):
        last = l == depth - 1
        mp = _Mod(mod_p, l, bp)
        msm = _Mod(mod_s, l, 0)

        qa, qpe, ckv_s, kpe_s = _attn_in_sample(xs, msm, w, l, tabs_s, bs, ss)
        conv_ns, cstate_s = _conv_in(xs, msm, w, l, state_conv, l, bs, ss, nb=bs, t=ss)
        attn_s = _sample_attention(qa, qpe, cache_kv_latent, past_kpe_t, ckv_s, kpe_s,
                                   w["w_uv_h"], l)
        xs, hs, w_out_l = _out_proj(xs, attn_s, conv_ns, msm, w, w["w_out"], l, 1, ms, tm=ms)
        xs, w_up_l, w_down_l = _mlp(xs, hs, msm, w["w_up"], w["w_down"], l, fnorm, 1, ms,
                                    tm=ms, tf=512, final=last)
        outs[1].append(ckv_s.reshape(bs, ss, KV_LORA))
        outs[2].append(kpe_s.reshape(bs, ss, ROPE_DIM))
        outs[3].append(cstate_s)

        q, k, v, *stacks, conv_n, cstate = _mixer_in_prompt(
            xp, mp, w, l, tabs_p, zero_conv, bp, sp, stacks, tm=256, nsub=1)
        attn = _prompt_attention(q, k, v, bp, sp)
        xp, hp = _out_proj(xp, attn, conv_n, mp, w, w_out_l, None, bp, sp, tm=512, nsub=2)
        xp, = _mlp(xp, hp, mp, w_up_l, w_down_l, None, fnorm, bp, sp, tm=1024, tf=1024, final=last)
        outs[0].append(cstate)

    return (xp.reshape(bp, sp, d), xs.reshape(bs, ss, d),
            stacks[0].reshape(depth, bp, sp, KV_LORA), jnp.swapaxes(stacks[1], 2, 3),
            jnp.stack(outs[0]), jnp.stack(outs[1]), jnp.stack(outs[2]), jnp.stack(outs[3]))
```

```python
import functools
import math

import jax
import jax.numpy as jnp
from jax import lax
from jax.experimental import pallas as pl
from jax.experimental.pallas import tpu as pltpu

F32 = jnp.float32
BF16 = jnp.bfloat16

CHUNK = 64
N_HEADS = 8
NOPE_DIM = 128
ROPE_DIM = 64
V_DIM = 128
QK_DIM = NOPE_DIM + ROPE_DIM
Q_LORA = 512
KV_LORA = 512
D_ATTN = N_HEADS * V_DIM
CONV_W = 3
N_MOD = 6
ROPE_BASE = 10000.0
EPS = 1e-6
ATTN_SCALE = QK_DIM ** -0.5

LANES = 128
HEAD_W = 2 * LANES
ROPE_W = LANES
IN_A_W = Q_LORA + KV_LORA + ROPE_W
CONV_PAD = 8
Q_SCALE = ATTN_SCALE * math.log2(math.e)
NEG_BIG = -1e30
VMEM_LIMIT = 60000 * 1024


def _cparams(n_axes):
    return pltpu.CompilerParams(dimension_semantics=("arbitrary",) * n_axes,
                                vmem_limit_bytes=VMEM_LIMIT)


def _rms(x):
    return x * lax.rsqrt(jnp.mean(x * x, axis=-1, keepdims=True) + EPS)


def _dot(a, b):
    return jnp.dot(a, b, preferred_element_type=F32)


def _dot_t(a, b):
    return lax.dot_general(a, b, (((1,), (1,)), ((), ())), preferred_element_type=F32)


def _layer_spec(arr, layer, block=None, index=None):
    block = tuple(arr.shape[1:]) if block is None else tuple(block)
    index = (0,) * len(block) if index is None else tuple(index)
    return pl.BlockSpec((None,) + block, lambda *g: (layer,) + index,
                        pipeline_mode=pl.Buffered(1))


def _rows(tm, cols, tiles_per_batch):
    return pl.BlockSpec((tm, cols), lambda b, s, *_: (b * tiles_per_batch + s, 0))


class _Mod:
    def __init__(self, arr, layer, per_batch):
        self.arr, self.layer, self.per_batch = arr, layer, per_batch

    def spec(self, role):
        layer, nb = self.layer, self.per_batch
        if nb:
            return pl.BlockSpec((None,) + self.arr.shape[1:],
                                lambda b, *_: ((layer * nb + b) * N_MOD + role, 0, 0))
        return pl.BlockSpec((None,) + self.arr.shape[1:],
                            lambda *_: (layer * N_MOD + role, 0, 0))


def _mod_kernel(c_ref, w_ref, b_ref, o_ref):
    c = c_ref[...]
    s = (c * jax.nn.sigmoid(c)).astype(BF16)
    o_ref[0] = _dot(s, w_ref[0].astype(BF16)) + b_ref[0]


def _modulation(c_all, w_ada, b_ada, tn=1024):
    depth, d, n = w_ada.shape
    r = c_all.shape[0]
    return pl.pallas_call(
        _mod_kernel,
        grid=(depth, n // tn),
        in_specs=[pl.BlockSpec((r, d), lambda l, j: (0, 0)),
                  pl.BlockSpec((1, d, tn), lambda l, j: (l, 0, j)),
                  pl.BlockSpec((1, 1, tn), lambda l, j: (l, 0, j))],
        out_specs=pl.BlockSpec((1, r, tn), lambda l, j: (l, 0, j)),
        out_shape=jax.ShapeDtypeStruct((depth, r, n), F32),
        compiler_params=_cparams(2),
        name="modulation",
    )(c_all, w_ada, b_ada.reshape(depth, 1, n))


def _rope128(p, cos, s1, s2):
    return p * cos + pltpu.roll(p, 96, 1) * s1 + pltpu.roll(p, 32, 1) * s2


def _adaln(x_ref, rs, sh_ref, sc_ref):
    return (_rms(x_ref[rs, :]) * (1.0 + sc_ref[...]) + sh_ref[...]).astype(BF16)


def _attn_in_common(h, rs, win_ref, qn_ref, wuq_ref, kvn_ref, cos_ref, s1_ref, s2_ref):
    z = _dot(h, win_ref[...])
    cos, s1, s2 = cos_ref[rs, :], s1_ref[rs, :], s2_ref[rs, :]
    qn = (_rms(z[:, :Q_LORA]) * qn_ref[...]).astype(BF16)
    q = _dot(qn, wuq_ref[...]) * Q_SCALE
    ckv = _rms(z[:, Q_LORA:Q_LORA + KV_LORA]) * kvn_ref[...]
    kp = _rope128(z[:, Q_LORA + KV_LORA:], cos, s1, s2)
    rope = functools.partial(_rope128, cos=cos, s1=s1, s2=s2)
    return q, ckv, kp, rope


def _conv_group(h, p0, tg, nb, w_ref, cw_ref, cn_ref, buf):
    dc = cw_ref.shape[-1]
    z = _dot(h, w_ref[...])
    return _conv_post(z[:, :dc], z[:, dc:2 * dc], z[:, 2 * dc:], p0, tg, nb, cw_ref, cn_ref, buf)


def _conv_post(gate_b, gate_c, u, p0, tg, nb, cw_ref, cn_ref, buf):
    dc = cw_ref.shape[-1]
    cu = gate_c * u
    buf[:, CONV_PAD + p0:CONV_PAD + p0 + tg, :] = cu.reshape(nb, tg, dc)
    base = CONV_PAD - (CONV_W - 1) + p0
    y = cw_ref[0:1, :].reshape(1, 1, dc) * buf[:, base:base + tg, :]
    for k in range(1, CONV_W):
        y = y + cw_ref[k:k + 1, :].reshape(1, 1, dc) * buf[:, base + k:base + k + tg, :]
    conv_o = gate_b * y.reshape(nb * tg, dc)
    return (_rms(conv_o) * cn_ref[...]).astype(BF16)


def _conv_carry(buf, st_ref, t):
    last = buf[:, CONV_PAD + t - 2:CONV_PAD + t, :]
    st_ref[...] = last
    buf[:, CONV_PAD - 2:CONV_PAD, :] = last


def _mixer_in_prompt_kernel(x_ref, sh_ref, sc_ref, win_ref, qn_ref, wuq_ref, kvn_ref,
                            cos_ref, s1_ref, s2_ref, wuk_ref, wuv_ref,
                            winb_ref, cw_ref, cn_ref, prev_ref, *rest, nsub, first):
    q_out, k_out, v_out, ckv_out, kpe_out, conv_out, st_ref, buf = rest[-8:]

    @pl.when(pl.program_id(1) == 0)
    def _():
        buf[:, CONV_PAD - 2:CONV_PAD, :] = prev_ref[...]

    if first:
        ckv_out[1:] = jnp.zeros((ckv_out.shape[0] - 1,) + ckv_out.shape[1:], F32)
        kpe_out[1:] = jnp.zeros((kpe_out.shape[0] - 1,) + kpe_out.shape[1:], F32)
        ckv_out, kpe_out = ckv_out.at[0], kpe_out.at[0]
    rows = x_ref.shape[0] // nsub
    lane = lax.broadcasted_iota(jnp.int32, (rows, LANES), 1)
    ones_col = jnp.where(lane == 0, 1.0, 0.0).astype(BF16)
    for j in range(nsub):
        rs = slice(j * rows, (j + 1) * rows)
        h = _adaln(x_ref, rs, sh_ref, sc_ref)
        conv_out[rs, :] = _conv_group(h, j * rows, rows, 1, winb_ref, cw_ref, cn_ref, buf)
        q, ckv, kp, rope = _attn_in_common(h, rs, win_ref, qn_ref, wuq_ref, kvn_ref,
                                           cos_ref, s1_ref, s2_ref)
        ckv_out[rs, :] = ckv
        kpe_out[:, rs] = kp.T[:ROPE_DIM, :]
        ckb = ckv.astype(BF16)
        kn = _dot(ckb, wuk_ref[...])
        v = _dot(ckb, wuv_ref[...]).astype(BF16)
        kpb = kp.astype(BF16)
        for hd in range(N_HEADS):
            c0 = hd * HEAD_W
            v_out[rs, c0:c0 + V_DIM] = v[:, hd * V_DIM:(hd + 1) * V_DIM]
            v_out[rs, c0 + V_DIM:c0 + HEAD_W] = ones_col
            q_out[rs, c0:c0 + NOPE_DIM] = q[:, c0:c0 + NOPE_DIM].astype(BF16)
            q_out[rs, c0 + NOPE_DIM:c0 + HEAD_W] = rope(q[:, c0 + NOPE_DIM:c0 + HEAD_W]).astype(BF16)
            k_out[rs, c0:c0 + NOPE_DIM] = kn[:, hd * NOPE_DIM:(hd + 1) * NOPE_DIM].astype(BF16)
            k_out[rs, c0 + NOPE_DIM:c0 + HEAD_W] = kpb
    _conv_carry(buf, st_ref, x_ref.shape[0])


def _mixer_in_sample_kernel(x_ref, sh_ref, sc_ref, wint_hbm, qn_ref, wuq_ref, kvn_ref,
                            cos_ref, s1_ref, s2_ref, wukt_ref, cw_ref, cn_ref, prev_ref,
                            qa_out, qpe_out, ckv_out, kpe_out, conv_out, st_ref, wa_out, wb_out,
                            w_buf, w_sem, h_scr, z_scr, buf, *, nb, t, layer):
    s = pl.program_id(0)
    dc = cw_ref.shape[-1]
    n_a = Q_LORA + KV_LORA + ROPE_DIM
    chunk_rows = [(0, IN_A_W)] + [(n_a + j * dc, dc) for j in range(3)]

    def chunk_copy(k):
        r0, rows = chunk_rows[k]
        return pltpu.make_async_copy(wint_hbm.at[layer, pl.ds(r0, rows), :],
                                     w_buf.at[k % 2, pl.ds(0, rows), :], w_sem.at[k % 2])

    for k in range(len(chunk_rows)):
        @pl.when(s == k)
        def _(k=k):
            if k == 0:
                chunk_copy(0).start()
                h_scr[...] = _adaln(x_ref, slice(None), sh_ref, sc_ref)
                buf[:, CONV_PAD - 2:CONV_PAD, :] = prev_ref[...]
            if k + 1 < len(chunk_rows):
                chunk_copy(k + 1).start()
            chunk_copy(k).wait()
            rows = chunk_rows[k][1]
            w_t = w_buf[k % 2, 0:rows, :].T.astype(BF16)
            if k == 0:
                wa_out[...] = w_t
                q, ckv, kp, rope = _attn_in_common(h_scr[...], slice(None), wa_out, qn_ref,
                                                   wuq_ref, kvn_ref, cos_ref, s1_ref, s2_ref)
                ckv_out[...] = ckv
                kpe_out[...] = kp[:, :ROPE_DIM]
                for hd in range(N_HEADS):
                    c0 = hd * HEAD_W
                    qa = _dot(q[:, c0:c0 + NOPE_DIM].astype(BF16), wukt_ref[hd])
                    qa_out[:, hd] = qa.reshape(nb, t, KV_LORA).astype(BF16)
                    qp = rope(q[:, c0 + NOPE_DIM:c0 + HEAD_W])
                    qpe_out[:, hd] = qp.reshape(nb, t, ROPE_W).astype(BF16)
            else:
                wb_out[...] = w_t
                z_scr[k - 1] = _dot(h_scr[...], wb_out[...])
            if k == len(chunk_rows) - 1:
                conv_out[...] = _conv_post(z_scr[0], z_scr[1], z_scr[2], 0, t, nb,
                                           cw_ref, cn_ref, buf)
                _conv_carry(buf, st_ref, t)


def _mixer_in_prompt(x, mod, w, w_a, w_b, layer, tabs, prev, nbatch, seq, stacks, tm, nsub):
    m, d = x.shape
    depth = w["q_norm"].shape[0]
    dc = w["conv_w"].shape[-1]
    nt = seq // tm
    tab_spec = pl.BlockSpec((tm, ROPE_W), lambda b, s: (s, 0))
    whole = lambda a: pl.BlockSpec(a.shape, lambda b, s: (0,) * a.ndim,
                                   pipeline_mode=pl.Buffered(1))
    ins = [x, mod.arr, mod.arr, w_a, w["q_norm"], w["w_uq"], w["kv_norm"],
           *tabs, w["w_uk"], w["w_uv"], w_b, w["conv_w"], w["conv_out_norm"], prev]
    specs = [_rows(tm, d, nt), mod.spec(0), mod.spec(1), whole(w_a),
             _layer_spec(w["q_norm"], layer), _layer_spec(w["w_uq"], layer),
             _layer_spec(w["kv_norm"], layer), tab_spec, tab_spec, tab_spec,
             _layer_spec(w["w_uk"], layer), _layer_spec(w["w_uv"], layer),
             whole(w_b), _layer_spec(w["conv_w"], layer),
             _layer_spec(w["conv_out_norm"], layer),
             pl.BlockSpec((None, 1, CONV_W - 1, dc), lambda b, s: (0, b, 0, 0))]
    aliases = {}
    if stacks is not None:
        aliases = {len(ins): 3, len(ins) + 1: 4}
        ins += list(stacks)
        specs += [pl.BlockSpec(memory_space=pl.ANY)] * 2
    outs = [jax.ShapeDtypeStruct((m, N_HEADS * HEAD_W), BF16),
            jax.ShapeDtypeStruct((m, N_HEADS * HEAD_W), BF16),
            jax.ShapeDtypeStruct((m, N_HEADS * HEAD_W), BF16),
            jax.ShapeDtypeStruct((depth, m, KV_LORA), F32),
            jax.ShapeDtypeStruct((depth, nbatch, ROPE_DIM, seq), F32),
            jax.ShapeDtypeStruct((m, dc), BF16),
            jax.ShapeDtypeStruct((nbatch, CONV_W - 1, dc), F32)]
    first = stacks is None
    lsel = (depth, 0) if first else (None, layer)
    out_specs = [_rows(tm, N_HEADS * HEAD_W, nt), _rows(tm, N_HEADS * HEAD_W, nt),
                 _rows(tm, N_HEADS * HEAD_W, nt),
                 pl.BlockSpec((lsel[0], tm, KV_LORA), lambda b, s: (lsel[1], b * nt + s, 0)),
                 pl.BlockSpec((lsel[0], None, ROPE_DIM, tm), lambda b, s: (lsel[1], b, 0, s)),
                 _rows(tm, dc, nt),
                 pl.BlockSpec((1, CONV_W - 1, dc), lambda b, s: (b, 0, 0))]
    return pl.pallas_call(
        functools.partial(_mixer_in_prompt_kernel, nsub=nsub, first=first), grid=(nbatch, nt),
        in_specs=specs, out_specs=out_specs, out_shape=outs, input_output_aliases=aliases,
        scratch_shapes=[pltpu.VMEM((1, CONV_PAD + tm, dc), F32)],
        compiler_params=_cparams(2), name="mixer_in_prompt",
    )(*ins)


def _mixer_in_sample(x, mod, w, w_in_t, layer, tabs, prev, nb, t):
    m, d = x.shape
    dc = w["conv_w"].shape[-1]
    steps = 4
    assert w_in_t.shape[1] == Q_LORA + KV_LORA + ROPE_DIM + 3 * dc
    const = lambda a: pl.BlockSpec(a.shape, lambda s: (0,) * a.ndim)
    ins = [x, mod.arr, mod.arr, w_in_t, w["q_norm"], w["w_uq"], w["kv_norm"],
           *tabs, w["w_uk_t"], w["conv_w"], w["conv_out_norm"], prev]
    specs = [const(x), mod.spec(0), mod.spec(1), pl.BlockSpec(memory_space=pl.ANY),
             _layer_spec(w["q_norm"], layer), _layer_spec(w["w_uq"], layer),
             _layer_spec(w["kv_norm"], layer), const(tabs[0]), const(tabs[1]), const(tabs[2]),
             _layer_spec(w["w_uk_t"], layer), _layer_spec(w["conv_w"], layer),
             _layer_spec(w["conv_out_norm"], layer),
             pl.BlockSpec((None, nb, CONV_W - 1, dc), lambda s: (layer, 0, 0, 0))]
    outs = [jax.ShapeDtypeStruct((nb, N_HEADS, t, KV_LORA), BF16),
            jax.ShapeDtypeStruct((nb, N_HEADS, t, ROPE_W), BF16),
            jax.ShapeDtypeStruct((m, KV_LORA), F32),
            jax.ShapeDtypeStruct((m, ROPE_DIM), F32),
            jax.ShapeDtypeStruct((m, dc), BF16),
            jax.ShapeDtypeStruct((nb, CONV_W - 1, dc), F32),
            jax.ShapeDtypeStruct((d, IN_A_W), BF16),
            jax.ShapeDtypeStruct((d, 3 * dc), BF16)]
    out_specs = [const(o) for o in outs[:7]]
    out_specs.append(pl.BlockSpec((d, dc), lambda s: (0, jnp.maximum(s - 1, 0))))
    return pl.pallas_call(
        functools.partial(_mixer_in_sample_kernel, nb=nb, t=t, layer=layer), grid=(steps,),
        in_specs=specs, out_specs=out_specs, out_shape=outs,
        scratch_shapes=[pltpu.VMEM((2, IN_A_W, d), F32), pltpu.SemaphoreType.DMA((2,)),
                        pltpu.VMEM((m, d), BF16), pltpu.VMEM((3, m, dc), F32),
                        pltpu.VMEM((nb, CONV_PAD + t, dc), F32)],
        compiler_params=_cparams(1), name="mixer_in_sample",
    )(*ins)


def _prompt_attn_kernel(q_ref, k_ref, v_ref, o_ref, s_buf, p_buf, *, seq, tq, heads):
    row_chunk = lax.broadcasted_iota(jnp.int32, (tq, tq), 0) // CHUNK
    col_chunk = lax.broadcasted_iota(jnp.int32, (tq, tq), 1) // CHUNK
    visible = col_chunk <= row_chunk
    tiles = [(hd, qi) for hd in range(heads) for qi in range(seq // tq)]

    def scores(i):
        hd, qi = tiles[i]
        hi = (qi + 1) * tq
        cols = slice(hd * HEAD_W, (hd + 1) * HEAD_W)
        s_buf[i % 2, :, 0:hi] = _dot_t(q_ref[qi * tq:hi, cols], k_ref[0:hi, cols])

    scores(0)
    for i, (hd, qi) in enumerate(tiles):
        lo, hi = qi * tq, (qi + 1) * tq
        if i + 1 < len(tiles):
            scores(i + 1)
        s_d = jnp.where(visible, s_buf[i % 2, :, lo:hi], NEG_BIG)
        m = jnp.max(s_d, axis=-1, keepdims=True)
        if qi > 0:
            m = jnp.maximum(m, jnp.max(s_buf[i % 2, :, 0:lo], axis=-1, keepdims=True))
            p_buf[:, 0:lo] = jnp.exp2(s_buf[i % 2, :, 0:lo] - m).astype(BF16)
        p_buf[:, lo:hi] = jnp.exp2(s_d - m).astype(BF16)
        oe = _dot(p_buf[:, 0:hi], v_ref[0:hi, hd * HEAD_W:(hd + 1) * HEAD_W])
        o_ref[lo:hi, hd * V_DIM:(hd + 1) * V_DIM] = oe[:, :V_DIM] / oe[:, V_DIM:V_DIM + 1]


def _prompt_attention(q, k, v, nbatch, seq, tq=256, heads=2):
    m = q.shape[0]
    qkv_spec = pl.BlockSpec((seq, heads * HEAD_W), lambda b, h: (b, h))
    return pl.pallas_call(
        functools.partial(_prompt_attn_kernel, seq=seq, tq=tq, heads=heads),
        grid=(nbatch, N_HEADS // heads),
        in_specs=[qkv_spec, qkv_spec, qkv_spec],
        out_specs=pl.BlockSpec((seq, heads * V_DIM), lambda b, h: (b, h)),
        out_shape=jax.ShapeDtypeStruct((m, D_ATTN), F32),
        scratch_shapes=[pltpu.VMEM((2, tq, seq), F32), pltpu.VMEM((tq, seq), BF16)],
        compiler_params=_cparams(2), name="prompt_attention",
    )(q, k, v)


def _sample_attn_kernel(qa_ref, qpe_ref, pc_ref, pk_ref, nc_ref, nk_ref, wuv_ref, o_ref, *, t):
    rows = N_HEADS * t
    qa = qa_ref[0].reshape(rows, KV_LORA)
    qp = qpe_ref[0].reshape(rows, ROPE_W)[:, :ROPE_DIM]
    pc = pc_ref[0].astype(BF16)
    pk_t = pk_ref[0].astype(BF16)
    nc = nc_ref[...].astype(BF16)
    nk = nk_ref[...].astype(BF16)
    s_p = _dot_t(qa, pc) + _dot(qp, pk_t)
    s_n = _dot_t(qa, nc) + _dot_t(qp, nk)
    m = jnp.maximum(jnp.max(s_p, axis=-1, keepdims=True), jnp.max(s_n, axis=-1, keepdims=True))
    p_p = jnp.exp2(s_p - m)
    p_n = jnp.exp2(s_n - m)
    l = jnp.sum(p_p, axis=-1, keepdims=True) + jnp.sum(p_n, axis=-1, keepdims=True)
    o_lat = (_dot(p_p.astype(BF16), pc) + _dot(p_n.astype(BF16), nc)) / l
    o_lat = o_lat.astype(BF16)
    for hd in range(N_HEADS):
        o_ref[:, hd * V_DIM:(hd + 1) * V_DIM] = _dot(o_lat[hd * t:(hd + 1) * t, :], wuv_ref[hd])


def _sample_attention(qa, qpe, past_ckv, past_kpe_t, new_ckv, new_kpe, w_uv_h, layer):
    nb, _, t, _ = qa.shape
    past = past_ckv.shape[2]
    return pl.pallas_call(
        functools.partial(_sample_attn_kernel, t=t),
        grid=(nb,),
        in_specs=[pl.BlockSpec((1, N_HEADS, t, KV_LORA), lambda b: (b, 0, 0, 0)),
                  pl.BlockSpec((1, N_HEADS, t, ROPE_W), lambda b: (b, 0, 0, 0)),
                  pl.BlockSpec((None, 1, past, KV_LORA), lambda b: (layer, b, 0, 0)),
                  pl.BlockSpec((None, 1, ROPE_DIM, past), lambda b: (layer, b, 0, 0)),
                  pl.BlockSpec((t, KV_LORA), lambda b: (b, 0)),
                  pl.BlockSpec((t, ROPE_DIM), lambda b: (b, 0)),
                  _layer_spec(w_uv_h, layer)],
        out_specs=pl.BlockSpec((t, D_ATTN), lambda b: (b, 0)),
        out_shape=jax.ShapeDtypeStruct((nb * t, D_ATTN), F32),
        compiler_params=_cparams(1), name="sample_attention",
    )(qa, qpe, past_ckv, past_kpe_t, new_ckv, new_kpe, w_uv_h)


def _out_proj_kernel(x_ref, attn_ref, conv_ref, g_ref, shm_ref, scm_ref, an_ref, woa_ref, woc_ref,
                     o_ref, h_ref, *bf16_w_out, groups):
    if bf16_w_out:
        wo_ref, = bf16_w_out
        da = woa_ref.shape[0]
        wo_ref[0:da, :] = woa_ref[...].astype(BF16)
        wo_ref[da:, :] = woc_ref[...].astype(BF16)
        woa, woc = wo_ref[0:da, :], wo_ref[da:, :]
    else:
        woa, woc = woa_ref[...], woc_ref[...]
    starts = [sum(groups[:j]) for j in range(len(groups))]
    for r0, rows in zip(starts, groups):
        rs = slice(r0, r0 + rows) if len(groups) > 1 else slice(None)
        a = (_rms(attn_ref[rs, :]) * an_ref[...]).astype(BF16)
        mix = _dot(a, woa) + _dot(conv_ref[rs, :], woc)
        y = x_ref[rs, :] + g_ref[...] * mix
        o_ref[rs, :] = y
        h_ref[rs, :] = (_rms(y) * (1.0 + scm_ref[...]) + shm_ref[...]).astype(BF16)


def _out_proj(x, attn, conv_n, mod, w, w_out, layer, nbatch, seq, tm, groups=None):
    groups = (tm,) if groups is None else tuple(groups)
    assert sum(groups) == tm
    m, d = x.shape
    nt = seq // tm
    da, dc = attn.shape[1], conv_n.shape[1]
    ins = [x, attn, conv_n, mod.arr, mod.arr, mod.arr, w["attn_out_norm"], w_out, w_out]
    specs = [_rows(tm, d, nt), _rows(tm, da, nt), _rows(tm, dc, nt),
             mod.spec(2), mod.spec(3), mod.spec(4),
             _layer_spec(w["attn_out_norm"], mod.layer)]
    out_specs = [_rows(tm, d, nt), _rows(tm, d, nt)]
    out_shape = [jax.ShapeDtypeStruct((m, d), F32), jax.ShapeDtypeStruct((m, d), BF16)]
    if layer is None:
        specs += [pl.BlockSpec((da, d), lambda b, s: (0, 0), pipeline_mode=pl.Buffered(1)),
                  pl.BlockSpec((dc, d), lambda b, s: (da // dc, 0), pipeline_mode=pl.Buffered(1))]
    else:
        assert nbatch * nt == 1
        specs += [_layer_spec(w_out, layer, (da, d), (0, 0)),
                  _layer_spec(w_out, layer, (dc, d), (da // dc, 0))]
        out_specs.append(pl.BlockSpec((da + dc, d), lambda b, s: (0, 0)))
        out_shape.append(jax.ShapeDtypeStruct((da + dc, d), BF16))
    return pl.pallas_call(
        functools.partial(_out_proj_kernel, groups=groups), grid=(nbatch, nt), in_specs=specs,
        out_specs=out_specs, out_shape=out_shape,
        compiler_params=_cparams(2), name="out_proj",
    )(*ins)


def _mlp_kernel(x_ref, h_ref, g_ref, wup_ref, wdn_ref, fn_ref, o_ref, *extra, nf, final, cast,
                x_rows):
    f = pl.program_id(2)
    if cast:
        wu_ref, wd_ref = extra
        wu_ref[...] = wup_ref[...].astype(BF16)
        wd_ref[...] = wdn_ref[...].astype(BF16)
    else:
        wu_ref, wd_ref = wup_ref, wdn_ref
    if x_rows:
        x_buf, x_sem = extra
        tile = pl.program_id(0) * pl.num_programs(1) + pl.program_id(1)
        row0 = pl.multiple_of(tile * x_rows, x_rows)
        x_copy = pltpu.make_async_copy(x_ref.at[pl.ds(row0, x_rows), :], x_buf, x_sem)

        @pl.when(f == 0)
        def _():
            x_copy.start()

    def partial_down():
        up = _dot(h_ref[...], wu_ref[...])
        act = jnp.square(jnp.maximum(up, 0.0)).astype(BF16)
        return _dot(act, wd_ref[...])

    @pl.when(f == 0)
    def _():
        o_ref[...] = partial_down()

    @pl.when(f > 0)
    def _():
        o_ref[...] += partial_down()

    @pl.when(f == nf - 1)
    def _():
        if x_rows:
            x_copy.wait()
            x = x_buf[...]
        else:
            x = x_ref[...]
        y = x + g_ref[...] * o_ref[...]
        if final:
            y = _rms(y) * fn_ref[...]
        o_ref[...] = y


def _mlp(x, h, mod, w_up, w_down, layer, final_norm, nbatch, seq, tm, tf, final):
    m, d = x.shape
    nt = seq // tm
    dff = w_up.shape[-1]
    nf = dff // tf
    cast = layer is not None
    out_specs = [_rows(tm, d, nt)]
    out_shape = [jax.ShapeDtypeStruct((m, d), F32)]
    scratch = []
    if cast:
        assert nbatch * nt == 1
        x_spec = _rows(tm, d, nt)
        w_specs = [pl.BlockSpec((None, d, tf), lambda b, s, f: (layer, 0, f)),
                   pl.BlockSpec((None, tf, d), lambda b, s, f: (layer, f, 0))]
        out_specs += [pl.BlockSpec((d, tf), lambda b, s, f: (0, f)),
                      pl.BlockSpec((tf, d), lambda b, s, f: (f, 0))]
        out_shape += [jax.ShapeDtypeStruct((d, dff), BF16), jax.ShapeDtypeStruct((dff, d), BF16)]
    else:
        x_spec = pl.BlockSpec(memory_space=pl.ANY)
        w_specs = [pl.BlockSpec((d, tf), lambda b, s, f: (0, f)),
                   pl.BlockSpec((tf, d), lambda b, s, f: (f, 0))]
        scratch = [pltpu.VMEM((tm, d), F32), pltpu.SemaphoreType.DMA(())]
    return pl.pallas_call(
        functools.partial(_mlp_kernel, nf=nf, final=final, cast=cast,
                          x_rows=0 if cast else tm),
        grid=(nbatch, nt, nf),
        in_specs=[x_spec, _rows(tm, d, nt), mod.spec(5), *w_specs,
                  pl.BlockSpec(final_norm.shape, lambda b, s, f: (0, 0))],
        out_specs=out_specs, out_shape=out_shape, scratch_shapes=scratch,
        compiler_params=_cparams(3), name="mlp",
    )(x, h, mod.arr, w_up, w_down, final_norm)


def _rope_tables(pos):
    half = ROPE_DIM // 2
    inv = ROPE_BASE ** (-jnp.arange(half, dtype=F32) / half)
    ang = pos.astype(F32)[:, None] * inv[None, :]
    cos, sin, zero = jnp.cos(ang), jnp.sin(ang), jnp.zeros_like(ang)
    return (jnp.concatenate([cos, cos, zero, zero], axis=-1),
            jnp.concatenate([-sin, zero, zero, zero], axis=-1),
            jnp.concatenate([zero, sin, zero, zero], axis=-1))


def _prep_weights(q_norm, w_uq, kv_norm, w_ukv, conv_w, attn_out_norm, conv_out_norm,
                  w_out, w_up, w_down):
    depth = w_uq.shape[0]
    wq = w_uq.astype(BF16).reshape(depth, Q_LORA, N_HEADS, QK_DIM)
    wq = jnp.pad(wq, ((0, 0), (0, 0), (0, 0), (0, HEAD_W - QK_DIM)))
    wkv = w_ukv.astype(BF16).reshape(depth, KV_LORA, N_HEADS, NOPE_DIM + V_DIM)
    w_uk = wkv[..., :NOPE_DIM]
    w_uv = wkv[..., NOPE_DIM:]
    return dict(
        w_uq=wq.reshape(depth, Q_LORA, N_HEADS * HEAD_W),
        w_uk=w_uk.reshape(depth, KV_LORA, N_HEADS * NOPE_DIM),
        w_uv=w_uv.reshape(depth, KV_LORA, N_HEADS * V_DIM),
        w_uk_t=jnp.transpose(w_uk, (0, 2, 3, 1)),
        w_uv_h=jnp.transpose(w_uv, (0, 2, 1, 3)),
        q_norm=q_norm[:, None, :], kv_norm=kv_norm[:, None, :],
        conv_w=conv_w, attn_out_norm=attn_out_norm[:, None, :],
        conv_out_norm=conv_out_norm[:, None, :],
        w_out=w_out, w_up=w_up, w_down=w_down)


def kernel(x_prompt, x_sample, c_prompt, c_sample, cache_kv_latent, cache_k_rope, state_conv,
           w_ada, b_ada, w_in, q_norm, w_uq, kv_norm, w_ukv, conv_w,
           attn_out_norm, conv_out_norm, w_out, w_up, w_down, final_norm):
    bp, sp, d = x_prompt.shape
    bs, ss, _ = x_sample.shape
    depth = w_ada.shape[0]
    past = cache_kv_latent.shape[2]
    dc = conv_w.shape[-1]
    ms = bs * ss

    w = _prep_weights(q_norm, w_uq, kv_norm, w_ukv, conv_w, attn_out_norm,
                      conv_out_norm, w_out, w_up, w_down)
    w_in_t = jnp.swapaxes(w_in, 1, 2)
    fnorm = final_norm[None, :]
    tabs_p = _rope_tables(jnp.arange(sp))
    tabs_s = tuple(jnp.tile(tb, (bs, 1)) for tb in _rope_tables(past + jnp.arange(ss)))

    mod = _modulation(jnp.concatenate([c_prompt, c_sample], axis=0), w_ada, b_ada)
    mod = mod.reshape(depth, bp + bs, N_MOD, d)
    mod_p = mod[:, :bp].reshape(depth * bp * N_MOD, 1, d)
    mod_s = jnp.repeat(jnp.swapaxes(mod[:, bp:], 1, 2), ss, axis=2).reshape(depth * N_MOD, ms, d)
    zero_conv = jnp.zeros((1, bp, CONV_W - 1, dc), F32)

    xp = x_prompt.reshape(bp * sp, d)
    xs = x_sample.reshape(ms, d)
    past_kpe_t = jnp.swapaxes(cache_k_rope, 2, 3)
    stacks = None
    outs = [[] for _ in range(4)]
    for l in range(depth):
        last = l == depth - 1
        mp = _Mod(mod_p, l, bp)
        msm = _Mod(mod_s, l, 0)

        qa, qpe, ckv_s, kpe_s, conv_ns, cstate_s, w_a_l, w_b_l = _mixer_in_sample(
            xs, msm, w, w_in_t, l, tabs_s, state_conv, bs, ss)
        attn_s = _sample_attention(qa, qpe, cache_kv_latent, past_kpe_t, ckv_s, kpe_s,
                                   w["w_uv_h"], l)
        xs, hs, w_out_l = _out_proj(xs, attn_s, conv_ns, msm, w, w["w_out"], l, 1, ms, tm=ms)
        xs, w_up_l, w_down_l = _mlp(xs, hs, msm, w["w_up"], w["w_down"], l, fnorm, 1, ms,
                                    tm=ms, tf=512, final=last)
        outs[1].append(ckv_s.reshape(bs, ss, KV_LORA))
        outs[2].append(kpe_s.reshape(bs, ss, ROPE_DIM))
        outs[3].append(cstate_s)

        q, k, v, *stacks, conv_n, cstate = _mixer_in_prompt(
            xp, mp, w, w_a_l, w_b_l, l, tabs_p, zero_conv, bp, sp, stacks, tm=256, nsub=1)
        attn = _prompt_attention(q, k, v, bp, sp)
        xp, hp = _out_proj(xp, attn, conv_n, mp, w, w_out_l, None, bp, sp, tm=512, groups=(384, 128))
        xp, = _mlp(xp, hp, mp, w_up_l, w_down_l, None, fnorm, bp, sp, tm=1024, tf=1024, final=last)
        outs[0].append(cstate)

    return (xp.reshape(bp, sp, d), xs.reshape(bs, ss, d),
            stacks[0].reshape(depth, bp, sp, KV_LORA), jnp.swapaxes(stacks[1], 2, 3),
            jnp.stack(outs[0]), jnp.stack(outs[1]), jnp.stack(outs[2]), jnp.stack(outs[3]))
```

```python
import functools
import math

import jax
import jax.numpy as jnp
from jax import lax
from jax.experimental import pallas as pl
from jax.experimental.pallas import tpu as pltpu

F32 = jnp.float32
BF16 = jnp.bfloat16

CHUNK = 64
N_HEADS = 8
NOPE_DIM = 128
ROPE_DIM = 64
V_DIM = 128
QK_DIM = NOPE_DIM + ROPE_DIM
Q_LORA = 512
KV_LORA = 512
D_ATTN = N_HEADS * V_DIM
CONV_W = 3
N_MOD = 6
ROPE_BASE = 10000.0
EPS = 1e-6
ATTN_SCALE = QK_DIM ** -0.5

LANES = 128
HEAD_W = 2 * LANES
ROPE_W = LANES
IN_A_W = Q_LORA + KV_LORA + ROPE_W
CONV_PAD = 8
Q_SCALE = ATTN_SCALE * math.log2(math.e)
NEG_BIG = -1e30
VMEM_LIMIT = 60000 * 1024


def _cparams(n_axes):
    return pltpu.CompilerParams(dimension_semantics=("arbitrary",) * n_axes,
                                vmem_limit_bytes=VMEM_LIMIT)


def _rms(x):
    return x * lax.rsqrt(jnp.mean(x * x, axis=-1, keepdims=True) + EPS)


def _dot(a, b):
    return jnp.dot(a, b, preferred_element_type=F32)


def _dot_t(a, b):
    return lax.dot_general(a, b, (((1,), (1,)), ((), ())), preferred_element_type=F32)


def _layer_spec(arr, layer, block=None, index=None):
    block = tuple(arr.shape[1:]) if block is None else tuple(block)
    index = (0,) * len(block) if index is None else tuple(index)
    return pl.BlockSpec((None,) + block, lambda *g: (layer,) + index,
                        pipeline_mode=pl.Buffered(1))


def _rows(tm, cols, tiles_per_batch):
    return pl.BlockSpec((tm, cols), lambda b, s, *_: (b * tiles_per_batch + s, 0))


class _Mod:
    def __init__(self, arr, layer, per_batch):
        self.arr, self.layer, self.per_batch = arr, layer, per_batch

    def spec(self, role):
        layer, nb = self.layer, self.per_batch
        if nb:
            return pl.BlockSpec((None,) + self.arr.shape[1:],
                                lambda b, *_: ((layer * nb + b) * N_MOD + role, 0, 0))
        return pl.BlockSpec((None,) + self.arr.shape[1:],
                            lambda *_: (layer * N_MOD + role, 0, 0))


def _mod_kernel(c_ref, w_ref, b_ref, o_ref):
    c = c_ref[...]
    s = (c * jax.nn.sigmoid(c)).astype(BF16)
    o_ref[0] = _dot(s, w_ref[0].astype(BF16)) + b_ref[0]


def _modulation(c_all, w_ada, b_ada, tn=1024):
    depth, d, n = w_ada.shape
    r = c_all.shape[0]
    return pl.pallas_call(
        _mod_kernel,
        grid=(depth, n // tn),
        in_specs=[pl.BlockSpec((r, d), lambda l, j: (0, 0)),
                  pl.BlockSpec((1, d, tn), lambda l, j: (l, 0, j)),
                  pl.BlockSpec((1, 1, tn), lambda l, j: (l, 0, j))],
        out_specs=pl.BlockSpec((1, r, tn), lambda l, j: (l, 0, j)),
        out_shape=jax.ShapeDtypeStruct((depth, r, n), F32),
        compiler_params=_cparams(2),
        name="modulation",
    )(c_all, w_ada, b_ada.reshape(depth, 1, n))


def _rope128(p, cos, s1, s2):
    return p * cos + pltpu.roll(p, 96, 1) * s1 + pltpu.roll(p, 32, 1) * s2


def _mod_rows(m_ref, rows):
    m = m_ref[...]
    nb, d = m.shape
    if nb in (1, rows):
        return m
    return jnp.broadcast_to(m[:, None, :], (nb, rows // nb, d)).reshape(rows, d)


def _adaln(x_ref, rs, sh_ref, sc_ref):
    x = x_ref[rs, :]
    n = x.shape[0]
    return (_rms(x) * (1.0 + _mod_rows(sc_ref, n)) + _mod_rows(sh_ref, n)).astype(BF16)


def _attn_in_common(h, rs, win_ref, qn_ref, wuq_ref, kvn_ref, cos_ref, s1_ref, s2_ref):
    z = _dot(h, win_ref[...])
    cos, s1, s2 = cos_ref[rs, :], s1_ref[rs, :], s2_ref[rs, :]
    qn = (_rms(z[:, :Q_LORA]) * qn_ref[...]).astype(BF16)
    q = _dot(qn, wuq_ref[...]) * Q_SCALE
    ckv = _rms(z[:, Q_LORA:Q_LORA + KV_LORA]) * kvn_ref[...]
    kp = _rope128(z[:, Q_LORA + KV_LORA:], cos, s1, s2)
    rope = functools.partial(_rope128, cos=cos, s1=s1, s2=s2)
    return q, ckv, kp, rope


def _conv_group(h, p0, tg, nb, w_ref, cw_ref, cn_ref, buf):
    dc = cw_ref.shape[-1]
    z = _dot(h, w_ref[...])
    return _conv_post(z[:, :dc], z[:, dc:2 * dc], z[:, 2 * dc:], p0, tg, nb, cw_ref, cn_ref, buf)


def _conv_post(gate_b, gate_c, u, p0, tg, nb, cw_ref, cn_ref, buf):
    dc = cw_ref.shape[-1]
    cu = gate_c * u
    buf[:, CONV_PAD + p0:CONV_PAD + p0 + tg, :] = cu.reshape(nb, tg, dc)
    base = CONV_PAD - (CONV_W - 1) + p0
    y = cw_ref[0:1, :].reshape(1, 1, dc) * buf[:, base:base + tg, :]
    for k in range(1, CONV_W):
        y = y + cw_ref[k:k + 1, :].reshape(1, 1, dc) * buf[:, base + k:base + k + tg, :]
    conv_o = gate_b * y.reshape(nb * tg, dc)
    return (_rms(conv_o) * cn_ref[...]).astype(BF16)


def _conv_carry(buf, st_ref, t):
    last = buf[:, CONV_PAD + t - 2:CONV_PAD + t, :]
    st_ref[...] = last
    buf[:, CONV_PAD - 2:CONV_PAD, :] = last


def _mixer_in_prompt_kernel(x_ref, sh_ref, sc_ref, win_ref, qn_ref, wuq_ref, kvn_ref,
                            cos_ref, s1_ref, s2_ref, wuk_ref, wuv_ref,
                            winb_ref, cw_ref, cn_ref, prev_ref, *rest, nsub, first):
    q_out, k_out, v_out, ckv_out, kpe_out, conv_out, st_ref, buf = rest[-8:]

    @pl.when(pl.program_id(1) == 0)
    def _():
        buf[:, CONV_PAD - 2:CONV_PAD, :] = prev_ref[...]

    if first:
        ckv_out[1:] = jnp.zeros((ckv_out.shape[0] - 1,) + ckv_out.shape[1:], F32)
        kpe_out[1:] = jnp.zeros((kpe_out.shape[0] - 1,) + kpe_out.shape[1:], F32)
        ckv_out, kpe_out = ckv_out.at[0], kpe_out.at[0]
    rows = x_ref.shape[0] // nsub
    lane = lax.broadcasted_iota(jnp.int32, (rows, LANES), 1)
    ones_col = jnp.where(lane == 0, 1.0, 0.0).astype(BF16)
    for j in range(nsub):
        rs = slice(j * rows, (j + 1) * rows)
        h = _adaln(x_ref, rs, sh_ref, sc_ref)
        conv_out[rs, :] = _conv_group(h, j * rows, rows, 1, winb_ref, cw_ref, cn_ref, buf)
        q, ckv, kp, rope = _attn_in_common(h, rs, win_ref, qn_ref, wuq_ref, kvn_ref,
                                           cos_ref, s1_ref, s2_ref)
        ckv_out[rs, :] = ckv
        kpe_out[:, rs] = kp.T[:ROPE_DIM, :]
        ckb = ckv.astype(BF16)
        kn = _dot(ckb, wuk_ref[...])
        v = _dot(ckb, wuv_ref[...]).astype(BF16)
        kpb = kp.astype(BF16)
        for hd in range(N_HEADS):
            c0 = hd * HEAD_W
            v_out[rs, c0:c0 + V_DIM] = v[:, hd * V_DIM:(hd + 1) * V_DIM]
            v_out[rs, c0 + V_DIM:c0 + HEAD_W] = ones_col
            q_out[rs, c0:c0 + NOPE_DIM] = q[:, c0:c0 + NOPE_DIM].astype(BF16)
            q_out[rs, c0 + NOPE_DIM:c0 + HEAD_W] = rope(q[:, c0 + NOPE_DIM:c0 + HEAD_W]).astype(BF16)
            k_out[rs, c0:c0 + NOPE_DIM] = kn[:, hd * NOPE_DIM:(hd + 1) * NOPE_DIM].astype(BF16)
            k_out[rs, c0 + NOPE_DIM:c0 + HEAD_W] = kpb
    _conv_carry(buf, st_ref, x_ref.shape[0])


def _mixer_in_sample_kernel(x_ref, sh_ref, sc_ref, wint_hbm, qn_ref, wuq_ref, kvn_ref,
                            cos_ref, s1_ref, s2_ref, wukv_ref, cw_ref, cn_ref, prev_ref,
                            qa_out, qpe_out, ckv_out, kpe_out, conv_out, st_ref, wa_out, wb_out,
                            wuk_out, wuv_out, wukt_out, wuvh_out,
                            w_buf, w_sem, h_scr, z_scr, buf, *, nb, t, layer):
    s = pl.program_id(0)
    dc = cw_ref.shape[-1]
    n_a = Q_LORA + KV_LORA + ROPE_DIM
    chunk_rows = [(0, IN_A_W)] + [(n_a + j * dc, dc) for j in range(3)]

    def chunk_copy(k):
        r0, rows = chunk_rows[k]
        return pltpu.make_async_copy(wint_hbm.at[layer, pl.ds(r0, rows), :],
                                     w_buf.at[k % 2, pl.ds(0, rows), :], w_sem.at[k % 2])

    for k in range(len(chunk_rows)):
        @pl.when(s == k)
        def _(k=k):
            if k == 0:
                chunk_copy(0).start()
                h_scr[...] = _adaln(x_ref, slice(None), sh_ref, sc_ref)
                buf[:, CONV_PAD - 2:CONV_PAD, :] = prev_ref[...]
            if k + 1 < len(chunk_rows):
                chunk_copy(k + 1).start()
            chunk_copy(k).wait()
            rows = chunk_rows[k][1]
            w_t = w_buf[k % 2, 0:rows, :].T.astype(BF16)
            if k == 0:
                wa_out[...] = w_t
                q, ckv, kp, rope = _attn_in_common(h_scr[...], slice(None), wa_out, qn_ref,
                                                   wuq_ref, kvn_ref, cos_ref, s1_ref, s2_ref)
                ckv_out[...] = ckv
                kpe_out[...] = kp[:, :ROPE_DIM]
                for hd in range(N_HEADS):
                    k0 = hd * (NOPE_DIM + V_DIM)
                    w_k = wukv_ref[:, k0:k0 + NOPE_DIM]
                    w_v = wukv_ref[:, k0 + NOPE_DIM:k0 + NOPE_DIM + V_DIM].astype(BF16)
                    wuk_out[:, hd * NOPE_DIM:(hd + 1) * NOPE_DIM] = w_k.astype(BF16)
                    wukt_out[hd] = w_k.T.astype(BF16)
                    wuv_out[:, hd * V_DIM:(hd + 1) * V_DIM] = w_v
                    wuvh_out[hd] = w_v
                for hd in range(N_HEADS):
                    c0 = hd * HEAD_W
                    qa = _dot(q[:, c0:c0 + NOPE_DIM].astype(BF16), wukt_out[hd])
                    qa_out[:, hd] = qa.reshape(nb, t, KV_LORA).astype(BF16)
                    qp = rope(q[:, c0 + NOPE_DIM:c0 + HEAD_W])
                    qpe_out[:, hd] = qp.reshape(nb, t, ROPE_W).astype(BF16)
            else:
                wb_out[...] = w_t
                z_scr[k - 1] = _dot(h_scr[...], wb_out[...])
            if k == len(chunk_rows) - 1:
                conv_out[...] = _conv_post(z_scr[0], z_scr[1], z_scr[2], 0, t, nb,
                                           cw_ref, cn_ref, buf)
                _conv_carry(buf, st_ref, t)


def _mixer_in_prompt(x, mod, w, w_a, w_b, w_uk, w_uv, layer, tabs, prev, nbatch, seq, stacks,
                     tm, nsub):
    m, d = x.shape
    depth = w["q_norm"].shape[0]
    dc = w["conv_w"].shape[-1]
    nt = seq // tm
    tab_spec = pl.BlockSpec((tm, ROPE_W), lambda b, s: (s, 0))
    whole = lambda a: pl.BlockSpec(a.shape, lambda b, s: (0,) * a.ndim,
                                   pipeline_mode=pl.Buffered(1))
    ins = [x, mod.arr, mod.arr, w_a, w["q_norm"], w["w_uq"], w["kv_norm"],
           *tabs, w_uk, w_uv, w_b, w["conv_w"], w["conv_out_norm"], prev]
    specs = [_rows(tm, d, nt), mod.spec(0), mod.spec(1), whole(w_a),
             _layer_spec(w["q_norm"], layer), _layer_spec(w["w_uq"], layer),
             _layer_spec(w["kv_norm"], layer), tab_spec, tab_spec, tab_spec,
             whole(w_uk), whole(w_uv),
             whole(w_b), _layer_spec(w["conv_w"], layer),
             _layer_spec(w["conv_out_norm"], layer),
             pl.BlockSpec((None, 1, CONV_W - 1, dc), lambda b, s: (0, b, 0, 0))]
    aliases = {}
    if stacks is not None:
        aliases = {len(ins): 3, len(ins) + 1: 4}
        ins += list(stacks)
        specs += [pl.BlockSpec(memory_space=pl.ANY)] * 2
    outs = [jax.ShapeDtypeStruct((m, N_HEADS * HEAD_W), BF16),
            jax.ShapeDtypeStruct((m, N_HEADS * HEAD_W), BF16),
            jax.ShapeDtypeStruct((m, N_HEADS * HEAD_W), BF16),
            jax.ShapeDtypeStruct((depth, m, KV_LORA), F32),
            jax.ShapeDtypeStruct((depth, nbatch, ROPE_DIM, seq), F32),
            jax.ShapeDtypeStruct((m, dc), BF16),
            jax.ShapeDtypeStruct((nbatch, CONV_W - 1, dc), F32)]
    first = stacks is None
    lsel = (depth, 0) if first else (None, layer)
    out_specs = [_rows(tm, N_HEADS * HEAD_W, nt), _rows(tm, N_HEADS * HEAD_W, nt),
                 _rows(tm, N_HEADS * HEAD_W, nt),
                 pl.BlockSpec((lsel[0], tm, KV_LORA), lambda b, s: (lsel[1], b * nt + s, 0)),
                 pl.BlockSpec((lsel[0], None, ROPE_DIM, tm), lambda b, s: (lsel[1], b, 0, s)),
                 _rows(tm, dc, nt),
                 pl.BlockSpec((1, CONV_W - 1, dc), lambda b, s: (b, 0, 0))]
    return pl.pallas_call(
        functools.partial(_mixer_in_prompt_kernel, nsub=nsub, first=first), grid=(nbatch, nt),
        in_specs=specs, out_specs=out_specs, out_shape=outs, input_output_aliases=aliases,
        scratch_shapes=[pltpu.VMEM((1, CONV_PAD + tm, dc), F32)],
        compiler_params=_cparams(2), name="mixer_in_prompt",
    )(*ins)


def _mixer_in_sample(x, mod, w, w_in_t, layer, tabs, prev, nb, t):
    m, d = x.shape
    dc = w["conv_w"].shape[-1]
    steps = 4
    assert w_in_t.shape[1] == Q_LORA + KV_LORA + ROPE_DIM + 3 * dc
    const = lambda a: pl.BlockSpec(a.shape, lambda s: (0,) * a.ndim)
    ins = [x, mod.arr, mod.arr, w_in_t, w["q_norm"], w["w_uq"], w["kv_norm"],
           *tabs, w["w_ukv"], w["conv_w"], w["conv_out_norm"], prev]
    specs = [const(x), mod.spec(0), mod.spec(1), pl.BlockSpec(memory_space=pl.ANY),
             _layer_spec(w["q_norm"], layer), _layer_spec(w["w_uq"], layer),
             _layer_spec(w["kv_norm"], layer), const(tabs[0]), const(tabs[1]), const(tabs[2]),
             _layer_spec(w["w_ukv"], layer), _layer_spec(w["conv_w"], layer),
             _layer_spec(w["conv_out_norm"], layer),
             pl.BlockSpec((None, nb, CONV_W - 1, dc), lambda s: (layer, 0, 0, 0))]
    outs = [jax.ShapeDtypeStruct((nb, N_HEADS, t, KV_LORA), BF16),
            jax.ShapeDtypeStruct((nb, N_HEADS, t, ROPE_W), BF16),
            jax.ShapeDtypeStruct((m, KV_LORA), F32),
            jax.ShapeDtypeStruct((m, ROPE_DIM), F32),
            jax.ShapeDtypeStruct((m, dc), BF16),
            jax.ShapeDtypeStruct((nb, CONV_W - 1, dc), F32),
            jax.ShapeDtypeStruct((d, IN_A_W), BF16),
            jax.ShapeDtypeStruct((d, 3 * dc), BF16),
            jax.ShapeDtypeStruct((KV_LORA, N_HEADS * NOPE_DIM), BF16),
            jax.ShapeDtypeStruct((KV_LORA, N_HEADS * V_DIM), BF16),
            jax.ShapeDtypeStruct((N_HEADS, NOPE_DIM, KV_LORA), BF16),
            jax.ShapeDtypeStruct((N_HEADS, KV_LORA, V_DIM), BF16)]
    out_specs = [const(o) for o in outs[:7]]
    out_specs.append(pl.BlockSpec((d, dc), lambda s: (0, jnp.maximum(s - 1, 0))))
    out_specs += [const(o) for o in outs[8:]]
    return pl.pallas_call(
        functools.partial(_mixer_in_sample_kernel, nb=nb, t=t, layer=layer), grid=(steps,),
        in_specs=specs, out_specs=out_specs, out_shape=outs,
        scratch_shapes=[pltpu.VMEM((2, IN_A_W, d), F32), pltpu.SemaphoreType.DMA((2,)),
                        pltpu.VMEM((m, d), BF16), pltpu.VMEM((3, m, dc), F32),
                        pltpu.VMEM((nb, CONV_PAD + t, dc), F32)],
        compiler_params=_cparams(1), name="mixer_in_sample",
    )(*ins)


def _prompt_attn_kernel(q_ref, k_ref, v_ref, o_ref, s_buf, p_buf, *, seq, tq, heads):
    row_chunk = lax.broadcasted_iota(jnp.int32, (tq, tq), 0) // CHUNK
    col_chunk = lax.broadcasted_iota(jnp.int32, (tq, tq), 1) // CHUNK
    visible = col_chunk <= row_chunk
    tiles = [(hd, qi) for hd in range(heads) for qi in range(seq // tq)]

    def scores(i):
        hd, qi = tiles[i]
        hi = (qi + 1) * tq
        cols = slice(hd * HEAD_W, (hd + 1) * HEAD_W)
        s_buf[i % 2, :, 0:hi] = _dot_t(q_ref[qi * tq:hi, cols], k_ref[0:hi, cols])

    scores(0)
    for i, (hd, qi) in enumerate(tiles):
        lo, hi = qi * tq, (qi + 1) * tq
        if i + 1 < len(tiles):
            scores(i + 1)
        s_d = jnp.where(visible, s_buf[i % 2, :, lo:hi], NEG_BIG)
        m = jnp.max(s_d, axis=-1, keepdims=True)
        if qi > 0:
            m = jnp.maximum(m, jnp.max(s_buf[i % 2, :, 0:lo], axis=-1, keepdims=True))
            p_buf[:, 0:lo] = jnp.exp2(s_buf[i % 2, :, 0:lo] - m).astype(BF16)
        p_buf[:, lo:hi] = jnp.exp2(s_d - m).astype(BF16)
        oe = _dot(p_buf[:, 0:hi], v_ref[0:hi, hd * HEAD_W:(hd + 1) * HEAD_W])
        o_ref[lo:hi, hd * V_DIM:(hd + 1) * V_DIM] = oe[:, :V_DIM] / oe[:, V_DIM:V_DIM + 1]


def _prompt_attention(q, k, v, nbatch, seq, tq=256, heads=2):
    m = q.shape[0]
    qkv_spec = pl.BlockSpec((seq, heads * HEAD_W), lambda b, h: (b, h))
    return pl.pallas_call(
        functools.partial(_prompt_attn_kernel, seq=seq, tq=tq, heads=heads),
        grid=(nbatch, N_HEADS // heads),
        in_specs=[qkv_spec, qkv_spec, qkv_spec],
        out_specs=pl.BlockSpec((seq, heads * V_DIM), lambda b, h: (b, h)),
        out_shape=jax.ShapeDtypeStruct((m, D_ATTN), F32),
        scratch_shapes=[pltpu.VMEM((2, tq, seq), F32), pltpu.VMEM((tq, seq), BF16)],
        compiler_params=_cparams(2), name="prompt_attention",
    )(q, k, v)


def _sample_attn_kernel(qa_ref, qpe_ref, pc_ref, pk_ref, nc_ref, nk_ref, wuv_ref, o_ref, *, t):
    rows = N_HEADS * t
    qa = qa_ref[0].reshape(rows, KV_LORA)
    qp = qpe_ref[0].reshape(rows, ROPE_W)[:, :ROPE_DIM]
    pc = pc_ref[0].astype(BF16)
    pk_t = pk_ref[0].astype(BF16)
    nc = nc_ref[...].astype(BF16)
    nk = nk_ref[...].astype(BF16)
    s_p = _dot_t(qa, pc) + _dot(qp, pk_t)
    s_n = _dot_t(qa, nc) + _dot_t(qp, nk)
    m = jnp.maximum(jnp.max(s_p, axis=-1, keepdims=True), jnp.max(s_n, axis=-1, keepdims=True))
    p_p = jnp.exp2(s_p - m)
    p_n = jnp.exp2(s_n - m)
    l = jnp.sum(p_p, axis=-1, keepdims=True) + jnp.sum(p_n, axis=-1, keepdims=True)
    o_lat = (_dot(p_p.astype(BF16), pc) + _dot(p_n.astype(BF16), nc)) / l
    o_lat = o_lat.astype(BF16)
    for hd in range(N_HEADS):
        o_ref[:, hd * V_DIM:(hd + 1) * V_DIM] = _dot(o_lat[hd * t:(hd + 1) * t, :], wuv_ref[hd])


def _sample_attention(qa, qpe, past_ckv, past_kpe_t, new_ckv, new_kpe, w_uv_h, layer):
    nb, _, t, _ = qa.shape
    past = past_ckv.shape[2]
    return pl.pallas_call(
        functools.partial(_sample_attn_kernel, t=t),
        grid=(nb,),
        in_specs=[pl.BlockSpec((1, N_HEADS, t, KV_LORA), lambda b: (b, 0, 0, 0)),
                  pl.BlockSpec((1, N_HEADS, t, ROPE_W), lambda b: (b, 0, 0, 0)),
                  pl.BlockSpec((None, 1, past, KV_LORA), lambda b: (layer, b, 0, 0)),
                  pl.BlockSpec((None, 1, ROPE_DIM, past), lambda b: (layer, b, 0, 0)),
                  pl.BlockSpec((t, KV_LORA), lambda b: (b, 0)),
                  pl.BlockSpec((t, ROPE_DIM), lambda b: (b, 0)),
                  pl.BlockSpec(w_uv_h.shape, lambda b: (0, 0, 0))],
        out_specs=pl.BlockSpec((t, D_ATTN), lambda b: (b, 0)),
        out_shape=jax.ShapeDtypeStruct((nb * t, D_ATTN), F32),
        compiler_params=_cparams(1), name="sample_attention",
    )(qa, qpe, past_ckv, past_kpe_t, new_ckv, new_kpe, w_uv_h)


def _out_proj_kernel(x_ref, attn_ref, conv_ref, g_ref, shm_ref, scm_ref, an_ref, woa_ref, woc_ref,
                     o_ref, h_ref, *bf16_w_out, groups):
    if bf16_w_out:
        wo_ref, = bf16_w_out
        da = woa_ref.shape[0]
        wo_ref[0:da, :] = woa_ref[...].astype(BF16)
        wo_ref[da:, :] = woc_ref[...].astype(BF16)
        woa, woc = wo_ref[0:da, :], wo_ref[da:, :]
    else:
        woa, woc = woa_ref[...], woc_ref[...]
    starts = [sum(groups[:j]) for j in range(len(groups))]
    for r0, rows in zip(starts, groups):
        rs = slice(r0, r0 + rows) if len(groups) > 1 else slice(None)
        a = (_rms(attn_ref[rs, :]) * an_ref[...]).astype(BF16)
        mix = _dot(a, woa) + _dot(conv_ref[rs, :], woc)
        y = x_ref[rs, :] + _mod_rows(g_ref, rows) * mix
        o_ref[rs, :] = y
        h_ref[rs, :] = (_rms(y) * (1.0 + _mod_rows(scm_ref, rows))
                        + _mod_rows(shm_ref, rows)).astype(BF16)


def _out_proj(x, attn, conv_n, mod, w, w_out, layer, nbatch, seq, tm, groups=None):
    groups = (tm,) if groups is None else tuple(groups)
    assert sum(groups) == tm
    m, d = x.shape
    nt = seq // tm
    da, dc = attn.shape[1], conv_n.shape[1]
    ins = [x, attn, conv_n, mod.arr, mod.arr, mod.arr, w["attn_out_norm"], w_out, w_out]
    specs = [_rows(tm, d, nt), _rows(tm, da, nt), _rows(tm, dc, nt),
             mod.spec(2), mod.spec(3), mod.spec(4),
             _layer_spec(w["attn_out_norm"], mod.layer)]
    out_specs = [_rows(tm, d, nt), _rows(tm, d, nt)]
    out_shape = [jax.ShapeDtypeStruct((m, d), F32), jax.ShapeDtypeStruct((m, d), BF16)]
    if layer is None:
        specs += [pl.BlockSpec((da, d), lambda b, s: (0, 0), pipeline_mode=pl.Buffered(1)),
                  pl.BlockSpec((dc, d), lambda b, s: (da // dc, 0), pipeline_mode=pl.Buffered(1))]
    else:
        assert nbatch * nt == 1
        specs += [_layer_spec(w_out, layer, (da, d), (0, 0)),
                  _layer_spec(w_out, layer, (dc, d), (da // dc, 0))]
        out_specs.append(pl.BlockSpec((da + dc, d), lambda b, s: (0, 0)))
        out_shape.append(jax.ShapeDtypeStruct((da + dc, d), BF16))
    return pl.pallas_call(
        functools.partial(_out_proj_kernel, groups=groups), grid=(nbatch, nt), in_specs=specs,
        out_specs=out_specs, out_shape=out_shape,
        compiler_params=_cparams(2), name="out_proj",
    )(*ins)


def _mlp_kernel(x_ref, h_ref, g_ref, wup_ref, wdn_ref, fn_ref, o_ref, *extra, nf, final, cast,
                x_rows):
    f = pl.program_id(2)
    if cast:
        wu_ref, wd_ref = extra
        wu_ref[...] = wup_ref[...].astype(BF16)
        wd_ref[...] = wdn_ref[...].astype(BF16)
    else:
        wu_ref, wd_ref = wup_ref, wdn_ref
    if x_rows:
        x_buf, x_sem = extra
        tile = pl.program_id(0) * pl.num_programs(1) + pl.program_id(1)
        row0 = pl.multiple_of(tile * x_rows, x_rows)
        x_copy = pltpu.make_async_copy(x_ref.at[pl.ds(row0, x_rows), :], x_buf, x_sem)

        @pl.when(f == 0)
        def _():
            x_copy.start()

    def partial_down():
        up = _dot(h_ref[...], wu_ref[...])
        act = jnp.square(jnp.maximum(up, 0.0)).astype(BF16)
        return _dot(act, wd_ref[...])

    @pl.when(f == 0)
    def _():
        o_ref[...] = partial_down()

    @pl.when(f > 0)
    def _():
        o_ref[...] += partial_down()

    @pl.when(f == nf - 1)
    def _():
        if x_rows:
            x_copy.wait()
            x = x_buf[...]
        else:
            x = x_ref[...]
        y = x + _mod_rows(g_ref, x.shape[0]) * o_ref[...]
        if final:
            y = _rms(y) * fn_ref[...]
        o_ref[...] = y


def _mlp(x, h, mod, w_up, w_down, layer, final_norm, nbatch, seq, tm, tf, final):
    m, d = x.shape
    nt = seq // tm
    dff = w_up.shape[-1]
    nf = dff // tf
    cast = layer is not None
    out_specs = [_rows(tm, d, nt)]
    out_shape = [jax.ShapeDtypeStruct((m, d), F32)]
    scratch = []
    if cast:
        assert nbatch * nt == 1
        x_spec = _rows(tm, d, nt)
        w_specs = [pl.BlockSpec((None, d, tf), lambda b, s, f: (layer, 0, f)),
                   pl.BlockSpec((None, tf, d), lambda b, s, f: (layer, f, 0))]
        out_specs += [pl.BlockSpec((d, tf), lambda b, s, f: (0, f)),
                      pl.BlockSpec((tf, d), lambda b, s, f: (f, 0))]
        out_shape += [jax.ShapeDtypeStruct((d, dff), BF16), jax.ShapeDtypeStruct((dff, d), BF16)]
    else:
        x_spec = pl.BlockSpec(memory_space=pl.ANY)
        w_specs = [pl.BlockSpec((d, tf), lambda b, s, f: (0, f)),
                   pl.BlockSpec((tf, d), lambda b, s, f: (f, 0))]
        scratch = [pltpu.VMEM((tm, d), F32), pltpu.SemaphoreType.DMA(())]
    return pl.pallas_call(
        functools.partial(_mlp_kernel, nf=nf, final=final, cast=cast,
                          x_rows=0 if cast else tm),
        grid=(nbatch, nt, nf),
        in_specs=[x_spec, _rows(tm, d, nt), mod.spec(5), *w_specs,
                  pl.BlockSpec(final_norm.shape, lambda b, s, f: (0, 0))],
        out_specs=out_specs, out_shape=out_shape, scratch_shapes=scratch,
        compiler_params=_cparams(3), name="mlp",
    )(x, h, mod.arr, w_up, w_down, final_norm)


def _rope_tables(pos):
    half = ROPE_DIM // 2
    inv = ROPE_BASE ** (-jnp.arange(half, dtype=F32) / half)
    ang = pos.astype(F32)[:, None] * inv[None, :]
    cos, sin, zero = jnp.cos(ang), jnp.sin(ang), jnp.zeros_like(ang)
    return (jnp.concatenate([cos, cos, zero, zero], axis=-1),
            jnp.concatenate([-sin, zero, zero, zero], axis=-1),
            jnp.concatenate([zero, sin, zero, zero], axis=-1))


def _prep_weights(q_norm, w_uq, kv_norm, w_ukv, conv_w, attn_out_norm, conv_out_norm,
                  w_out, w_up, w_down):
    depth = w_uq.shape[0]
    wq = w_uq.astype(BF16).reshape(depth, Q_LORA, N_HEADS, QK_DIM)
    wq = jnp.pad(wq, ((0, 0), (0, 0), (0, 0), (0, HEAD_W - QK_DIM)))
    return dict(
        w_uq=wq.reshape(depth, Q_LORA, N_HEADS * HEAD_W),
        w_ukv=w_ukv,
        q_norm=q_norm[:, None, :], kv_norm=kv_norm[:, None, :],
        conv_w=conv_w, attn_out_norm=attn_out_norm[:, None, :],
        conv_out_norm=conv_out_norm[:, None, :],
        w_out=w_out, w_up=w_up, w_down=w_down)


def kernel(x_prompt, x_sample, c_prompt, c_sample, cache_kv_latent, cache_k_rope, state_conv,
           w_ada, b_ada, w_in, q_norm, w_uq, kv_norm, w_ukv, conv_w,
           attn_out_norm, conv_out_norm, w_out, w_up, w_down, final_norm):
    bp, sp, d = x_prompt.shape
    bs, ss, _ = x_sample.shape
    depth = w_ada.shape[0]
    past = cache_kv_latent.shape[2]
    dc = conv_w.shape[-1]
    ms = bs * ss

    w = _prep_weights(q_norm, w_uq, kv_norm, w_ukv, conv_w, attn_out_norm,
                      conv_out_norm, w_out, w_up, w_down)
    w_in_t = jnp.swapaxes(w_in, 1, 2)
    fnorm = final_norm[None, :]
    tabs_p = _rope_tables(jnp.arange(sp))
    tabs_s = tuple(jnp.tile(tb, (bs, 1)) for tb in _rope_tables(past + jnp.arange(ss)))

    mod = _modulation(jnp.concatenate([c_prompt, c_sample], axis=0), w_ada, b_ada)
    mod = mod.reshape(depth, bp + bs, N_MOD, d)
    mod_p = mod[:, :bp].reshape(depth * bp * N_MOD, 1, d)
    mod_s = jnp.swapaxes(mod[:, bp:], 1, 2).reshape(depth * N_MOD, bs, d)
    zero_conv = jnp.zeros((1, bp, CONV_W - 1, dc), F32)

    xp = x_prompt.reshape(bp * sp, d)
    xs = x_sample.reshape(ms, d)
    past_kpe_t = jnp.swapaxes(cache_k_rope, 2, 3)
    stacks = None
    outs = [[] for _ in range(4)]
    for l in range(depth):
        last = l == depth - 1
        mp = _Mod(mod_p, l, bp)
        msm = _Mod(mod_s, l, 0)

        (qa, qpe, ckv_s, kpe_s, conv_ns, cstate_s, w_a_l, w_b_l, w_uk_l, w_uv_l, _,
         w_uv_h_l) = _mixer_in_sample(xs, msm, w, w_in_t, l, tabs_s, state_conv, bs, ss)
        attn_s = _sample_attention(qa, qpe, cache_kv_latent, past_kpe_t, ckv_s, kpe_s,
                                   w_uv_h_l, l)
        xs, hs, w_out_l = _out_proj(xs, attn_s, conv_ns, msm, w, w["w_out"], l, 1, ms, tm=ms)
        xs, w_up_l, w_down_l = _mlp(xs, hs, msm, w["w_up"], w["w_down"], l, fnorm, 1, ms,
                                    tm=ms, tf=512, final=last)
        outs[1].append(ckv_s.reshape(bs, ss, KV_LORA))
        outs[2].append(kpe_s.reshape(bs, ss, ROPE_DIM))
        outs[3].append(cstate_s)

        q, k, v, *stacks, conv_n, cstate = _mixer_in_prompt(
            xp, mp, w, w_a_l, w_b_l, w_uk_l, w_uv_l, l, tabs_p, zero_conv, bp, sp, stacks,
            tm=256, nsub=1)
        attn = _prompt_attention(q, k, v, bp, sp)
        xp, hp = _out_proj(xp, attn, conv_n, mp, w, w_out_l, None, bp, sp, tm=512, groups=(384, 128))
        xp, = _mlp(xp, hp, mp, w_up_l, w_down_l, None, fnorm, bp, sp, tm=1024, tf=1024, final=last)
        outs[0].append(cstate)

    return (xp.reshape(bp, sp, d), xs.reshape(bs, ss, d),
            stacks[0].reshape(depth, bp, sp, KV_LORA), jnp.swapaxes(stacks[1], 2, 3),
            jnp.stack(outs[0]), jnp.stack(outs[1]), jnp.stack(outs[2]), jnp.stack(outs[3]))
```

```python
import functools
import math

import jax
import jax.numpy as jnp
from jax import lax
from jax.experimental import pallas as pl
from jax.experimental.pallas import tpu as pltpu

F32 = jnp.float32
BF16 = jnp.bfloat16

CHUNK = 64
N_HEADS = 8
NOPE_DIM = 128
ROPE_DIM = 64
V_DIM = 128
QK_DIM = NOPE_DIM + ROPE_DIM
Q_LORA = 512
KV_LORA = 512
D_ATTN = N_HEADS * V_DIM
CONV_W = 3
N_MOD = 6
ROPE_BASE = 10000.0
EPS = 1e-6
ATTN_SCALE = QK_DIM ** -0.5

LANES = 128
HEAD_W = 2 * LANES
ROPE_W = LANES
IN_A_W = Q_LORA + KV_LORA + ROPE_W
CONV_PAD = 8
Q_SCALE = ATTN_SCALE * math.log2(math.e)
NEG_BIG = -1e30
VMEM_LIMIT = 60000 * 1024


def _cparams(n_axes):
    return pltpu.CompilerParams(dimension_semantics=("arbitrary",) * n_axes,
                                vmem_limit_bytes=VMEM_LIMIT)


def _rms(x):
    return x * lax.rsqrt(jnp.mean(x * x, axis=-1, keepdims=True) + EPS)


def _dot(a, b):
    return jnp.dot(a, b, preferred_element_type=F32)


def _dot_t(a, b):
    return lax.dot_general(a, b, (((1,), (1,)), ((), ())), preferred_element_type=F32)


def _layer_spec(arr, layer, block=None, index=None):
    block = tuple(arr.shape[1:]) if block is None else tuple(block)
    index = (0,) * len(block) if index is None else tuple(index)
    return pl.BlockSpec((None,) + block, lambda *g: (layer,) + index,
                        pipeline_mode=pl.Buffered(1))


def _rows(tm, cols, tiles_per_batch):
    return pl.BlockSpec((tm, cols), lambda b, s, *_: (b * tiles_per_batch + s, 0))


class _Mod:
    def __init__(self, arr, layer, per_batch):
        self.arr, self.layer, self.per_batch = arr, layer, per_batch

    def spec(self, role):
        layer, nb = self.layer, self.per_batch
        if nb:
            return pl.BlockSpec((None,) + self.arr.shape[1:],
                                lambda b, *_: ((layer * nb + b) * N_MOD + role, 0, 0))
        return pl.BlockSpec((None,) + self.arr.shape[1:],
                            lambda *_: (layer * N_MOD + role, 0, 0))


def _mod_kernel(c_ref, w_ref, b_ref, o_ref):
    c = c_ref[...]
    s = (c * jax.nn.sigmoid(c)).astype(BF16)
    o_ref[0] = _dot(s, w_ref[0].astype(BF16)) + b_ref[0]


def _modulation(c_all, w_ada, b_ada, tn=1024):
    depth, d, n = w_ada.shape
    r = c_all.shape[0]
    return pl.pallas_call(
        _mod_kernel,
        grid=(depth, n // tn),
        in_specs=[pl.BlockSpec((r, d), lambda l, j: (0, 0)),
                  pl.BlockSpec((1, d, tn), lambda l, j: (l, 0, j)),
                  pl.BlockSpec((1, 1, tn), lambda l, j: (l, 0, j))],
        out_specs=pl.BlockSpec((1, r, tn), lambda l, j: (l, 0, j)),
        out_shape=jax.ShapeDtypeStruct((depth, r, n), F32),
        compiler_params=_cparams(2),
        name="modulation",
    )(c_all, w_ada, b_ada.reshape(depth, 1, n))


def _rope128(p, cos, s1, s2):
    return p * cos + pltpu.roll(p, 96, 1) * s1 + pltpu.roll(p, 32, 1) * s2


def _mod_rows(m_ref, rows):
    m = m_ref[...]
    nb, d = m.shape
    if nb in (1, rows):
        return m
    return jnp.broadcast_to(m[:, None, :], (nb, rows // nb, d)).reshape(rows, d)


def _adaln(x_ref, rs, sh_ref, sc_ref):
    x = x_ref[rs, :]
    n = x.shape[0]
    return (_rms(x) * (1.0 + _mod_rows(sc_ref, n)) + _mod_rows(sh_ref, n)).astype(BF16)


def _attn_in_common(h, rs, win_ref, qn_ref, wuq_ref, kvn_ref, cos_ref, s1_ref, s2_ref):
    z = _dot(h, win_ref[...])
    cos, s1, s2 = cos_ref[rs, :], s1_ref[rs, :], s2_ref[rs, :]
    qn = (_rms(z[:, :Q_LORA]) * qn_ref[...]).astype(BF16)
    q = _dot(qn, wuq_ref[...]) * Q_SCALE
    ckv = _rms(z[:, Q_LORA:Q_LORA + KV_LORA]) * kvn_ref[...]
    kp = _rope128(z[:, Q_LORA + KV_LORA:], cos, s1, s2)
    rope = functools.partial(_rope128, cos=cos, s1=s1, s2=s2)
    return q, ckv, kp, rope


def _conv_group(h, p0, tg, nb, w_ref, cw_ref, cn_ref, buf):
    dc = cw_ref.shape[-1]
    z = _dot(h, w_ref[...])
    return _conv_post(z[:, :dc], z[:, dc:2 * dc], z[:, 2 * dc:], p0, tg, nb, cw_ref, cn_ref, buf)


def _conv_post(gate_b, gate_c, u, p0, tg, nb, cw_ref, cn_ref, buf):
    dc = cw_ref.shape[-1]
    cu = gate_c * u
    buf[:, CONV_PAD + p0:CONV_PAD + p0 + tg, :] = cu.reshape(nb, tg, dc)
    base = CONV_PAD - (CONV_W - 1) + p0
    y = cw_ref[0:1, :].reshape(1, 1, dc) * buf[:, base:base + tg, :]
    for k in range(1, CONV_W):
        y = y + cw_ref[k:k + 1, :].reshape(1, 1, dc) * buf[:, base + k:base + k + tg, :]
    conv_o = gate_b * y.reshape(nb * tg, dc)
    return (_rms(conv_o) * cn_ref[...]).astype(BF16)


def _conv_carry(buf, st_ref, t):
    last = buf[:, CONV_PAD + t - 2:CONV_PAD + t, :]
    st_ref[...] = last
    buf[:, CONV_PAD - 2:CONV_PAD, :] = last


def _mixer_in_prompt_kernel(x_ref, sh_ref, sc_ref, win_ref, qn_ref, wuq_ref, kvn_ref,
                            cos_ref, s1_ref, s2_ref, wuk_ref, wuv_ref,
                            winb_ref, cw_ref, cn_ref, prev_ref, *rest, nsub, first):
    q_out, k_out, v_out, ckv_out, kpe_out, conv_out, st_ref, buf = rest[-8:]

    @pl.when(pl.program_id(1) == 0)
    def _():
        buf[:, CONV_PAD - 2:CONV_PAD, :] = prev_ref[...]

    if first:
        ckv_out[1:] = jnp.zeros((ckv_out.shape[0] - 1,) + ckv_out.shape[1:], F32)
        kpe_out[1:] = jnp.zeros((kpe_out.shape[0] - 1,) + kpe_out.shape[1:], F32)
        ckv_out, kpe_out = ckv_out.at[0], kpe_out.at[0]
    rows = x_ref.shape[0] // nsub
    lane = lax.broadcasted_iota(jnp.int32, (rows, LANES), 1)
    ones_col = jnp.where(lane == 0, 1.0, 0.0).astype(BF16)
    for j in range(nsub):
        rs = slice(j * rows, (j + 1) * rows)
        h = _adaln(x_ref, rs, sh_ref, sc_ref)
        conv_out[rs, :] = _conv_group(h, j * rows, rows, 1, winb_ref, cw_ref, cn_ref, buf)
        q, ckv, kp, rope = _attn_in_common(h, rs, win_ref, qn_ref, wuq_ref, kvn_ref,
                                           cos_ref, s1_ref, s2_ref)
        ckv_out[rs, :] = ckv
        kpe_out[:, rs] = kp.T[:ROPE_DIM, :]
        ckb = ckv.astype(BF16)
        kn = _dot(ckb, wuk_ref[...])
        v = _dot(ckb, wuv_ref[...]).astype(BF16)
        kpb = kp.astype(BF16)
        for hd in range(N_HEADS):
            c0 = hd * HEAD_W
            v_out[rs, c0:c0 + V_DIM] = v[:, hd * V_DIM:(hd + 1) * V_DIM]
            v_out[rs, c0 + V_DIM:c0 + HEAD_W] = ones_col
            q_out[rs, c0:c0 + NOPE_DIM] = q[:, c0:c0 + NOPE_DIM].astype(BF16)
            q_out[rs, c0 + NOPE_DIM:c0 + HEAD_W] = rope(q[:, c0 + NOPE_DIM:c0 + HEAD_W]).astype(BF16)
            k_out[rs, c0:c0 + NOPE_DIM] = kn[:, hd * NOPE_DIM:(hd + 1) * NOPE_DIM].astype(BF16)
            k_out[rs, c0 + NOPE_DIM:c0 + HEAD_W] = kpb
    _conv_carry(buf, st_ref, x_ref.shape[0])


def _mixer_in_sample_kernel(x_ref, sh_ref, sc_ref, wint_hbm, qn_ref, wuq_ref, kvn_ref,
                            cos_ref, s1_ref, s2_ref, wukv_ref, cw_ref, cn_ref, prev_ref,
                            qa_out, qpe_out, ckv_out, kpe_out, conv_out, st_ref, wa_out, wb_out,
                            wuk_out, wuv_out, wukt_out, wuvh_out, wuq_out,
                            w_buf, w_sem, h_scr, z_scr, buf, *, nb, t, layer):
    s = pl.program_id(0)
    dc = cw_ref.shape[-1]
    n_a = Q_LORA + KV_LORA + ROPE_DIM
    chunk_rows = [(0, IN_A_W)] + [(n_a + j * dc, dc) for j in range(3)]

    def chunk_copy(k):
        r0, rows = chunk_rows[k]
        return pltpu.make_async_copy(wint_hbm.at[layer, pl.ds(r0, rows), :],
                                     w_buf.at[k % 2, pl.ds(0, rows), :], w_sem.at[k % 2])

    for k in range(len(chunk_rows)):
        @pl.when(s == k)
        def _(k=k):
            if k == 0:
                chunk_copy(0).start()
                h_scr[...] = _adaln(x_ref, slice(None), sh_ref, sc_ref)
                buf[:, CONV_PAD - 2:CONV_PAD, :] = prev_ref[...]
            if k + 1 < len(chunk_rows):
                chunk_copy(k + 1).start()
            chunk_copy(k).wait()
            rows = chunk_rows[k][1]
            w_t = w_buf[k % 2, 0:rows, :].T.astype(BF16)
            if k == 0:
                wa_out[...] = w_t
                for hd in range(N_HEADS):
                    c0 = hd * HEAD_W
                    wuq_out[:, c0:c0 + QK_DIM] = wuq_ref[:, hd * QK_DIM:(hd + 1) * QK_DIM].astype(BF16)
                    wuq_out[:, c0 + QK_DIM:c0 + HEAD_W] = jnp.zeros(
                        (wuq_ref.shape[0], HEAD_W - QK_DIM), BF16)
                q, ckv, kp, rope = _attn_in_common(h_scr[...], slice(None), wa_out, qn_ref,
                                                   wuq_out, kvn_ref, cos_ref, s1_ref, s2_ref)
                ckv_out[...] = ckv
                kpe_out[...] = kp[:, :ROPE_DIM]
                for hd in range(N_HEADS):
                    k0 = hd * (NOPE_DIM + V_DIM)
                    w_k = wukv_ref[:, k0:k0 + NOPE_DIM]
                    w_v = wukv_ref[:, k0 + NOPE_DIM:k0 + NOPE_DIM + V_DIM].astype(BF16)
                    wuk_out[:, hd * NOPE_DIM:(hd + 1) * NOPE_DIM] = w_k.astype(BF16)
                    wukt_out[hd] = w_k.T.astype(BF16)
                    wuv_out[:, hd * V_DIM:(hd + 1) * V_DIM] = w_v
                    wuvh_out[hd] = w_v
                for hd in range(N_HEADS):
                    c0 = hd * HEAD_W
                    qa = _dot(q[:, c0:c0 + NOPE_DIM].astype(BF16), wukt_out[hd])
                    qa_out[:, hd] = qa.reshape(nb, t, KV_LORA).astype(BF16)
                    qp = rope(q[:, c0 + NOPE_DIM:c0 + HEAD_W])
                    qpe_out[:, hd] = qp.reshape(nb, t, ROPE_W).astype(BF16)
            else:
                wb_out[...] = w_t
                z_scr[k - 1] = _dot(h_scr[...], wb_out[...])
            if k == len(chunk_rows) - 1:
                conv_out[...] = _conv_post(z_scr[0], z_scr[1], z_scr[2], 0, t, nb,
                                           cw_ref, cn_ref, buf)
                _conv_carry(buf, st_ref, t)


def _mixer_in_prompt(x, mod, w, w_a, w_b, w_uk, w_uv, w_uq, layer, tabs, prev, nbatch, seq,
                     stacks, tm, nsub):
    m, d = x.shape
    depth = w["q_norm"].shape[0]
    dc = w["conv_w"].shape[-1]
    nt = seq // tm
    tab_spec = pl.BlockSpec((tm, ROPE_W), lambda b, s: (s, 0))
    whole = lambda a: pl.BlockSpec(a.shape, lambda b, s: (0,) * a.ndim,
                                   pipeline_mode=pl.Buffered(1))
    ins = [x, mod.arr, mod.arr, w_a, w["q_norm"], w_uq, w["kv_norm"],
           *tabs, w_uk, w_uv, w_b, w["conv_w"], w["conv_out_norm"], prev]
    specs = [_rows(tm, d, nt), mod.spec(0), mod.spec(1), whole(w_a),
             _layer_spec(w["q_norm"], layer), whole(w_uq),
             _layer_spec(w["kv_norm"], layer), tab_spec, tab_spec, tab_spec,
             whole(w_uk), whole(w_uv),
             whole(w_b), _layer_spec(w["conv_w"], layer),
             _layer_spec(w["conv_out_norm"], layer),
             pl.BlockSpec((None, 1, CONV_W - 1, dc), lambda b, s: (0, b, 0, 0))]
    aliases = {}
    if stacks is not None:
        aliases = {len(ins): 3, len(ins) + 1: 4}
        ins += list(stacks)
        specs += [pl.BlockSpec(memory_space=pl.ANY)] * 2
    outs = [jax.ShapeDtypeStruct((m, N_HEADS * HEAD_W), BF16),
            jax.ShapeDtypeStruct((m, N_HEADS * HEAD_W), BF16),
            jax.ShapeDtypeStruct((m, N_HEADS * HEAD_W), BF16),
            jax.ShapeDtypeStruct((depth, m, KV_LORA), F32),
            jax.ShapeDtypeStruct((depth, nbatch, ROPE_DIM, seq), F32),
            jax.ShapeDtypeStruct((m, dc), BF16),
            jax.ShapeDtypeStruct((nbatch, CONV_W - 1, dc), F32)]
    first = stacks is None
    lsel = (depth, 0) if first else (None, layer)
    out_specs = [_rows(tm, N_HEADS * HEAD_W, nt), _rows(tm, N_HEADS * HEAD_W, nt),
                 _rows(tm, N_HEADS * HEAD_W, nt),
                 pl.BlockSpec((lsel[0], tm, KV_LORA), lambda b, s: (lsel[1], b * nt + s, 0)),
                 pl.BlockSpec((lsel[0], None, ROPE_DIM, tm), lambda b, s: (lsel[1], b, 0, s)),
                 _rows(tm, dc, nt),
                 pl.BlockSpec((1, CONV_W - 1, dc), lambda b, s: (b, 0, 0))]
    return pl.pallas_call(
        functools.partial(_mixer_in_prompt_kernel, nsub=nsub, first=first), grid=(nbatch, nt),
        in_specs=specs, out_specs=out_specs, out_shape=outs, input_output_aliases=aliases,
        scratch_shapes=[pltpu.VMEM((1, CONV_PAD + tm, dc), F32)],
        compiler_params=_cparams(2), name="mixer_in_prompt",
    )(*ins)


def _mixer_in_sample(x, mod, w, w_in_t, layer, tabs, prev, nb, t):
    m, d = x.shape
    dc = w["conv_w"].shape[-1]
    steps = 4
    assert w_in_t.shape[1] == Q_LORA + KV_LORA + ROPE_DIM + 3 * dc
    const = lambda a: pl.BlockSpec(a.shape, lambda s: (0,) * a.ndim)
    ins = [x, mod.arr, mod.arr, w_in_t, w["q_norm"], w["w_uq"], w["kv_norm"],
           *tabs, w["w_ukv"], w["conv_w"], w["conv_out_norm"], prev]
    specs = [const(x), mod.spec(0), mod.spec(1), pl.BlockSpec(memory_space=pl.ANY),
             _layer_spec(w["q_norm"], layer), _layer_spec(w["w_uq"], layer),
             _layer_spec(w["kv_norm"], layer), const(tabs[0]), const(tabs[1]), const(tabs[2]),
             _layer_spec(w["w_ukv"], layer), _layer_spec(w["conv_w"], layer),
             _layer_spec(w["conv_out_norm"], layer),
             pl.BlockSpec((None, nb, CONV_W - 1, dc), lambda s: (layer, 0, 0, 0))]
    outs = [jax.ShapeDtypeStruct((nb, N_HEADS, t, KV_LORA), BF16),
            jax.ShapeDtypeStruct((nb, N_HEADS, t, ROPE_W), BF16),
            jax.ShapeDtypeStruct((m, KV_LORA), F32),
            jax.ShapeDtypeStruct((m, ROPE_DIM), F32),
            jax.ShapeDtypeStruct((m, dc), BF16),
            jax.ShapeDtypeStruct((nb, CONV_W - 1, dc), F32),
            jax.ShapeDtypeStruct((d, IN_A_W), BF16),
            jax.ShapeDtypeStruct((d, 3 * dc), BF16),
            jax.ShapeDtypeStruct((KV_LORA, N_HEADS * NOPE_DIM), BF16),
            jax.ShapeDtypeStruct((KV_LORA, N_HEADS * V_DIM), BF16),
            jax.ShapeDtypeStruct((N_HEADS, NOPE_DIM, KV_LORA), BF16),
            jax.ShapeDtypeStruct((N_HEADS, KV_LORA, V_DIM), BF16),
            jax.ShapeDtypeStruct((Q_LORA, N_HEADS * HEAD_W), BF16)]
    out_specs = [const(o) for o in outs[:7]]
    out_specs.append(pl.BlockSpec((d, dc), lambda s: (0, jnp.maximum(s - 1, 0))))
    out_specs += [const(o) for o in outs[8:]]
    return pl.pallas_call(
        functools.partial(_mixer_in_sample_kernel, nb=nb, t=t, layer=layer), grid=(steps,),
        in_specs=specs, out_specs=out_specs, out_shape=outs,
        scratch_shapes=[pltpu.VMEM((2, IN_A_W, d), F32), pltpu.SemaphoreType.DMA((2,)),
                        pltpu.VMEM((m, d), BF16), pltpu.VMEM((3, m, dc), F32),
                        pltpu.VMEM((nb, CONV_PAD + t, dc), F32)],
        compiler_params=_cparams(1), name="mixer_in_sample",
    )(*ins)


def _prompt_attn_kernel(q_ref, k_ref, v_ref, o_ref, s_buf, p_buf, *, seq, tq, heads):
    row_chunk = lax.broadcasted_iota(jnp.int32, (tq, tq), 0) // CHUNK
    col_chunk = lax.broadcasted_iota(jnp.int32, (tq, tq), 1) // CHUNK
    visible = col_chunk <= row_chunk
    tiles = [(hd, qi) for hd in range(heads) for qi in range(seq // tq)]

    def scores(i):
        hd, qi = tiles[i]
        hi = (qi + 1) * tq
        cols = slice(hd * HEAD_W, (hd + 1) * HEAD_W)
        s_buf[i % 2, :, 0:hi] = _dot_t(q_ref[qi * tq:hi, cols], k_ref[0:hi, cols])

    scores(0)
    for i, (hd, qi) in enumerate(tiles):
        lo, hi = qi * tq, (qi + 1) * tq
        if i + 1 < len(tiles):
            scores(i + 1)
        s_d = jnp.where(visible, s_buf[i % 2, :, lo:hi], NEG_BIG)
        m = jnp.max(s_d, axis=-1, keepdims=True)
        if qi > 0:
            m = jnp.maximum(m, jnp.max(s_buf[i % 2, :, 0:lo], axis=-1, keepdims=True))
            p_buf[:, 0:lo] = jnp.exp2(s_buf[i % 2, :, 0:lo] - m).astype(BF16)
        p_buf[:, lo:hi] = jnp.exp2(s_d - m).astype(BF16)
        oe = _dot(p_buf[:, 0:hi], v_ref[0:hi, hd * HEAD_W:(hd + 1) * HEAD_W])
        o_ref[lo:hi, hd * V_DIM:(hd + 1) * V_DIM] = oe[:, :V_DIM] / oe[:, V_DIM:V_DIM + 1]


def _prompt_attention(q, k, v, nbatch, seq, tq=256, heads=4):
    m = q.shape[0]
    qkv_spec = pl.BlockSpec((seq, heads * HEAD_W), lambda b, h: (b, h))
    return pl.pallas_call(
        functools.partial(_prompt_attn_kernel, seq=seq, tq=tq, heads=heads),
        grid=(nbatch, N_HEADS // heads),
        in_specs=[qkv_spec, qkv_spec, qkv_spec],
        out_specs=pl.BlockSpec((seq, heads * V_DIM), lambda b, h: (b, h)),
        out_shape=jax.ShapeDtypeStruct((m, D_ATTN), F32),
        scratch_shapes=[pltpu.VMEM((2, tq, seq), F32), pltpu.VMEM((tq, seq), BF16)],
        compiler_params=_cparams(2), name="prompt_attention",
    )(q, k, v)


def _sample_attn_kernel(qa_ref, qpe_ref, pc_ref, pk_ref, nc_ref, nk_ref, wuv_ref, o_ref, *, t):
    rows = N_HEADS * t
    qa = qa_ref[0].reshape(rows, KV_LORA)
    qp = qpe_ref[0].reshape(rows, ROPE_W)[:, :ROPE_DIM]
    pc = pc_ref[0].astype(BF16)
    pk_t = pk_ref[0].astype(BF16)
    nc = nc_ref[...].astype(BF16)
    nk = nk_ref[...].astype(BF16)
    s_p = _dot_t(qa, pc) + _dot(qp, pk_t)
    s_n = _dot_t(qa, nc) + _dot_t(qp, nk)
    m = jnp.maximum(jnp.max(s_p, axis=-1, keepdims=True), jnp.max(s_n, axis=-1, keepdims=True))
    p_p = jnp.exp2(s_p - m)
    p_n = jnp.exp2(s_n - m)
    l = jnp.sum(p_p, axis=-1, keepdims=True) + jnp.sum(p_n, axis=-1, keepdims=True)
    o_lat = (_dot(p_p.astype(BF16), pc) + _dot(p_n.astype(BF16), nc)) / l
    o_lat = o_lat.astype(BF16)
    for hd in range(N_HEADS):
        o_ref[:, hd * V_DIM:(hd + 1) * V_DIM] = _dot(o_lat[hd * t:(hd + 1) * t, :], wuv_ref[hd])


def _sample_attention(qa, qpe, past_ckv, past_kpe_t, new_ckv, new_kpe, w_uv_h, layer):
    nb, _, t, _ = qa.shape
    past = past_ckv.shape[2]
    return pl.pallas_call(
        functools.partial(_sample_attn_kernel, t=t),
        grid=(nb,),
        in_specs=[pl.BlockSpec((1, N_HEADS, t, KV_LORA), lambda b: (b, 0, 0, 0)),
                  pl.BlockSpec((1, N_HEADS, t, ROPE_W), lambda b: (b, 0, 0, 0)),
                  pl.BlockSpec((None, 1, past, KV_LORA), lambda b: (layer, b, 0, 0)),
                  pl.BlockSpec((None, 1, ROPE_DIM, past), lambda b: (layer, b, 0, 0)),
                  pl.BlockSpec((t, KV_LORA), lambda b: (b, 0)),
                  pl.BlockSpec((t, ROPE_DIM), lambda b: (b, 0)),
                  pl.BlockSpec(w_uv_h.shape, lambda b: (0, 0, 0))],
        out_specs=pl.BlockSpec((t, D_ATTN), lambda b: (b, 0)),
        out_shape=jax.ShapeDtypeStruct((nb * t, D_ATTN), F32),
        compiler_params=_cparams(1), name="sample_attention",
    )(qa, qpe, past_ckv, past_kpe_t, new_ckv, new_kpe, w_uv_h)


def _out_proj_kernel(x_ref, attn_ref, conv_ref, g_ref, shm_ref, scm_ref, an_ref, woa_ref, woc_ref,
                     o_ref, h_ref, *bf16_w_out, groups):
    if bf16_w_out:
        wo_ref, = bf16_w_out
        da = woa_ref.shape[0]
        wo_ref[0:da, :] = woa_ref[...].astype(BF16)
        wo_ref[da:, :] = woc_ref[...].astype(BF16)
        woa, woc = wo_ref[0:da, :], wo_ref[da:, :]
    else:
        woa, woc = woa_ref[...], woc_ref[...]
    starts = [sum(groups[:j]) for j in range(len(groups))]
    for r0, rows in zip(starts, groups):
        rs = slice(r0, r0 + rows) if len(groups) > 1 else slice(None)
        a = (_rms(attn_ref[rs, :]) * an_ref[...]).astype(BF16)
        mix = _dot(a, woa) + _dot(conv_ref[rs, :], woc)
        y = x_ref[rs, :] + _mod_rows(g_ref, rows) * mix
        o_ref[rs, :] = y
        h_ref[rs, :] = (_rms(y) * (1.0 + _mod_rows(scm_ref, rows))
                        + _mod_rows(shm_ref, rows)).astype(BF16)


def _out_proj(x, attn, conv_n, mod, w, w_out, layer, nbatch, seq, tm, groups=None):
    groups = (tm,) if groups is None else tuple(groups)
    assert sum(groups) == tm
    m, d = x.shape
    nt = seq // tm
    da, dc = attn.shape[1], conv_n.shape[1]
    ins = [x, attn, conv_n, mod.arr, mod.arr, mod.arr, w["attn_out_norm"], w_out, w_out]
    specs = [_rows(tm, d, nt), _rows(tm, da, nt), _rows(tm, dc, nt),
             mod.spec(2), mod.spec(3), mod.spec(4),
             _layer_spec(w["attn_out_norm"], mod.layer)]
    out_specs = [_rows(tm, d, nt), _rows(tm, d, nt)]
    out_shape = [jax.ShapeDtypeStruct((m, d), F32), jax.ShapeDtypeStruct((m, d), BF16)]
    if layer is None:
        specs += [pl.BlockSpec((da, d), lambda b, s: (0, 0), pipeline_mode=pl.Buffered(1)),
                  pl.BlockSpec((dc, d), lambda b, s: (da // dc, 0), pipeline_mode=pl.Buffered(1))]
    else:
        assert nbatch * nt == 1
        specs += [_layer_spec(w_out, layer, (da, d), (0, 0)),
                  _layer_spec(w_out, layer, (dc, d), (da // dc, 0))]
        out_specs.append(pl.BlockSpec((da + dc, d), lambda b, s: (0, 0)))
        out_shape.append(jax.ShapeDtypeStruct((da + dc, d), BF16))
    return pl.pallas_call(
        functools.partial(_out_proj_kernel, groups=groups), grid=(nbatch, nt), in_specs=specs,
        out_specs=out_specs, out_shape=out_shape,
        compiler_params=_cparams(2), name="out_proj",
    )(*ins)


def _mlp_kernel(x_ref, h_ref, g_ref, wup_ref, wdn_ref, fn_ref, o_ref, *extra, nf, final, cast,
                x_rows):
    f = pl.program_id(2)
    if cast:
        wu_ref, wd_ref = extra
        wu_ref[...] = wup_ref[...].astype(BF16)
        wd_ref[...] = wdn_ref[...].astype(BF16)
    else:
        wu_ref, wd_ref = wup_ref, wdn_ref
    if x_rows:
        x_buf, x_sem = extra
        tile = pl.program_id(0) * pl.num_programs(1) + pl.program_id(1)
        row0 = pl.multiple_of(tile * x_rows, x_rows)
        x_copy = pltpu.make_async_copy(x_ref.at[pl.ds(row0, x_rows), :], x_buf, x_sem)

        @pl.when(f == 0)
        def _():
            x_copy.start()

    def partial_down():
        up = _dot(h_ref[...], wu_ref[...])
        act = jnp.square(jnp.maximum(up, 0.0)).astype(BF16)
        return _dot(act, wd_ref[...])

    @pl.when(f == 0)
    def _():
        o_ref[...] = partial_down()

    @pl.when(f > 0)
    def _():
        o_ref[...] += partial_down()

    @pl.when(f == nf - 1)
    def _():
        if x_rows:
            x_copy.wait()
            x = x_buf[...]
        else:
            x = x_ref[...]
        y = x + _mod_rows(g_ref, x.shape[0]) * o_ref[...]
        if final:
            y = _rms(y) * fn_ref[...]
        o_ref[...] = y


def _mlp(x, h, mod, w_up, w_down, layer, final_norm, nbatch, seq, tm, tf, final):
    m, d = x.shape
    nt = seq // tm
    dff = w_up.shape[-1]
    nf = dff // tf
    cast = layer is not None
    out_specs = [_rows(tm, d, nt)]
    out_shape = [jax.ShapeDtypeStruct((m, d), F32)]
    scratch = []
    if cast:
        assert nbatch * nt == 1
        x_spec = _rows(tm, d, nt)
        w_specs = [pl.BlockSpec((None, d, tf), lambda b, s, f: (layer, 0, f)),
                   pl.BlockSpec((None, tf, d), lambda b, s, f: (layer, f, 0))]
        out_specs += [pl.BlockSpec((d, tf), lambda b, s, f: (0, f)),
                      pl.BlockSpec((tf, d), lambda b, s, f: (f, 0))]
        out_shape += [jax.ShapeDtypeStruct((d, dff), BF16), jax.ShapeDtypeStruct((dff, d), BF16)]
    else:
        x_spec = pl.BlockSpec(memory_space=pl.ANY)
        w_specs = [pl.BlockSpec((d, tf), lambda b, s, f: (0, f)),
                   pl.BlockSpec((tf, d), lambda b, s, f: (f, 0))]
        scratch = [pltpu.VMEM((tm, d), F32), pltpu.SemaphoreType.DMA(())]
    return pl.pallas_call(
        functools.partial(_mlp_kernel, nf=nf, final=final, cast=cast,
                          x_rows=0 if cast else tm),
        grid=(nbatch, nt, nf),
        in_specs=[x_spec, _rows(tm, d, nt), mod.spec(5), *w_specs,
                  pl.BlockSpec(final_norm.shape, lambda b, s, f: (0, 0))],
        out_specs=out_specs, out_shape=out_shape, scratch_shapes=scratch,
        compiler_params=_cparams(3), name="mlp",
    )(x, h, mod.arr, w_up, w_down, final_norm)


def _rope_tables(pos):
    half = ROPE_DIM // 2
    inv = ROPE_BASE ** (-jnp.arange(half, dtype=F32) / half)
    ang = pos.astype(F32)[:, None] * inv[None, :]
    cos, sin, zero = jnp.cos(ang), jnp.sin(ang), jnp.zeros_like(ang)
    return (jnp.concatenate([cos, cos, zero, zero], axis=-1),
            jnp.concatenate([-sin, zero, zero, zero], axis=-1),
            jnp.concatenate([zero, sin, zero, zero], axis=-1))


def _prep_weights(q_norm, w_uq, kv_norm, w_ukv, conv_w, attn_out_norm, conv_out_norm,
                  w_out, w_up, w_down):
    return dict(
        w_uq=w_uq, w_ukv=w_ukv,
        q_norm=q_norm[:, None, :], kv_norm=kv_norm[:, None, :],
        conv_w=conv_w, attn_out_norm=attn_out_norm[:, None, :],
        conv_out_norm=conv_out_norm[:, None, :],
        w_out=w_out, w_up=w_up, w_down=w_down)


def kernel(x_prompt, x_sample, c_prompt, c_sample, cache_kv_latent, cache_k_rope, state_conv,
           w_ada, b_ada, w_in, q_norm, w_uq, kv_norm, w_ukv, conv_w,
           attn_out_norm, conv_out_norm, w_out, w_up, w_down, final_norm):
    bp, sp, d = x_prompt.shape
    bs, ss, _ = x_sample.shape
    depth = w_ada.shape[0]
    past = cache_kv_latent.shape[2]
    dc = conv_w.shape[-1]
    ms = bs * ss

    w = _prep_weights(q_norm, w_uq, kv_norm, w_ukv, conv_w, attn_out_norm,
                      conv_out_norm, w_out, w_up, w_down)
    w_in_t = jnp.swapaxes(w_in, 1, 2)
    fnorm = final_norm[None, :]
    tabs_p = _rope_tables(jnp.arange(sp))
    tabs_s = tuple(jnp.tile(tb, (bs, 1)) for tb in _rope_tables(past + jnp.arange(ss)))

    mod = _modulation(jnp.concatenate([c_prompt, c_sample], axis=0), w_ada, b_ada)
    mod = mod.reshape(depth, bp + bs, N_MOD, d)
    mod_p = mod[:, :bp].reshape(depth * bp * N_MOD, 1, d)
    mod_s = jnp.swapaxes(mod[:, bp:], 1, 2).reshape(depth * N_MOD, bs, d)
    zero_conv = jnp.zeros((1, bp, CONV_W - 1, dc), F32)

    xp = x_prompt.reshape(bp * sp, d)
    xs = x_sample.reshape(ms, d)
    past_kpe_t = jnp.swapaxes(cache_k_rope, 2, 3)
    stacks = None
    outs = [[] for _ in range(4)]
    for l in range(depth):
        last = l == depth - 1
        mp = _Mod(mod_p, l, bp)
        msm = _Mod(mod_s, l, 0)

        (qa, qpe, ckv_s, kpe_s, conv_ns, cstate_s, w_a_l, w_b_l, w_uk_l, w_uv_l, _,
         w_uv_h_l, w_uq_l) = _mixer_in_sample(xs, msm, w, w_in_t, l, tabs_s, state_conv, bs, ss)
        attn_s = _sample_attention(qa, qpe, cache_kv_latent, past_kpe_t, ckv_s, kpe_s,
                                   w_uv_h_l, l)
        xs, hs, w_out_l = _out_proj(xs, attn_s, conv_ns, msm, w, w["w_out"], l, 1, ms, tm=ms)
        xs, w_up_l, w_down_l = _mlp(xs, hs, msm, w["w_up"], w["w_down"], l, fnorm, 1, ms,
                                    tm=ms, tf=512, final=last)
        outs[1].append(ckv_s.reshape(bs, ss, KV_LORA))
        outs[2].append(kpe_s.reshape(bs, ss, ROPE_DIM))
        outs[3].append(cstate_s)

        q, k, v, *stacks, conv_n, cstate = _mixer_in_prompt(
            xp, mp, w, w_a_l, w_b_l, w_uk_l, w_uv_l, w_uq_l, l, tabs_p, zero_conv, bp, sp,
            stacks, tm=256, nsub=1)
        attn = _prompt_attention(q, k, v, bp, sp)
        xp, hp = _out_proj(xp, attn, conv_n, mp, w, w_out_l, None, bp, sp, tm=512, groups=(384, 128))
        xp, = _mlp(xp, hp, mp, w_up_l, w_down_l, None, fnorm, bp, sp, tm=1024, tf=1024, final=last)
        outs[0].append(cstate)

    return (xp.reshape(bp, sp, d), xs.reshape(bs, ss, d),
            stacks[0].reshape(depth, bp, sp, KV_LORA), jnp.swapaxes(stacks[1], 2, 3),
            jnp.stack(outs[0]), jnp.stack(outs[1]), jnp.stack(outs[2]), jnp.stack(outs[3]))
```

```python
import functools
import math

import jax
import jax.numpy as jnp
from jax import lax
from jax.experimental import pallas as pl
from jax.experimental.pallas import tpu as pltpu

F32 = jnp.float32
BF16 = jnp.bfloat16

CHUNK = 64
N_HEADS = 8
NOPE_DIM = 128
ROPE_DIM = 64
V_DIM = 128
QK_DIM = NOPE_DIM + ROPE_DIM
Q_LORA = 512
KV_LORA = 512
D_ATTN = N_HEADS * V_DIM
CONV_W = 3
N_MOD = 6
ROPE_BASE = 10000.0
EPS = 1e-6
ATTN_SCALE = QK_DIM ** -0.5

LANES = 128
HEAD_W = 2 * LANES
ROPE_W = LANES
IN_A_W = Q_LORA + KV_LORA + ROPE_W
CONV_PAD = 8
Q_SCALE = ATTN_SCALE * math.log2(math.e)
NEG_BIG = -1e30
VMEM_LIMIT = 60000 * 1024


def _cparams(n_axes):
    return pltpu.CompilerParams(dimension_semantics=("arbitrary",) * n_axes,
                                vmem_limit_bytes=VMEM_LIMIT)


def _rms(x):
    return x * lax.rsqrt(jnp.mean(x * x, axis=-1, keepdims=True) + EPS)


def _dot(a, b):
    return jnp.dot(a, b, preferred_element_type=F32)


def _dot_t(a, b):
    return lax.dot_general(a, b, (((1,), (1,)), ((), ())), preferred_element_type=F32)


def _layer_spec(arr, layer, block=None, index=None):
    block = tuple(arr.shape[1:]) if block is None else tuple(block)
    index = (0,) * len(block) if index is None else tuple(index)
    return pl.BlockSpec((None,) + block, lambda *g: (layer,) + index,
                        pipeline_mode=pl.Buffered(1))


def _rows(tm, cols, tiles_per_batch):
    return pl.BlockSpec((tm, cols), lambda b, s, *_: (b * tiles_per_batch + s, 0))


class _Mod:
    def __init__(self, arr, layer, per_batch):
        self.arr, self.layer, self.per_batch = arr, layer, per_batch

    def spec(self, role):
        layer, nb = self.layer, self.per_batch
        if nb:
            return pl.BlockSpec((None,) + self.arr.shape[1:],
                                lambda b, *_: ((layer * nb + b) * N_MOD + role, 0, 0))
        return pl.BlockSpec((None,) + self.arr.shape[1:],
                            lambda *_: (layer * N_MOD + role, 0, 0))


def _mod_kernel(c_ref, w_ref, b_ref, o_ref):
    c = c_ref[...]
    s = (c * jax.nn.sigmoid(c)).astype(BF16)
    o_ref[0] = _dot(s, w_ref[0].astype(BF16)) + b_ref[0]


def _modulation(c_all, w_ada, b_ada, tn=2048):
    depth, d, n = w_ada.shape
    r = c_all.shape[0]
    return pl.pallas_call(
        _mod_kernel,
        grid=(depth, n // tn),
        in_specs=[pl.BlockSpec((r, d), lambda l, j: (0, 0)),
                  pl.BlockSpec((1, d, tn), lambda l, j: (l, 0, j)),
                  pl.BlockSpec((1, 1, tn), lambda l, j: (l, 0, j))],
        out_specs=pl.BlockSpec((1, r, tn), lambda l, j: (l, 0, j)),
        out_shape=jax.ShapeDtypeStruct((depth, r, n), F32),
        compiler_params=_cparams(2),
        name="modulation",
    )(c_all, w_ada, b_ada.reshape(depth, 1, n))


def _rope128(p, cos, s1, s2):
    return p * cos + pltpu.roll(p, 96, 1) * s1 + pltpu.roll(p, 32, 1) * s2


def _mod_rows(m_ref, rows):
    m = m_ref[...]
    nb, d = m.shape
    if nb in (1, rows):
        return m
    return jnp.broadcast_to(m[:, None, :], (nb, rows // nb, d)).reshape(rows, d)


def _adaln(x_ref, rs, sh_ref, sc_ref):
    x = x_ref[rs, :]
    n = x.shape[0]
    return (_rms(x) * (1.0 + _mod_rows(sc_ref, n)) + _mod_rows(sh_ref, n)).astype(BF16)


def _attn_in_common(h, rs, win_ref, qn_ref, wuq_ref, kvn_ref, cos_ref, s1_ref, s2_ref):
    z = _dot(h, win_ref[...])
    cos, s1, s2 = cos_ref[rs, :], s1_ref[rs, :], s2_ref[rs, :]
    qn = (_rms(z[:, :Q_LORA]) * qn_ref[...]).astype(BF16)
    q = _dot(qn, wuq_ref[...]) * Q_SCALE
    ckv = _rms(z[:, Q_LORA:Q_LORA + KV_LORA]) * kvn_ref[...]
    kp = _rope128(z[:, Q_LORA + KV_LORA:], cos, s1, s2)
    rope = functools.partial(_rope128, cos=cos, s1=s1, s2=s2)
    return q, ckv, kp, rope


def _conv_group(h, p0, tg, nb, w_ref, cw_ref, cn_ref, buf):
    dc = cw_ref.shape[-1]
    z = _dot(h, w_ref[...])
    return _conv_post(z[:, :dc], z[:, dc:2 * dc], z[:, 2 * dc:], p0, tg, nb, cw_ref, cn_ref, buf)


def _conv_post(gate_b, gate_c, u, p0, tg, nb, cw_ref, cn_ref, buf):
    dc = cw_ref.shape[-1]
    cu = gate_c * u
    buf[:, CONV_PAD + p0:CONV_PAD + p0 + tg, :] = cu.reshape(nb, tg, dc)
    base = CONV_PAD - (CONV_W - 1) + p0
    y = cw_ref[0:1, :].reshape(1, 1, dc) * buf[:, base:base + tg, :]
    for k in range(1, CONV_W):
        y = y + cw_ref[k:k + 1, :].reshape(1, 1, dc) * buf[:, base + k:base + k + tg, :]
    conv_o = gate_b * y.reshape(nb * tg, dc)
    return (_rms(conv_o) * cn_ref[...]).astype(BF16)


def _conv_carry(buf, st_ref, t):
    last = buf[:, CONV_PAD + t - 2:CONV_PAD + t, :]
    st_ref[...] = last
    buf[:, CONV_PAD - 2:CONV_PAD, :] = last


def _mixer_in_prompt_kernel(x_ref, sh_ref, sc_ref, win_ref, qn_ref, wuq_ref, kvn_ref,
                            cos_ref, s1_ref, s2_ref, wuk_ref, wuv_ref,
                            winb_ref, cw_ref, cn_ref, prev_ref, *rest, nsub, first):
    q_out, k_out, v_out, ckv_out, kpe_out, conv_out, st_ref, buf = rest[-8:]

    @pl.when(pl.program_id(1) == 0)
    def _():
        buf[:, CONV_PAD - 2:CONV_PAD, :] = prev_ref[...]

    if first:
        ckv_out[1:] = jnp.zeros((ckv_out.shape[0] - 1,) + ckv_out.shape[1:], F32)
        kpe_out[1:] = jnp.zeros((kpe_out.shape[0] - 1,) + kpe_out.shape[1:], F32)
        ckv_out, kpe_out = ckv_out.at[0], kpe_out.at[0]
    rows = x_ref.shape[0] // nsub
    lane = lax.broadcasted_iota(jnp.int32, (rows, LANES), 1)
    ones_col = jnp.where(lane == 0, 1.0, 0.0).astype(BF16)
    for j in range(nsub):
        rs = slice(j * rows, (j + 1) * rows)
        h = _adaln(x_ref, rs, sh_ref, sc_ref)
        conv_out[rs, :] = _conv_group(h, j * rows, rows, 1, winb_ref, cw_ref, cn_ref, buf)
        q, ckv, kp, rope = _attn_in_common(h, rs, win_ref, qn_ref, wuq_ref, kvn_ref,
                                           cos_ref, s1_ref, s2_ref)
        ckv_out[rs, :] = ckv
        kpe_out[:, rs] = kp.T[:ROPE_DIM, :]
        ckb = ckv.astype(BF16)
        kn = _dot(ckb, wuk_ref[...])
        v = _dot(ckb, wuv_ref[...]).astype(BF16)
        kpb = kp.astype(BF16)
        for hd in range(N_HEADS):
            c0 = hd * HEAD_W
            v_out[rs, c0:c0 + V_DIM] = v[:, hd * V_DIM:(hd + 1) * V_DIM]
            v_out[rs, c0 + V_DIM:c0 + HEAD_W] = ones_col
            q_out[rs, c0:c0 + NOPE_DIM] = q[:, c0:c0 + NOPE_DIM].astype(BF16)
            q_out[rs, c0 + NOPE_DIM:c0 + HEAD_W] = rope(q[:, c0 + NOPE_DIM:c0 + HEAD_W]).astype(BF16)
            k_out[rs, c0:c0 + NOPE_DIM] = kn[:, hd * NOPE_DIM:(hd + 1) * NOPE_DIM].astype(BF16)
            k_out[rs, c0 + NOPE_DIM:c0 + HEAD_W] = kpb
    _conv_carry(buf, st_ref, x_ref.shape[0])


def _mixer_in_sample_kernel(x_ref, sh_ref, sc_ref, wint_hbm, qn_ref, wuq_ref, kvn_ref,
                            cos_ref, s1_ref, s2_ref, wukv_ref, cw_ref, cn_ref, prev_ref,
                            qa_out, qpe_out, ckv_out, kpe_out, conv_out, st_ref, wa_out, wb_out,
                            wuk_out, wuv_out, wukt_out, wuvh_out, wuq_out,
                            w_buf, w_sem, h_scr, z_scr, buf, *, nb, t, layer):
    s = pl.program_id(0)
    dc = cw_ref.shape[-1]
    n_a = Q_LORA + KV_LORA + ROPE_DIM
    chunk_rows = [(0, IN_A_W)] + [(n_a + j * dc, dc) for j in range(3)]

    def chunk_copy(k):
        r0, rows = chunk_rows[k]
        return pltpu.make_async_copy(wint_hbm.at[layer, pl.ds(r0, rows), :],
                                     w_buf.at[k % 2, pl.ds(0, rows), :], w_sem.at[k % 2])

    for k in range(len(chunk_rows)):
        @pl.when(s == k)
        def _(k=k):
            if k == 0:
                chunk_copy(0).start()
                h_scr[...] = _adaln(x_ref, slice(None), sh_ref, sc_ref)
                buf[:, CONV_PAD - 2:CONV_PAD, :] = prev_ref[...]
            if k + 1 < len(chunk_rows):
                chunk_copy(k + 1).start()
            chunk_copy(k).wait()
            rows = chunk_rows[k][1]
            w_t = w_buf[k % 2, 0:rows, :].T.astype(BF16)
            if k == 0:
                wa_out[...] = w_t
                for hd in range(N_HEADS):
                    c0 = hd * HEAD_W
                    wuq_out[:, c0:c0 + QK_DIM] = wuq_ref[:, hd * QK_DIM:(hd + 1) * QK_DIM].astype(BF16)
                    wuq_out[:, c0 + QK_DIM:c0 + HEAD_W] = jnp.zeros(
                        (wuq_ref.shape[0], HEAD_W - QK_DIM), BF16)
                q, ckv, kp, rope = _attn_in_common(h_scr[...], slice(None), wa_out, qn_ref,
                                                   wuq_out, kvn_ref, cos_ref, s1_ref, s2_ref)
                ckv_out[...] = ckv
                kpe_out[...] = kp[:, :ROPE_DIM]
                for hd in range(N_HEADS):
                    k0 = hd * (NOPE_DIM + V_DIM)
                    w_k = wukv_ref[:, k0:k0 + NOPE_DIM]
                    w_v = wukv_ref[:, k0 + NOPE_DIM:k0 + NOPE_DIM + V_DIM].astype(BF16)
                    wuk_out[:, hd * NOPE_DIM:(hd + 1) * NOPE_DIM] = w_k.astype(BF16)
                    wukt_out[hd] = w_k.T.astype(BF16)
                    wuv_out[:, hd * V_DIM:(hd + 1) * V_DIM] = w_v
                    wuvh_out[hd] = w_v
                for hd in range(N_HEADS):
                    c0 = hd * HEAD_W
                    qa = _dot(q[:, c0:c0 + NOPE_DIM].astype(BF16), wukt_out[hd])
                    qa_out[:, hd] = qa.reshape(nb, t, KV_LORA).astype(BF16)
                    qp = rope(q[:, c0 + NOPE_DIM:c0 + HEAD_W])
                    qpe_out[:, hd] = qp.reshape(nb, t, ROPE_W).astype(BF16)
            else:
                wb_out[...] = w_t
                z_scr[k - 1] = _dot(h_scr[...], wb_out[...])
            if k == len(chunk_rows) - 1:
                conv_out[...] = _conv_post(z_scr[0], z_scr[1], z_scr[2], 0, t, nb,
                                           cw_ref, cn_ref, buf)
                _conv_carry(buf, st_ref, t)


def _mixer_in_prompt(x, mod, w, w_a, w_b, w_uk, w_uv, w_uq, layer, tabs, prev, nbatch, seq,
                     stacks, tm, nsub):
    m, d = x.shape
    depth = w["q_norm"].shape[0]
    dc = w["conv_w"].shape[-1]
    nt = seq // tm
    tab_spec = pl.BlockSpec((tm, ROPE_W), lambda b, s: (s, 0))
    whole = lambda a: pl.BlockSpec(a.shape, lambda b, s: (0,) * a.ndim,
                                   pipeline_mode=pl.Buffered(1))
    ins = [x, mod.arr, mod.arr, w_a, w["q_norm"], w_uq, w["kv_norm"],
           *tabs, w_uk, w_uv, w_b, w["conv_w"], w["conv_out_norm"], prev]
    specs = [_rows(tm, d, nt), mod.spec(0), mod.spec(1), whole(w_a),
             _layer_spec(w["q_norm"], layer), whole(w_uq),
             _layer_spec(w["kv_norm"], layer), tab_spec, tab_spec, tab_spec,
             whole(w_uk), whole(w_uv),
             whole(w_b), _layer_spec(w["conv_w"], layer),
             _layer_spec(w["conv_out_norm"], layer),
             pl.BlockSpec((None, 1, CONV_W - 1, dc), lambda b, s: (0, b, 0, 0))]
    aliases = {}
    if stacks is not None:
        aliases = {len(ins): 3, len(ins) + 1: 4}
        ins += list(stacks)
        specs += [pl.BlockSpec(memory_space=pl.ANY)] * 2
    outs = [jax.ShapeDtypeStruct((m, N_HEADS * HEAD_W), BF16),
            jax.ShapeDtypeStruct((m, N_HEADS * HEAD_W), BF16),
            jax.ShapeDtypeStruct((m, N_HEADS * HEAD_W), BF16),
            jax.ShapeDtypeStruct((depth, m, KV_LORA), F32),
            jax.ShapeDtypeStruct((depth, nbatch, ROPE_DIM, seq), F32),
            jax.ShapeDtypeStruct((m, dc), BF16),
            jax.ShapeDtypeStruct((nbatch, CONV_W - 1, dc), F32)]
    first = stacks is None
    lsel = (depth, 0) if first else (None, layer)
    out_specs = [_rows(tm, N_HEADS * HEAD_W, nt), _rows(tm, N_HEADS * HEAD_W, nt),
                 _rows(tm, N_HEADS * HEAD_W, nt),
                 pl.BlockSpec((lsel[0], tm, KV_LORA), lambda b, s: (lsel[1], b * nt + s, 0)),
                 pl.BlockSpec((lsel[0], None, ROPE_DIM, tm), lambda b, s: (lsel[1], b, 0, s)),
                 _rows(tm, dc, nt),
                 pl.BlockSpec((1, CONV_W - 1, dc), lambda b, s: (b, 0, 0))]
    return pl.pallas_call(
        functools.partial(_mixer_in_prompt_kernel, nsub=nsub, first=first), grid=(nbatch, nt),
        in_specs=specs, out_specs=out_specs, out_shape=outs, input_output_aliases=aliases,
        scratch_shapes=[pltpu.VMEM((1, CONV_PAD + tm, dc), F32)],
        compiler_params=_cparams(2), name="mixer_in_prompt",
    )(*ins)


def _mixer_in_sample(x, mod, w, w_in_t, layer, tabs, prev, nb, t):
    m, d = x.shape
    dc = w["conv_w"].shape[-1]
    steps = 4
    assert w_in_t.shape[1] == Q_LORA + KV_LORA + ROPE_DIM + 3 * dc
    const = lambda a: pl.BlockSpec(a.shape, lambda s: (0,) * a.ndim)
    ins = [x, mod.arr, mod.arr, w_in_t, w["q_norm"], w["w_uq"], w["kv_norm"],
           *tabs, w["w_ukv"], w["conv_w"], w["conv_out_norm"], prev]
    specs = [const(x), mod.spec(0), mod.spec(1), pl.BlockSpec(memory_space=pl.ANY),
             _layer_spec(w["q_norm"], layer), _layer_spec(w["w_uq"], layer),
             _layer_spec(w["kv_norm"], layer), const(tabs[0]), const(tabs[1]), const(tabs[2]),
             _layer_spec(w["w_ukv"], layer), _layer_spec(w["conv_w"], layer),
             _layer_spec(w["conv_out_norm"], layer),
             pl.BlockSpec((None, nb, CONV_W - 1, dc), lambda s: (layer, 0, 0, 0))]
    outs = [jax.ShapeDtypeStruct((nb, N_HEADS, t, KV_LORA), BF16),
            jax.ShapeDtypeStruct((nb, N_HEADS, t, ROPE_W), BF16),
            jax.ShapeDtypeStruct((m, KV_LORA), F32),
            jax.ShapeDtypeStruct((m, ROPE_DIM), F32),
            jax.ShapeDtypeStruct((m, dc), BF16),
            jax.ShapeDtypeStruct((nb, CONV_W - 1, dc), F32),
            jax.ShapeDtypeStruct((d, IN_A_W), BF16),
            jax.ShapeDtypeStruct((d, 3 * dc), BF16),
            jax.ShapeDtypeStruct((KV_LORA, N_HEADS * NOPE_DIM), BF16),
            jax.ShapeDtypeStruct((KV_LORA, N_HEADS * V_DIM), BF16),
            jax.ShapeDtypeStruct((N_HEADS, NOPE_DIM, KV_LORA), BF16),
            jax.ShapeDtypeStruct((N_HEADS, KV_LORA, V_DIM), BF16),
            jax.ShapeDtypeStruct((Q_LORA, N_HEADS * HEAD_W), BF16)]
    out_specs = [const(o) for o in outs[:7]]
    out_specs.append(pl.BlockSpec((d, dc), lambda s: (0, jnp.maximum(s - 1, 0))))
    out_specs += [const(o) for o in outs[8:]]
    return pl.pallas_call(
        functools.partial(_mixer_in_sample_kernel, nb=nb, t=t, layer=layer), grid=(steps,),
        in_specs=specs, out_specs=out_specs, out_shape=outs,
        scratch_shapes=[pltpu.VMEM((2, IN_A_W, d), F32), pltpu.SemaphoreType.DMA((2,)),
                        pltpu.VMEM((m, d), BF16), pltpu.VMEM((3, m, dc), F32),
                        pltpu.VMEM((nb, CONV_PAD + t, dc), F32)],
        compiler_params=_cparams(1), name="mixer_in_sample",
    )(*ins)


def _prompt_attn_kernel(q_ref, k_ref, v_ref, o_ref, s_buf, p_buf, *, seq, tq, heads):
    row_chunk = lax.broadcasted_iota(jnp.int32, (tq, tq), 0) // CHUNK
    col_chunk = lax.broadcasted_iota(jnp.int32, (tq, tq), 1) // CHUNK
    visible = col_chunk <= row_chunk
    tiles = [(hd, qi) for hd in range(heads) for qi in range(seq // tq)]

    def scores(i):
        hd, qi = tiles[i]
        hi = (qi + 1) * tq
        cols = slice(hd * HEAD_W, (hd + 1) * HEAD_W)
        s_buf[i % 2, :, 0:hi] = _dot_t(q_ref[qi * tq:hi, cols], k_ref[0:hi, cols])

    scores(0)
    for i, (hd, qi) in enumerate(tiles):
        lo, hi = qi * tq, (qi + 1) * tq
        if i + 1 < len(tiles):
            scores(i + 1)
        s_d = jnp.where(visible, s_buf[i % 2, :, lo:hi], NEG_BIG)
        m = jnp.max(s_d, axis=-1, keepdims=True)
        if qi > 0:
            m = jnp.maximum(m, jnp.max(s_buf[i % 2, :, 0:lo], axis=-1, keepdims=True))
            p_buf[:, 0:lo] = jnp.exp2(s_buf[i % 2, :, 0:lo] - m).astype(BF16)
        p_buf[:, lo:hi] = jnp.exp2(s_d - m).astype(BF16)
        oe = _dot(p_buf[:, 0:hi], v_ref[0:hi, hd * HEAD_W:(hd + 1) * HEAD_W])
        o_ref[lo:hi, hd * V_DIM:(hd + 1) * V_DIM] = oe[:, :V_DIM] / oe[:, V_DIM:V_DIM + 1]


def _prompt_attention(q, k, v, nbatch, seq, tq=256, heads=4):
    m = q.shape[0]
    qkv_spec = pl.BlockSpec((seq, heads * HEAD_W), lambda b, h: (b, h))
    return pl.pallas_call(
        functools.partial(_prompt_attn_kernel, seq=seq, tq=tq, heads=heads),
        grid=(nbatch, N_HEADS // heads),
        in_specs=[qkv_spec, qkv_spec, qkv_spec],
        out_specs=pl.BlockSpec((seq, heads * V_DIM), lambda b, h: (b, h)),
        out_shape=jax.ShapeDtypeStruct((m, D_ATTN), F32),
        scratch_shapes=[pltpu.VMEM((2, tq, seq), F32), pltpu.VMEM((tq, seq), BF16)],
        compiler_params=_cparams(2), name="prompt_attention",
    )(q, k, v)


def _sample_attn_kernel(qa_ref, qpe_ref, pc_ref, pk_ref, nc_ref, nk_ref, wuv_ref, o_ref, *, t):
    rows = N_HEADS * t
    qa = qa_ref[0].reshape(rows, KV_LORA)
    qp = qpe_ref[0].reshape(rows, ROPE_W)[:, :ROPE_DIM]
    pc = pc_ref[0].astype(BF16)
    pk_t = pk_ref[0].astype(BF16)
    nc = nc_ref[...].astype(BF16)
    nk = nk_ref[...].astype(BF16)
    s_p = _dot_t(qa, pc) + _dot(qp, pk_t)
    s_n = _dot_t(qa, nc) + _dot_t(qp, nk)
    m = jnp.maximum(jnp.max(s_p, axis=-1, keepdims=True), jnp.max(s_n, axis=-1, keepdims=True))
    p_p = jnp.exp2(s_p - m)
    p_n = jnp.exp2(s_n - m)
    l = jnp.sum(p_p, axis=-1, keepdims=True) + jnp.sum(p_n, axis=-1, keepdims=True)
    o_lat = (_dot(p_p.astype(BF16), pc) + _dot(p_n.astype(BF16), nc)) / l
    o_lat = o_lat.astype(BF16)
    for hd in range(N_HEADS):
        o_ref[:, hd * V_DIM:(hd + 1) * V_DIM] = _dot(o_lat[hd * t:(hd + 1) * t, :], wuv_ref[hd])


def _sample_attention(qa, qpe, past_ckv, past_kpe_t, new_ckv, new_kpe, w_uv_h, layer):
    nb, _, t, _ = qa.shape
    past = past_ckv.shape[2]
    return pl.pallas_call(
        functools.partial(_sample_attn_kernel, t=t),
        grid=(nb,),
        in_specs=[pl.BlockSpec((1, N_HEADS, t, KV_LORA), lambda b: (b, 0, 0, 0)),
                  pl.BlockSpec((1, N_HEADS, t, ROPE_W), lambda b: (b, 0, 0, 0)),
                  pl.BlockSpec((None, 1, past, KV_LORA), lambda b: (layer, b, 0, 0)),
                  pl.BlockSpec((None, 1, ROPE_DIM, past), lambda b: (layer, b, 0, 0)),
                  pl.BlockSpec((t, KV_LORA), lambda b: (b, 0)),
                  pl.BlockSpec((t, ROPE_DIM), lambda b: (b, 0)),
                  pl.BlockSpec(w_uv_h.shape, lambda b: (0, 0, 0))],
        out_specs=pl.BlockSpec((t, D_ATTN), lambda b: (b, 0)),
        out_shape=jax.ShapeDtypeStruct((nb * t, D_ATTN), F32),
        compiler_params=_cparams(1), name="sample_attention",
    )(qa, qpe, past_ckv, past_kpe_t, new_ckv, new_kpe, w_uv_h)


def _out_proj_kernel(x_ref, attn_ref, conv_ref, g_ref, shm_ref, scm_ref, an_ref, woa_ref, woc_ref,
                     o_ref, h_ref, *bf16_w_out, groups):
    if bf16_w_out:
        wo_ref, = bf16_w_out
        da = woa_ref.shape[0]
        wo_ref[0:da, :] = woa_ref[...].astype(BF16)
        wo_ref[da:, :] = woc_ref[...].astype(BF16)
        woa, woc = wo_ref[0:da, :], wo_ref[da:, :]
    else:
        woa, woc = woa_ref[...], woc_ref[...]
    starts = [sum(groups[:j]) for j in range(len(groups))]
    for r0, rows in zip(starts, groups):
        rs = slice(r0, r0 + rows) if len(groups) > 1 else slice(None)
        a = (_rms(attn_ref[rs, :]) * an_ref[...]).astype(BF16)
        mix = _dot(a, woa) + _dot(conv_ref[rs, :], woc)
        y = x_ref[rs, :] + _mod_rows(g_ref, rows) * mix
        o_ref[rs, :] = y
        h_ref[rs, :] = (_rms(y) * (1.0 + _mod_rows(scm_ref, rows))
                        + _mod_rows(shm_ref, rows)).astype(BF16)


def _out_proj(x, attn, conv_n, mod, w, w_out, layer, nbatch, seq, tm, groups=None):
    groups = (tm,) if groups is None else tuple(groups)
    assert sum(groups) == tm
    m, d = x.shape
    nt = seq // tm
    da, dc = attn.shape[1], conv_n.shape[1]
    ins = [x, attn, conv_n, mod.arr, mod.arr, mod.arr, w["attn_out_norm"], w_out, w_out]
    specs = [_rows(tm, d, nt), _rows(tm, da, nt), _rows(tm, dc, nt),
             mod.spec(2), mod.spec(3), mod.spec(4),
             _layer_spec(w["attn_out_norm"], mod.layer)]
    out_specs = [_rows(tm, d, nt), _rows(tm, d, nt)]
    out_shape = [jax.ShapeDtypeStruct((m, d), F32), jax.ShapeDtypeStruct((m, d), BF16)]
    if layer is None:
        specs += [pl.BlockSpec((da, d), lambda b, s: (0, 0), pipeline_mode=pl.Buffered(1)),
                  pl.BlockSpec((dc, d), lambda b, s: (da // dc, 0), pipeline_mode=pl.Buffered(1))]
    else:
        assert nbatch * nt == 1
        specs += [_layer_spec(w_out, layer, (da, d), (0, 0)),
                  _layer_spec(w_out, layer, (dc, d), (da // dc, 0))]
        out_specs.append(pl.BlockSpec((da + dc, d), lambda b, s: (0, 0)))
        out_shape.append(jax.ShapeDtypeStruct((da + dc, d), BF16))
    return pl.pallas_call(
        functools.partial(_out_proj_kernel, groups=groups), grid=(nbatch, nt), in_specs=specs,
        out_specs=out_specs, out_shape=out_shape,
        compiler_params=_cparams(2), name="out_proj",
    )(*ins)


def _mlp_kernel(x_ref, h_ref, g_ref, wup_ref, wdn_ref, fn_ref, o_ref, *extra, nf, final, cast,
                x_rows):
    f = pl.program_id(2)
    if cast:
        wu_ref, wd_ref = extra
        wu_ref[...] = wup_ref[...].astype(BF16)
        wd_ref[...] = wdn_ref[...].astype(BF16)
    else:
        wu_ref, wd_ref = wup_ref, wdn_ref
    if x_rows:
        x_buf, x_sem = extra
        tile = pl.program_id(0) * pl.num_programs(1) + pl.program_id(1)
        row0 = pl.multiple_of(tile * x_rows, x_rows)
        x_copy = pltpu.make_async_copy(x_ref.at[pl.ds(row0, x_rows), :], x_buf, x_sem)

        @pl.when(f == 0)
        def _():
            x_copy.start()

    def partial_down():
        up = _dot(h_ref[...], wu_ref[...])
        act = jnp.square(jnp.maximum(up, 0.0)).astype(BF16)
        return _dot(act, wd_ref[...])

    @pl.when(f == 0)
    def _():
        o_ref[...] = partial_down()

    @pl.when(f > 0)
    def _():
        o_ref[...] += partial_down()

    @pl.when(f == nf - 1)
    def _():
        if x_rows:
            x_copy.wait()
            x = x_buf[...]
        else:
            x = x_ref[...]
        y = x + _mod_rows(g_ref, x.shape[0]) * o_ref[...]
        if final:
            y = _rms(y) * fn_ref[...]
        o_ref[...] = y


def _mlp(x, h, mod, w_up, w_down, layer, final_norm, nbatch, seq, tm, tf, final):
    m, d = x.shape
    nt = seq // tm
    dff = w_up.shape[-1]
    nf = dff // tf
    cast = layer is not None
    out_specs = [_rows(tm, d, nt)]
    out_shape = [jax.ShapeDtypeStruct((m, d), F32)]
    scratch = []
    if cast:
        assert nbatch * nt == 1
        x_spec = _rows(tm, d, nt)
        w_specs = [pl.BlockSpec((None, d, tf), lambda b, s, f: (layer, 0, f)),
                   pl.BlockSpec((None, tf, d), lambda b, s, f: (layer, f, 0))]
        out_specs += [pl.BlockSpec((d, tf), lambda b, s, f: (0, f)),
                      pl.BlockSpec((tf, d), lambda b, s, f: (f, 0))]
        out_shape += [jax.ShapeDtypeStruct((d, dff), BF16), jax.ShapeDtypeStruct((dff, d), BF16)]
    else:
        x_spec = pl.BlockSpec(memory_space=pl.ANY)
        w_specs = [pl.BlockSpec((d, tf), lambda b, s, f: (0, f)),
                   pl.BlockSpec((tf, d), lambda b, s, f: (f, 0))]
        scratch = [pltpu.VMEM((tm, d), F32), pltpu.SemaphoreType.DMA(())]
    return pl.pallas_call(
        functools.partial(_mlp_kernel, nf=nf, final=final, cast=cast,
                          x_rows=0 if cast else tm),
        grid=(nbatch, nt, nf),
        in_specs=[x_spec, _rows(tm, d, nt), mod.spec(5), *w_specs,
                  pl.BlockSpec(final_norm.shape, lambda b, s, f: (0, 0))],
        out_specs=out_specs, out_shape=out_shape, scratch_shapes=scratch,
        compiler_params=_cparams(3), name="mlp",
    )(x, h, mod.arr, w_up, w_down, final_norm)


def _rope_tables(pos):
    half = ROPE_DIM // 2
    inv = ROPE_BASE ** (-jnp.arange(half, dtype=F32) / half)
    ang = pos.astype(F32)[:, None] * inv[None, :]
    cos, sin, zero = jnp.cos(ang), jnp.sin(ang), jnp.zeros_like(ang)
    return (jnp.concatenate([cos, cos, zero, zero], axis=-1),
            jnp.concatenate([-sin, zero, zero, zero], axis=-1),
            jnp.concatenate([zero, sin, zero, zero], axis=-1))


def _prep_weights(q_norm, w_uq, kv_norm, w_ukv, conv_w, attn_out_norm, conv_out_norm,
                  w_out, w_up, w_down):
    return dict(
        w_uq=w_uq, w_ukv=w_ukv,
        q_norm=q_norm[:, None, :], kv_norm=kv_norm[:, None, :],
        conv_w=conv_w, attn_out_norm=attn_out_norm[:, None, :],
        conv_out_norm=conv_out_norm[:, None, :],
        w_out=w_out, w_up=w_up, w_down=w_down)


def kernel(x_prompt, x_sample, c_prompt, c_sample, cache_kv_latent, cache_k_rope, state_conv,
           w_ada, b_ada, w_in, q_norm, w_uq, kv_norm, w_ukv, conv_w,
           attn_out_norm, conv_out_norm, w_out, w_up, w_down, final_norm):
    bp, sp, d = x_prompt.shape
    bs, ss, _ = x_sample.shape
    depth = w_ada.shape[0]
    past = cache_kv_latent.shape[2]
    dc = conv_w.shape[-1]
    ms = bs * ss

    w = _prep_weights(q_norm, w_uq, kv_norm, w_ukv, conv_w, attn_out_norm,
                      conv_out_norm, w_out, w_up, w_down)
    w_in_t = jnp.swapaxes(w_in, 1, 2)
    fnorm = final_norm[None, :]
    tabs_p = _rope_tables(jnp.arange(sp))
    tabs_s = tuple(jnp.tile(tb, (bs, 1)) for tb in _rope_tables(past + jnp.arange(ss)))

    mod = _modulation(jnp.concatenate([c_prompt, c_sample], axis=0), w_ada, b_ada)
    mod = mod.reshape(depth, bp + bs, N_MOD, d)
    mod_p = mod[:, :bp].reshape(depth * bp * N_MOD, 1, d)
    mod_s = jnp.swapaxes(mod[:, bp:], 1, 2).reshape(depth * N_MOD, bs, d)
    zero_conv = jnp.zeros((1, bp, CONV_W - 1, dc), F32)

    xp = x_prompt.reshape(bp * sp, d)
    xs = x_sample.reshape(ms, d)
    past_kpe_t = jnp.swapaxes(cache_k_rope, 2, 3)
    stacks = None
    outs = [[] for _ in range(4)]
    for l in range(depth):
        last = l == depth - 1
        mp = _Mod(mod_p, l, bp)
        msm = _Mod(mod_s, l, 0)

        (qa, qpe, ckv_s, kpe_s, conv_ns, cstate_s, w_a_l, w_b_l, w_uk_l, w_uv_l, _,
         w_uv_h_l, w_uq_l) = _mixer_in_sample(xs, msm, w, w_in_t, l, tabs_s, state_conv, bs, ss)
        attn_s = _sample_attention(qa, qpe, cache_kv_latent, past_kpe_t, ckv_s, kpe_s,
                                   w_uv_h_l, l)
        xs, hs, w_out_l = _out_proj(xs, attn_s, conv_ns, msm, w, w["w_out"], l, 1, ms, tm=ms)
        xs, w_up_l, w_down_l = _mlp(xs, hs, msm, w["w_up"], w["w_down"], l, fnorm, 1, ms,
                                    tm=ms, tf=1024, final=last)
        outs[1].append(ckv_s.reshape(bs, ss, KV_LORA))
        outs[2].append(kpe_s.reshape(bs, ss, ROPE_DIM))
        outs[3].append(cstate_s)

        q, k, v, *stacks, conv_n, cstate = _mixer_in_prompt(
            xp, mp, w, w_a_l, w_b_l, w_uk_l, w_uv_l, w_uq_l, l, tabs_p, zero_conv, bp, sp,
            stacks, tm=256, nsub=1)
        attn = _prompt_attention(q, k, v, bp, sp)
        xp, hp = _out_proj(xp, attn, conv_n, mp, w, w_out_l, None, bp, sp, tm=512, groups=(384, 128))
        xp, = _mlp(xp, hp, mp, w_up_l, w_down_l, None, fnorm, bp, sp, tm=1024, tf=1024, final=last)
        outs[0].append(cstate)

    return (xp.reshape(bp, sp, d), xs.reshape(bs, ss, d),
            stacks[0].reshape(depth, bp, sp, KV_LORA), jnp.swapaxes(stacks[1], 2, 3),
            jnp.stack(outs[0]), jnp.stack(outs[1]), jnp.stack(outs[2]), jnp.stack(outs[3]))
```

```python
import functools
import math

import jax
import jax.numpy as jnp
from jax import lax
from jax.experimental import pallas as pl
from jax.experimental.pallas import tpu as pltpu

F32 = jnp.float32
BF16 = jnp.bfloat16

CHUNK = 64
N_HEADS = 8
NOPE_DIM = 128
ROPE_DIM = 64
V_DIM = 128
QK_DIM = NOPE_DIM + ROPE_DIM
Q_LORA = 512
KV_LORA = 512
D_ATTN = N_HEADS * V_DIM
CONV_W = 3
N_MOD = 6
ROPE_BASE = 10000.0
EPS = 1e-6
ATTN_SCALE = QK_DIM ** -0.5

LANES = 128
HEAD_W = 2 * LANES
ROPE_W = LANES
IN_A_W = Q_LORA + KV_LORA + ROPE_W
CONV_PAD = 8
Q_SCALE = ATTN_SCALE * math.log2(math.e)
NEG_BIG = -1e30
VMEM_LIMIT = 60000 * 1024


def _cparams(n_axes):
    return pltpu.CompilerParams(dimension_semantics=("arbitrary",) * n_axes,
                                vmem_limit_bytes=VMEM_LIMIT)


def _rms(x):
    return x * lax.rsqrt(jnp.mean(x * x, axis=-1, keepdims=True) + EPS)


def _dot(a, b):
    return jnp.dot(a, b, preferred_element_type=F32)


def _dot_t(a, b):
    return lax.dot_general(a, b, (((1,), (1,)), ((), ())), preferred_element_type=F32)


def _layer_spec(arr, layer, block=None, index=None):
    block = tuple(arr.shape[1:]) if block is None else tuple(block)
    index = (0,) * len(block) if index is None else tuple(index)
    return pl.BlockSpec((None,) + block, lambda *g: (layer,) + index,
                        pipeline_mode=pl.Buffered(1))


def _rows(tm, cols, tiles_per_batch):
    return pl.BlockSpec((tm, cols), lambda b, s, *_: (b * tiles_per_batch + s, 0))


class _Mod:
    def __init__(self, arr, layer, per_batch):
        self.arr, self.layer, self.per_batch = arr, layer, per_batch

    def spec(self, role):
        layer, nb = self.layer, self.per_batch
        if nb:
            return pl.BlockSpec((None,) + self.arr.shape[1:],
                                lambda b, *_: ((layer * nb + b) * N_MOD + role, 0, 0))
        return pl.BlockSpec((None,) + self.arr.shape[1:],
                            lambda *_: (layer * N_MOD + role, 0, 0))


def _mod_kernel(c_ref, w_ref, b_ref, o_ref):
    c = c_ref[...]
    s = (c * jax.nn.sigmoid(c)).astype(BF16)
    o_ref[0] = _dot(s, w_ref[0].astype(BF16)) + b_ref[0]


def _modulation(c_all, w_ada, b_ada, tn=1024):
    depth, d, n = w_ada.shape
    r = c_all.shape[0]
    return pl.pallas_call(
        _mod_kernel,
        grid=(depth, n // tn),
        in_specs=[pl.BlockSpec((r, d), lambda l, j: (0, 0)),
                  pl.BlockSpec((1, d, tn), lambda l, j: (l, 0, j)),
                  pl.BlockSpec((1, 1, tn), lambda l, j: (l, 0, j))],
        out_specs=pl.BlockSpec((1, r, tn), lambda l, j: (l, 0, j)),
        out_shape=jax.ShapeDtypeStruct((depth, r, n), F32),
        compiler_params=_cparams(2),
        name="modulation",
    )(c_all, w_ada, b_ada.reshape(depth, 1, n))


def _rope128(p, cos, s1, s2):
    return p * cos + pltpu.roll(p, 96, 1) * s1 + pltpu.roll(p, 32, 1) * s2


def _mod_rows(m_ref, rows):
    m = m_ref[...]
    nb, d = m.shape
    if nb in (1, rows):
        return m
    return jnp.broadcast_to(m[:, None, :], (nb, rows // nb, d)).reshape(rows, d)


def _adaln(x_ref, rs, sh_ref, sc_ref):
    x = x_ref[rs, :]
    n = x.shape[0]
    return (_rms(x) * (1.0 + _mod_rows(sc_ref, n)) + _mod_rows(sh_ref, n)).astype(BF16)


def _attn_in_common(h, rs, win_ref, qn_ref, wuq_ref, kvn_ref, cos_ref, s1_ref, s2_ref):
    z = _dot(h, win_ref[...])
    cos, s1, s2 = cos_ref[rs, :], s1_ref[rs, :], s2_ref[rs, :]
    qn = (_rms(z[:, :Q_LORA]) * qn_ref[...]).astype(BF16)
    q = _dot(qn, wuq_ref[...]) * Q_SCALE
    ckv = _rms(z[:, Q_LORA:Q_LORA + KV_LORA]) * kvn_ref[...]
    kp = _rope128(z[:, Q_LORA + KV_LORA:], cos, s1, s2)
    rope = functools.partial(_rope128, cos=cos, s1=s1, s2=s2)
    return q, ckv, kp, rope


def _conv_group(h, p0, tg, nb, w_ref, cw_ref, cn_ref, buf):
    dc = cw_ref.shape[-1]
    z = _dot(h, w_ref[...])
    return _conv_post(z[:, :dc], z[:, dc:2 * dc], z[:, 2 * dc:], p0, tg, nb, cw_ref, cn_ref, buf)


def _conv_post(gate_b, gate_c, u, p0, tg, nb, cw_ref, cn_ref, buf):
    dc = cw_ref.shape[-1]
    cu = gate_c * u
    buf[:, CONV_PAD + p0:CONV_PAD + p0 + tg, :] = cu.reshape(nb, tg, dc)
    base = CONV_PAD - (CONV_W - 1) + p0
    y = cw_ref[0:1, :].reshape(1, 1, dc) * buf[:, base:base + tg, :]
    for k in range(1, CONV_W):
        y = y + cw_ref[k:k + 1, :].reshape(1, 1, dc) * buf[:, base + k:base + k + tg, :]
    conv_o = gate_b * y.reshape(nb * tg, dc)
    return (_rms(conv_o) * cn_ref[...]).astype(BF16)


def _conv_carry(buf, st_ref, t):
    last = buf[:, CONV_PAD + t - 2:CONV_PAD + t, :]
    st_ref[...] = last
    buf[:, CONV_PAD - 2:CONV_PAD, :] = last


def _mixer_in_prompt_kernel(x_ref, sh_ref, sc_ref, win_ref, qn_ref, wuq_ref, kvn_ref,
                            cos_ref, s1_ref, s2_ref, wuk_ref, wuv_ref,
                            winb_ref, cw_ref, cn_ref, prev_ref, *rest, nsub, first):
    q_out, k_out, v_out, ckv_out, kpe_out, conv_out, st_ref, buf = rest[-8:]

    @pl.when(pl.program_id(1) == 0)
    def _():
        buf[:, CONV_PAD - 2:CONV_PAD, :] = prev_ref[...]

    if first:
        ckv_out[1:] = jnp.zeros((ckv_out.shape[0] - 1,) + ckv_out.shape[1:], F32)
        kpe_out[1:] = jnp.zeros((kpe_out.shape[0] - 1,) + kpe_out.shape[1:], F32)
        ckv_out, kpe_out = ckv_out.at[0], kpe_out.at[0]
    rows = x_ref.shape[0] // nsub
    lane = lax.broadcasted_iota(jnp.int32, (rows, LANES), 1)
    ones_col = jnp.where(lane == 0, 1.0, 0.0).astype(BF16)
    for j in range(nsub):
        rs = slice(j * rows, (j + 1) * rows)
        h = _adaln(x_ref, rs, sh_ref, sc_ref)
        conv_out[rs, :] = _conv_group(h, j * rows, rows, 1, winb_ref, cw_ref, cn_ref, buf)
        q, ckv, kp, rope = _attn_in_common(h, rs, win_ref, qn_ref, wuq_ref, kvn_ref,
                                           cos_ref, s1_ref, s2_ref)
        ckv_out[rs, :] = ckv
        kpe_out[:, rs] = kp.T[:ROPE_DIM, :]
        ckb = ckv.astype(BF16)
        kn = _dot(ckb, wuk_ref[...])
        v = _dot(ckb, wuv_ref[...]).astype(BF16)
        kpb = kp.astype(BF16)
        for hd in range(N_HEADS):
            c0 = hd * HEAD_W
            v_out[rs, c0:c0 + V_DIM] = v[:, hd * V_DIM:(hd + 1) * V_DIM]
            v_out[rs, c0 + V_DIM:c0 + HEAD_W] = ones_col
            q_out[rs, c0:c0 + NOPE_DIM] = q[:, c0:c0 + NOPE_DIM].astype(BF16)
            q_out[rs, c0 + NOPE_DIM:c0 + HEAD_W] = rope(q[:, c0 + NOPE_DIM:c0 + HEAD_W]).astype(BF16)
            k_out[rs, c0:c0 + NOPE_DIM] = kn[:, hd * NOPE_DIM:(hd + 1) * NOPE_DIM].astype(BF16)
            k_out[rs, c0 + NOPE_DIM:c0 + HEAD_W] = kpb
    _conv_carry(buf, st_ref, x_ref.shape[0])


def _mixer_in_sample_kernel(x_ref, sh_ref, sc_ref, wint_hbm, qn_ref, wuq_ref, kvn_ref,
                            cos_ref, s1_ref, s2_ref, wukv_ref, cw_ref, cn_ref, prev_ref,
                            qa_out, qpe_out, ckv_out, kpe_out, conv_out, st_ref, wa_out, wb_out,
                            wuk_out, wuv_out, wukt_out, wuvh_out, wuq_out,
                            w_buf, w_sem, h_scr, z_scr, buf, *, nb, t, layer):
    s = pl.program_id(0)
    dc = cw_ref.shape[-1]
    n_a = Q_LORA + KV_LORA + ROPE_DIM
    chunk_rows = [(0, IN_A_W)] + [(n_a + j * dc, dc) for j in range(3)]

    def chunk_copy(k):
        r0, rows = chunk_rows[k]
        return pltpu.make_async_copy(wint_hbm.at[layer, pl.ds(r0, rows), :],
                                     w_buf.at[k % 2, pl.ds(0, rows), :], w_sem.at[k % 2])

    for k in range(len(chunk_rows)):
        @pl.when(s == k)
        def _(k=k):
            if k == 0:
                chunk_copy(0).start()
                h_scr[...] = _adaln(x_ref, slice(None), sh_ref, sc_ref)
                buf[:, CONV_PAD - 2:CONV_PAD, :] = prev_ref[...]
            if k + 1 < len(chunk_rows):
                chunk_copy(k + 1).start()
            chunk_copy(k).wait()
            rows = chunk_rows[k][1]
            w_t = w_buf[k % 2, 0:rows, :].T.astype(BF16)
            if k == 0:
                wa_out[...] = w_t
                for hd in range(N_HEADS):
                    c0 = hd * HEAD_W
                    wuq_out[:, c0:c0 + QK_DIM] = wuq_ref[:, hd * QK_DIM:(hd + 1) * QK_DIM].astype(BF16)
                    wuq_out[:, c0 + QK_DIM:c0 + HEAD_W] = jnp.zeros(
                        (wuq_ref.shape[0], HEAD_W - QK_DIM), BF16)
                q, ckv, kp, rope = _attn_in_common(h_scr[...], slice(None), wa_out, qn_ref,
                                                   wuq_out, kvn_ref, cos_ref, s1_ref, s2_ref)
                ckv_out[...] = ckv
                kpe_out[...] = kp[:, :ROPE_DIM]
                for hd in range(N_HEADS):
                    k0 = hd * (NOPE_DIM + V_DIM)
                    w_k = wukv_ref[:, k0:k0 + NOPE_DIM]
                    w_v = wukv_ref[:, k0 + NOPE_DIM:k0 + NOPE_DIM + V_DIM].astype(BF16)
                    wuk_out[:, hd * NOPE_DIM:(hd + 1) * NOPE_DIM] = w_k.astype(BF16)
                    wukt_out[hd] = w_k.T.astype(BF16)
                    wuv_out[:, hd * V_DIM:(hd + 1) * V_DIM] = w_v
                    wuvh_out[hd] = w_v
                for hd in range(N_HEADS):
                    c0 = hd * HEAD_W
                    qa = _dot(q[:, c0:c0 + NOPE_DIM].astype(BF16), wukt_out[hd])
                    qa_out[:, hd] = qa.reshape(nb, t, KV_LORA).astype(BF16)
                    qp = rope(q[:, c0 + NOPE_DIM:c0 + HEAD_W])
                    qpe_out[:, hd] = qp.reshape(nb, t, ROPE_W).astype(BF16)
            else:
                wb_out[...] = w_t
                z_scr[k - 1] = _dot(h_scr[...], wb_out[...])
            if k == len(chunk_rows) - 1:
                conv_out[...] = _conv_post(z_scr[0], z_scr[1], z_scr[2], 0, t, nb,
                                           cw_ref, cn_ref, buf)
                _conv_carry(buf, st_ref, t)


def _mixer_in_prompt(x, mod, w, w_a, w_b, w_uk, w_uv, w_uq, layer, tabs, prev, nbatch, seq,
                     stacks, tm, nsub):
    m, d = x.shape
    depth = w["q_norm"].shape[0]
    dc = w["conv_w"].shape[-1]
    nt = seq // tm
    tab_spec = pl.BlockSpec((tm, ROPE_W), lambda b, s: (s, 0))
    whole = lambda a: pl.BlockSpec(a.shape, lambda b, s: (0,) * a.ndim,
                                   pipeline_mode=pl.Buffered(1))
    ins = [x, mod.arr, mod.arr, w_a, w["q_norm"], w_uq, w["kv_norm"],
           *tabs, w_uk, w_uv, w_b, w["conv_w"], w["conv_out_norm"], prev]
    specs = [_rows(tm, d, nt), mod.spec(0), mod.spec(1), whole(w_a),
             _layer_spec(w["q_norm"], layer), whole(w_uq),
             _layer_spec(w["kv_norm"], layer), tab_spec, tab_spec, tab_spec,
             whole(w_uk), whole(w_uv),
             whole(w_b), _layer_spec(w["conv_w"], layer),
             _layer_spec(w["conv_out_norm"], layer),
             pl.BlockSpec((None, 1, CONV_W - 1, dc), lambda b, s: (0, b, 0, 0))]
    aliases = {}
    if stacks is not None:
        aliases = {len(ins): 3, len(ins) + 1: 4}
        ins += list(stacks)
        specs += [pl.BlockSpec(memory_space=pl.ANY)] * 2
    outs = [jax.ShapeDtypeStruct((m, N_HEADS * HEAD_W), BF16),
            jax.ShapeDtypeStruct((m, N_HEADS * HEAD_W), BF16),
            jax.ShapeDtypeStruct((m, N_HEADS * HEAD_W), BF16),
            jax.ShapeDtypeStruct((depth, m, KV_LORA), F32),
            jax.ShapeDtypeStruct((depth, nbatch, ROPE_DIM, seq), F32),
            jax.ShapeDtypeStruct((m, dc), BF16),
            jax.ShapeDtypeStruct((nbatch, CONV_W - 1, dc), F32)]
    first = stacks is None
    lsel = (depth, 0) if first else (None, layer)
    out_specs = [_rows(tm, N_HEADS * HEAD_W, nt), _rows(tm, N_HEADS * HEAD_W, nt),
                 _rows(tm, N_HEADS * HEAD_W, nt),
                 pl.BlockSpec((lsel[0], tm, KV_LORA), lambda b, s: (lsel[1], b * nt + s, 0)),
                 pl.BlockSpec((lsel[0], None, ROPE_DIM, tm), lambda b, s: (lsel[1], b, 0, s)),
                 _rows(tm, dc, nt),
                 pl.BlockSpec((1, CONV_W - 1, dc), lambda b, s: (b, 0, 0))]
    return pl.pallas_call(
        functools.partial(_mixer_in_prompt_kernel, nsub=nsub, first=first), grid=(nbatch, nt),
        in_specs=specs, out_specs=out_specs, out_shape=outs, input_output_aliases=aliases,
        scratch_shapes=[pltpu.VMEM((1, CONV_PAD + tm, dc), F32)],
        compiler_params=_cparams(2), name="mixer_in_prompt",
    )(*ins)


def _mixer_in_sample(x, mod, w, w_in_t, layer, tabs, prev, nb, t):
    m, d = x.shape
    dc = w["conv_w"].shape[-1]
    steps = 4
    assert w_in_t.shape[1] == Q_LORA + KV_LORA + ROPE_DIM + 3 * dc
    const = lambda a: pl.BlockSpec(a.shape, lambda s: (0,) * a.ndim)
    ins = [x, mod.arr, mod.arr, w_in_t, w["q_norm"], w["w_uq"], w["kv_norm"],
           *tabs, w["w_ukv"], w["conv_w"], w["conv_out_norm"], prev]
    specs = [const(x), mod.spec(0), mod.spec(1), pl.BlockSpec(memory_space=pl.ANY),
             _layer_spec(w["q_norm"], layer), _layer_spec(w["w_uq"], layer),
             _layer_spec(w["kv_norm"], layer), const(tabs[0]), const(tabs[1]), const(tabs[2]),
             _layer_spec(w["w_ukv"], layer), _layer_spec(w["conv_w"], layer),
             _layer_spec(w["conv_out_norm"], layer),
             pl.BlockSpec((None, nb, CONV_W - 1, dc), lambda s: (layer, 0, 0, 0))]
    outs = [jax.ShapeDtypeStruct((nb, N_HEADS, t, KV_LORA), BF16),
            jax.ShapeDtypeStruct((nb, N_HEADS, t, ROPE_W), BF16),
            jax.ShapeDtypeStruct((m, KV_LORA), F32),
            jax.ShapeDtypeStruct((m, ROPE_DIM), F32),
            jax.ShapeDtypeStruct((m, dc), BF16),
            jax.ShapeDtypeStruct((nb, CONV_W - 1, dc), F32),
            jax.ShapeDtypeStruct((d, IN_A_W), BF16),
            jax.ShapeDtypeStruct((d, 3 * dc), BF16),
            jax.ShapeDtypeStruct((KV_LORA, N_HEADS * NOPE_DIM), BF16),
            jax.ShapeDtypeStruct((KV_LORA, N_HEADS * V_DIM), BF16),
            jax.ShapeDtypeStruct((N_HEADS, NOPE_DIM, KV_LORA), BF16),
            jax.ShapeDtypeStruct((N_HEADS, KV_LORA, V_DIM), BF16),
            jax.ShapeDtypeStruct((Q_LORA, N_HEADS * HEAD_W), BF16)]
    out_specs = [const(o) for o in outs[:7]]
    out_specs.append(pl.BlockSpec((d, dc), lambda s: (0, jnp.maximum(s - 1, 0))))
    out_specs += [const(o) for o in outs[8:]]
    return pl.pallas_call(
        functools.partial(_mixer_in_sample_kernel, nb=nb, t=t, layer=layer), grid=(steps,),
        in_specs=specs, out_specs=out_specs, out_shape=outs,
        scratch_shapes=[pltpu.VMEM((2, IN_A_W, d), F32), pltpu.SemaphoreType.DMA((2,)),
                        pltpu.VMEM((m, d), BF16), pltpu.VMEM((3, m, dc), F32),
                        pltpu.VMEM((nb, CONV_PAD + t, dc), F32)],
        compiler_params=_cparams(1), name="mixer_in_sample",
    )(*ins)


def _prompt_attn_kernel(q_ref, k_ref, v_ref, o_ref, s_buf, p_buf, *, seq, tq, heads):
    row_chunk = lax.broadcasted_iota(jnp.int32, (tq, tq), 0) // CHUNK
    col_chunk = lax.broadcasted_iota(jnp.int32, (tq, tq), 1) // CHUNK
    visible = col_chunk <= row_chunk
    tiles = [(hd, qi) for hd in range(heads) for qi in range(seq // tq)]

    def scores(i):
        hd, qi = tiles[i]
        hi = (qi + 1) * tq
        cols = slice(hd * HEAD_W, (hd + 1) * HEAD_W)
        s_buf[i % 2, :, 0:hi] = _dot_t(q_ref[qi * tq:hi, cols], k_ref[0:hi, cols])

    scores(0)
    for i, (hd, qi) in enumerate(tiles):
        lo, hi = qi * tq, (qi + 1) * tq
        if i + 1 < len(tiles):
            scores(i + 1)
        s_d = jnp.where(visible, s_buf[i % 2, :, lo:hi], NEG_BIG)
        m = jnp.max(s_d, axis=-1, keepdims=True)
        if qi > 0:
            m = jnp.maximum(m, jnp.max(s_buf[i % 2, :, 0:lo], axis=-1, keepdims=True))
            p_buf[:, 0:lo] = jnp.exp2(s_buf[i % 2, :, 0:lo] - m).astype(BF16)
        p_buf[:, lo:hi] = jnp.exp2(s_d - m).astype(BF16)
        oe = _dot(p_buf[:, 0:hi], v_ref[0:hi, hd * HEAD_W:(hd + 1) * HEAD_W])
        o_ref[lo:hi, hd * V_DIM:(hd + 1) * V_DIM] = oe[:, :V_DIM] / oe[:, V_DIM:V_DIM + 1]


def _prompt_attention(q, k, v, nbatch, seq, tq=256, heads=4):
    m = q.shape[0]
    qkv_spec = pl.BlockSpec((seq, heads * HEAD_W), lambda b, h: (b, h))
    return pl.pallas_call(
        functools.partial(_prompt_attn_kernel, seq=seq, tq=tq, heads=heads),
        grid=(nbatch, N_HEADS // heads),
        in_specs=[qkv_spec, qkv_spec, qkv_spec],
        out_specs=pl.BlockSpec((seq, heads * V_DIM), lambda b, h: (b, h)),
        out_shape=jax.ShapeDtypeStruct((m, D_ATTN), F32),
        scratch_shapes=[pltpu.VMEM((2, tq, seq), F32), pltpu.VMEM((tq, seq), BF16)],
        compiler_params=_cparams(2), name="prompt_attention",
    )(q, k, v)


def _sample_attn_kernel(qa_ref, qpe_ref, pc_ref, pk_ref, nc_ref, nk_ref, wuv_ref, o_ref, *, t):
    rows = N_HEADS * t
    qa = qa_ref[0].reshape(rows, KV_LORA)
    qp = qpe_ref[0].reshape(rows, ROPE_W)[:, :ROPE_DIM]
    pc = pc_ref[0].astype(BF16)
    pk_t = pk_ref[0].astype(BF16)
    nc = nc_ref[...].astype(BF16)
    nk = nk_ref[...].astype(BF16)
    s_p = _dot_t(qa, pc) + _dot(qp, pk_t)
    s_n = _dot_t(qa, nc) + _dot_t(qp, nk)
    m = jnp.maximum(jnp.max(s_p, axis=-1, keepdims=True), jnp.max(s_n, axis=-1, keepdims=True))
    p_p = jnp.exp2(s_p - m)
    p_n = jnp.exp2(s_n - m)
    l = jnp.sum(p_p, axis=-1, keepdims=True) + jnp.sum(p_n, axis=-1, keepdims=True)
    o_lat = (_dot(p_p.astype(BF16), pc) + _dot(p_n.astype(BF16), nc)) / l
    o_lat = o_lat.astype(BF16)
    for hd in range(N_HEADS):
        o_ref[:, hd * V_DIM:(hd + 1) * V_DIM] = _dot(o_lat[hd * t:(hd + 1) * t, :], wuv_ref[hd])


def _sample_attention(qa, qpe, past_ckv, past_kpe_t, new_ckv, new_kpe, w_uv_h, layer):
    nb, _, t, _ = qa.shape
    past = past_ckv.shape[2]
    return pl.pallas_call(
        functools.partial(_sample_attn_kernel, t=t),
        grid=(nb,),
        in_specs=[pl.BlockSpec((1, N_HEADS, t, KV_LORA), lambda b: (b, 0, 0, 0)),
                  pl.BlockSpec((1, N_HEADS, t, ROPE_W), lambda b: (b, 0, 0, 0)),
                  pl.BlockSpec((None, 1, past, KV_LORA), lambda b: (layer, b, 0, 0)),
                  pl.BlockSpec((None, 1, ROPE_DIM, past), lambda b: (layer, b, 0, 0)),
                  pl.BlockSpec((t, KV_LORA), lambda b: (b, 0)),
                  pl.BlockSpec((t, ROPE_DIM), lambda b: (b, 0)),
                  pl.BlockSpec(w_uv_h.shape, lambda b: (0, 0, 0))],
        out_specs=pl.BlockSpec((t, D_ATTN), lambda b: (b, 0)),
        out_shape=jax.ShapeDtypeStruct((nb * t, D_ATTN), F32),
        compiler_params=_cparams(1), name="sample_attention",
    )(qa, qpe, past_ckv, past_kpe_t, new_ckv, new_kpe, w_uv_h)


def _out_proj_kernel(x_ref, attn_ref, conv_ref, g_ref, shm_ref, scm_ref, an_ref, woa_ref, woc_ref,
                     o_ref, h_ref, *bf16_w_out, groups):
    if bf16_w_out:
        wo_ref, = bf16_w_out
        da = woa_ref.shape[0]
        wo_ref[0:da, :] = woa_ref[...].astype(BF16)
        wo_ref[da:, :] = woc_ref[...].astype(BF16)
        woa, woc = wo_ref[0:da, :], wo_ref[da:, :]
    else:
        woa, woc = woa_ref[...], woc_ref[...]
    starts = [sum(groups[:j]) for j in range(len(groups))]
    for r0, rows in zip(starts, groups):
        rs = slice(r0, r0 + rows) if len(groups) > 1 else slice(None)
        a = (_rms(attn_ref[rs, :]) * an_ref[...]).astype(BF16)
        mix = _dot(a, woa) + _dot(conv_ref[rs, :], woc)
        y = x_ref[rs, :] + _mod_rows(g_ref, rows) * mix
        o_ref[rs, :] = y
        h_ref[rs, :] = (_rms(y) * (1.0 + _mod_rows(scm_ref, rows))
                        + _mod_rows(shm_ref, rows)).astype(BF16)


def _out_proj(x, attn, conv_n, mod, w, w_out, layer, nbatch, seq, tm, groups=None):
    groups = (tm,) if groups is None else tuple(groups)
    assert sum(groups) == tm
    m, d = x.shape
    nt = seq // tm
    da, dc = attn.shape[1], conv_n.shape[1]
    ins = [x, attn, conv_n, mod.arr, mod.arr, mod.arr, w["attn_out_norm"], w_out, w_out]
    specs = [_rows(tm, d, nt), _rows(tm, da, nt), _rows(tm, dc, nt),
             mod.spec(2), mod.spec(3), mod.spec(4),
             _layer_spec(w["attn_out_norm"], mod.layer)]
    out_specs = [_rows(tm, d, nt), _rows(tm, d, nt)]
    out_shape = [jax.ShapeDtypeStruct((m, d), F32), jax.ShapeDtypeStruct((m, d), BF16)]
    if layer is None:
        specs += [pl.BlockSpec((da, d), lambda b, s: (0, 0), pipeline_mode=pl.Buffered(1)),
                  pl.BlockSpec((dc, d), lambda b, s: (da // dc, 0), pipeline_mode=pl.Buffered(1))]
    else:
        assert nbatch * nt == 1
        specs += [_layer_spec(w_out, layer, (da, d), (0, 0)),
                  _layer_spec(w_out, layer, (dc, d), (da // dc, 0))]
        out_specs.append(pl.BlockSpec((da + dc, d), lambda b, s: (0, 0)))
        out_shape.append(jax.ShapeDtypeStruct((da + dc, d), BF16))
    return pl.pallas_call(
        functools.partial(_out_proj_kernel, groups=groups), grid=(nbatch, nt), in_specs=specs,
        out_specs=out_specs, out_shape=out_shape,
        compiler_params=_cparams(2), name="out_proj",
    )(*ins)


def _mlp_kernel(x_ref, h_ref, g_ref, wup_ref, wdn_ref, fn_ref, o_ref, *extra, nf, final, cast,
                x_rows):
    f = pl.program_id(2)
    if cast:
        wu_ref, wd_ref = extra
        wu_ref[...] = wup_ref[...].astype(BF16)
        wd_ref[...] = wdn_ref[...].astype(BF16)
    else:
        wu_ref, wd_ref = wup_ref, wdn_ref
    if x_rows:
        x_buf, x_sem = extra
        tile = pl.program_id(0) * pl.num_programs(1) + pl.program_id(1)
        row0 = pl.multiple_of(tile * x_rows, x_rows)
        x_copy = pltpu.make_async_copy(x_ref.at[pl.ds(row0, x_rows), :], x_buf, x_sem)

        @pl.when(f == 0)
        def _():
            x_copy.start()

    def partial_down():
        up = _dot(h_ref[...], wu_ref[...])
        act = jnp.square(jnp.maximum(up, 0.0)).astype(BF16)
        return _dot(act, wd_ref[...])

    @pl.when(f == 0)
    def _():
        o_ref[...] = partial_down()

    @pl.when(f > 0)
    def _():
        o_ref[...] += partial_down()

    @pl.when(f == nf - 1)
    def _():
        if x_rows:
            x_copy.wait()
            x = x_buf[...]
        else:
            x = x_ref[...]
        y = x + _mod_rows(g_ref, x.shape[0]) * o_ref[...]
        if final:
            y = _rms(y) * fn_ref[...]
        o_ref[...] = y


def _mlp(x, h, mod, w_up, w_down, layer, final_norm, nbatch, seq, tm, tf, final):
    m, d = x.shape
    nt = seq // tm
    dff = w_up.shape[-1]
    nf = dff // tf
    cast = layer is not None
    out_specs = [_rows(tm, d, nt)]
    out_shape = [jax.ShapeDtypeStruct((m, d), F32)]
    scratch = []
    if cast:
        assert nbatch * nt == 1
        x_spec = _rows(tm, d, nt)
        w_specs = [pl.BlockSpec((None, d, tf), lambda b, s, f: (layer, 0, f)),
                   pl.BlockSpec((None, tf, d), lambda b, s, f: (layer, f, 0))]
        out_specs += [pl.BlockSpec((d, tf), lambda b, s, f: (0, f)),
                      pl.BlockSpec((tf, d), lambda b, s, f: (f, 0))]
        out_shape += [jax.ShapeDtypeStruct((d, dff), BF16), jax.ShapeDtypeStruct((dff, d), BF16)]
    else:
        x_spec = pl.BlockSpec(memory_space=pl.ANY)
        w_specs = [pl.BlockSpec((d, tf), lambda b, s, f: (0, f)),
                   pl.BlockSpec((tf, d), lambda b, s, f: (f, 0))]
        scratch = [pltpu.VMEM((tm, d), F32), pltpu.SemaphoreType.DMA(())]
    return pl.pallas_call(
        functools.partial(_mlp_kernel, nf=nf, final=final, cast=cast,
                          x_rows=0 if cast else tm),
        grid=(nbatch, nt, nf),
        in_specs=[x_spec, _rows(tm, d, nt), mod.spec(5), *w_specs,
                  pl.BlockSpec(final_norm.shape, lambda b, s, f: (0, 0))],
        out_specs=out_specs, out_shape=out_shape, scratch_shapes=scratch,
        compiler_params=_cparams(3), name="mlp",
    )(x, h, mod.arr, w_up, w_down, final_norm)


def _rope_tables(pos):
    half = ROPE_DIM // 2
    inv = ROPE_BASE ** (-jnp.arange(half, dtype=F32) / half)
    ang = pos.astype(F32)[:, None] * inv[None, :]
    cos, sin, zero = jnp.cos(ang), jnp.sin(ang), jnp.zeros_like(ang)
    return (jnp.concatenate([cos, cos, zero, zero], axis=-1),
            jnp.concatenate([-sin, zero, zero, zero], axis=-1),
            jnp.concatenate([zero, sin, zero, zero], axis=-1))


def _prep_weights(q_norm, w_uq, kv_norm, w_ukv, conv_w, attn_out_norm, conv_out_norm,
                  w_out, w_up, w_down):
    return dict(
        w_uq=w_uq, w_ukv=w_ukv,
        q_norm=q_norm[:, None, :], kv_norm=kv_norm[:, None, :],
        conv_w=conv_w, attn_out_norm=attn_out_norm[:, None, :],
        conv_out_norm=conv_out_norm[:, None, :],
        w_out=w_out, w_up=w_up, w_down=w_down)


def kernel(x_prompt, x_sample, c_prompt, c_sample, cache_kv_latent, cache_k_rope, state_conv,
           w_ada, b_ada, w_in, q_norm, w_uq, kv_norm, w_ukv, conv_w,
           attn_out_norm, conv_out_norm, w_out, w_up, w_down, final_norm):
    bp, sp, d = x_prompt.shape
    bs, ss, _ = x_sample.shape
    depth = w_ada.shape[0]
    past = cache_kv_latent.shape[2]
    dc = conv_w.shape[-1]
    ms = bs * ss

    w = _prep_weights(q_norm, w_uq, kv_norm, w_ukv, conv_w, attn_out_norm,
                      conv_out_norm, w_out, w_up, w_down)
    w_in_t = jnp.swapaxes(w_in, 1, 2)
    fnorm = final_norm[None, :]
    tabs_p = _rope_tables(jnp.arange(sp))
    tabs_s = tuple(jnp.tile(tb, (bs, 1)) for tb in _rope_tables(past + jnp.arange(ss)))

    mod = _modulation(jnp.concatenate([c_prompt, c_sample], axis=0), w_ada, b_ada)
    mod = mod.reshape(depth, bp + bs, N_MOD, d)
    mod_p = mod[:, :bp].reshape(depth * bp * N_MOD, 1, d)
    mod_s = jnp.swapaxes(mod[:, bp:], 1, 2).reshape(depth * N_MOD, bs, d)
    zero_conv = jnp.zeros((1, bp, CONV_W - 1, dc), F32)

    xp = x_prompt.reshape(bp * sp, d)
    xs = x_sample.reshape(ms, d)
    past_kpe_t = jnp.swapaxes(cache_k_rope, 2, 3)
    stacks = None
    outs = [[] for _ in range(4)]
    for l in range(depth):
        last = l == depth - 1
        mp = _Mod(mod_p, l, bp)
        msm = _Mod(mod_s, l, 0)

        (qa, qpe, ckv_s, kpe_s, conv_ns, cstate_s, w_a_l, w_b_l, w_uk_l, w_uv_l, _,
         w_uv_h_l, w_uq_l) = _mixer_in_sample(xs, msm, w, w_in_t, l, tabs_s, state_conv, bs, ss)
        attn_s = _sample_attention(qa, qpe, cache_kv_latent, past_kpe_t, ckv_s, kpe_s,
                                   w_uv_h_l, l)
        xs, hs, w_out_l = _out_proj(xs, attn_s, conv_ns, msm, w, w["w_out"], l, 1, ms, tm=ms)
        xs, w_up_l, w_down_l = _mlp(xs, hs, msm, w["w_up"], w["w_down"], l, fnorm, 1, ms,
                                    tm=ms, tf=1024, final=last)
        outs[1].append(ckv_s.reshape(bs, ss, KV_LORA))
        outs[2].append(kpe_s.reshape(bs, ss, ROPE_DIM))
        outs[3].append(cstate_s)

        q, k, v, *stacks, conv_n, cstate = _mixer_in_prompt(
            xp, mp, w, w_a_l, w_b_l, w_uk_l, w_uv_l, w_uq_l, l, tabs_p, zero_conv, bp, sp,
            stacks, tm=256, nsub=1)
        attn = _prompt_attention(q, k, v, bp, sp)
        xp, hp = _out_proj(xp, attn, conv_n, mp, w, w_out_l, None, bp, sp, tm=512, groups=(384, 128))
        xp, = _mlp(xp, hp, mp, w_up_l, w_down_l, None, fnorm, bp, sp, tm=1024, tf=1024, final=last)
        outs[0].append(cstate)

    return (xp.reshape(bp, sp, d), xs.reshape(bs, ss, d),
            stacks[0].reshape(depth, bp, sp, KV_LORA), jnp.swapaxes(stacks[1], 2, 3),
            jnp.stack(outs[0]), jnp.stack(outs[1]), jnp.stack(outs[2]), jnp.stack(outs[3]))
```

```python
import functools
import math

import jax
import jax.numpy as jnp
from jax import lax
from jax.experimental import pallas as pl
from jax.experimental.pallas import tpu as pltpu

F32 = jnp.float32
BF16 = jnp.bfloat16

CHUNK = 64
N_HEADS = 8
NOPE_DIM = 128
ROPE_DIM = 64
V_DIM = 128
QK_DIM = NOPE_DIM + ROPE_DIM
Q_LORA = 512
KV_LORA = 512
D_ATTN = N_HEADS * V_DIM
CONV_W = 3
N_MOD = 6
ROPE_BASE = 10000.0
EPS = 1e-6
ATTN_SCALE = QK_DIM ** -0.5

LANES = 128
HEAD_W = 2 * LANES
ROPE_W = LANES
IN_A_W = Q_LORA + KV_LORA + ROPE_W
CONV_PAD = 8
CACHE_BUFS = 3
Q_SCALE = ATTN_SCALE * math.log2(math.e)
NEG_BIG = -1e30
VMEM_LIMIT = 60000 * 1024


def _cparams(n_axes):
    return pltpu.CompilerParams(dimension_semantics=("arbitrary",) * n_axes,
                                vmem_limit_bytes=VMEM_LIMIT)


def _rms(x):
    return x * lax.rsqrt(jnp.mean(x * x, axis=-1, keepdims=True) + EPS)


def _dot(a, b):
    return jnp.dot(a, b, preferred_element_type=F32)


def _dot_t(a, b):
    return lax.dot_general(a, b, (((1,), (1,)), ((), ())), preferred_element_type=F32)


def _layer_spec(arr, layer, block=None, index=None):
    block = tuple(arr.shape[1:]) if block is None else tuple(block)
    index = (0,) * len(block) if index is None else tuple(index)
    return pl.BlockSpec((None,) + block, lambda *g: (layer,) + index,
                        pipeline_mode=pl.Buffered(1))


def _rows(tm, cols, tiles_per_batch):
    return pl.BlockSpec((tm, cols), lambda b, s, *_: (b * tiles_per_batch + s, 0))


class _Mod:
    def __init__(self, arr, layer, per_batch):
        self.arr, self.layer, self.per_batch = arr, layer, per_batch

    def spec(self, role):
        layer, nb = self.layer, self.per_batch
        if nb:
            return pl.BlockSpec((None,) + self.arr.shape[1:],
                                lambda b, *_: ((layer * nb + b) * N_MOD + role, 0, 0))
        return pl.BlockSpec((None,) + self.arr.shape[1:],
                            lambda *_: (layer * N_MOD + role, 0, 0))


def _mod_kernel(c_ref, w_ref, b_ref, o_ref):
    c = c_ref[...]
    s = (c * jax.nn.sigmoid(c)).astype(BF16)
    o_ref[0] = _dot(s, w_ref[0].astype(BF16)) + b_ref[0]


def _modulation(c_all, w_ada, b_ada, tn=1024):
    depth, d, n = w_ada.shape
    r = c_all.shape[0]
    return pl.pallas_call(
        _mod_kernel,
        grid=(depth, n // tn),
        in_specs=[pl.BlockSpec((r, d), lambda l, j: (0, 0)),
                  pl.BlockSpec((1, d, tn), lambda l, j: (l, 0, j)),
                  pl.BlockSpec((1, 1, tn), lambda l, j: (l, 0, j))],
        out_specs=pl.BlockSpec((1, r, tn), lambda l, j: (l, 0, j)),
        out_shape=jax.ShapeDtypeStruct((depth, r, n), F32),
        compiler_params=_cparams(2),
        name="modulation",
    )(c_all, w_ada, b_ada.reshape(depth, 1, n))


def _rope128(p, cos, s1, s2):
    return p * cos + pltpu.roll(p, 96, 1) * s1 + pltpu.roll(p, 32, 1) * s2


def _mod_rows(m_ref, rows):
    m = m_ref[...]
    nb, d = m.shape
    if nb in (1, rows):
        return m
    return jnp.broadcast_to(m[:, None, :], (nb, rows // nb, d)).reshape(rows, d)


def _adaln(x_ref, rs, sh_ref, sc_ref):
    x = x_ref[rs, :]
    n = x.shape[0]
    return (_rms(x) * (1.0 + _mod_rows(sc_ref, n)) + _mod_rows(sh_ref, n)).astype(BF16)


def _attn_in_common(h, rs, win_ref, qn_ref, wuq_ref, kvn_ref, cos_ref, s1_ref, s2_ref):
    z = _dot(h, win_ref[...])
    cos, s1, s2 = cos_ref[rs, :], s1_ref[rs, :], s2_ref[rs, :]
    qn = (_rms(z[:, :Q_LORA]) * qn_ref[...]).astype(BF16)
    q = _dot(qn, wuq_ref[...]) * Q_SCALE
    ckv = _rms(z[:, Q_LORA:Q_LORA + KV_LORA]) * kvn_ref[...]
    kp = _rope128(z[:, Q_LORA + KV_LORA:], cos, s1, s2)
    rope = functools.partial(_rope128, cos=cos, s1=s1, s2=s2)
    return q, ckv, kp, rope


def _conv_group(h, p0, tg, nb, w_ref, cw_ref, cn_ref, buf):
    dc = cw_ref.shape[-1]
    z = _dot(h, w_ref[...])
    return _conv_post(z[:, :dc], z[:, dc:2 * dc], z[:, 2 * dc:], p0, tg, nb, cw_ref, cn_ref, buf)


def _conv_post(gate_b, gate_c, u, p0, tg, nb, cw_ref, cn_ref, buf):
    dc = cw_ref.shape[-1]
    cu = gate_c * u
    buf[:, CONV_PAD + p0:CONV_PAD + p0 + tg, :] = cu.reshape(nb, tg, dc)
    base = CONV_PAD - (CONV_W - 1) + p0
    y = cw_ref[0:1, :].reshape(1, 1, dc) * buf[:, base:base + tg, :]
    for k in range(1, CONV_W):
        y = y + cw_ref[k:k + 1, :].reshape(1, 1, dc) * buf[:, base + k:base + k + tg, :]
    conv_o = gate_b * y.reshape(nb * tg, dc)
    return (_rms(conv_o) * cn_ref[...]).astype(BF16)


def _conv_carry(buf, st_ref, t):
    last = buf[:, CONV_PAD + t - 2:CONV_PAD + t, :]
    st_ref[...] = last
    buf[:, CONV_PAD - 2:CONV_PAD, :] = last


def _mixer_in_prompt_kernel(x_ref, sh_ref, sc_ref, win_ref, qn_ref, wuq_ref, kvn_ref,
                            cos_ref, s1_ref, s2_ref, wuk_ref, wuv_ref,
                            winb_ref, cw_ref, cn_ref, prev_ref, *rest, nsub, first):
    q_out, k_out, v_out, ckv_out, kpe_out, conv_out, st_ref, buf = rest[-8:]

    @pl.when(pl.program_id(1) == 0)
    def _():
        buf[:, CONV_PAD - 2:CONV_PAD, :] = prev_ref[...]

    if first:
        ckv_out[1:] = jnp.zeros((ckv_out.shape[0] - 1,) + ckv_out.shape[1:], F32)
        kpe_out[1:] = jnp.zeros((kpe_out.shape[0] - 1,) + kpe_out.shape[1:], F32)
        ckv_out, kpe_out = ckv_out.at[0], kpe_out.at[0]
    rows = x_ref.shape[0] // nsub
    lane = lax.broadcasted_iota(jnp.int32, (rows, LANES), 1)
    ones_col = jnp.where(lane == 0, 1.0, 0.0).astype(BF16)
    for j in range(nsub):
        rs = slice(j * rows, (j + 1) * rows)
        h = _adaln(x_ref, rs, sh_ref, sc_ref)
        conv_out[rs, :] = _conv_group(h, j * rows, rows, 1, winb_ref, cw_ref, cn_ref, buf)
        q, ckv, kp, rope = _attn_in_common(h, rs, win_ref, qn_ref, wuq_ref, kvn_ref,
                                           cos_ref, s1_ref, s2_ref)
        ckv_out[rs, :] = ckv
        kpe_out[:, rs] = kp.T[:ROPE_DIM, :]
        ckb = ckv.astype(BF16)
        kn = _dot(ckb, wuk_ref[...])
        v = _dot(ckb, wuv_ref[...]).astype(BF16)
        kpb = kp.astype(BF16)
        for hd in range(N_HEADS):
            c0 = hd * HEAD_W
            v_out[rs, c0:c0 + V_DIM] = v[:, hd * V_DIM:(hd + 1) * V_DIM]
            v_out[rs, c0 + V_DIM:c0 + HEAD_W] = ones_col
            q_out[rs, c0:c0 + NOPE_DIM] = q[:, c0:c0 + NOPE_DIM].astype(BF16)
            q_out[rs, c0 + NOPE_DIM:c0 + HEAD_W] = rope(q[:, c0 + NOPE_DIM:c0 + HEAD_W]).astype(BF16)
            k_out[rs, c0:c0 + NOPE_DIM] = kn[:, hd * NOPE_DIM:(hd + 1) * NOPE_DIM].astype(BF16)
            k_out[rs, c0 + NOPE_DIM:c0 + HEAD_W] = kpb
    _conv_carry(buf, st_ref, x_ref.shape[0])


def _mixer_in_sample_kernel(x_ref, sh_ref, sc_ref, wint_hbm, qn_ref, wuq_ref, kvn_ref,
                            cos_ref, s1_ref, s2_ref, wukv_ref, cw_ref, cn_ref, prev_ref,
                            qa_out, qpe_out, ckv_out, kpe_out, conv_out, st_ref, wa_out, wb_out,
                            wuk_out, wuv_out, wukt_out, wuvh_out, wuq_out,
                            w_buf, w_sem, h_scr, z_scr, buf, *, nb, t, layer):
    s = pl.program_id(0)
    dc = cw_ref.shape[-1]
    n_a = Q_LORA + KV_LORA + ROPE_DIM
    chunk_rows = [(0, IN_A_W)] + [(n_a + j * dc, dc) for j in range(3)]

    def chunk_copy(k):
        r0, rows = chunk_rows[k]
        return pltpu.make_async_copy(wint_hbm.at[layer, pl.ds(r0, rows), :],
                                     w_buf.at[k % 2, pl.ds(0, rows), :], w_sem.at[k % 2])

    for k in range(len(chunk_rows)):
        @pl.when(s == k)
        def _(k=k):
            if k == 0:
                chunk_copy(0).start()
                h_scr[...] = _adaln(x_ref, slice(None), sh_ref, sc_ref)
                buf[:, CONV_PAD - 2:CONV_PAD, :] = prev_ref[...]
            if k + 1 < len(chunk_rows):
                chunk_copy(k + 1).start()
            chunk_copy(k).wait()
            rows = chunk_rows[k][1]
            w_t = w_buf[k % 2, 0:rows, :].T.astype(BF16)
            if k == 0:
                wa_out[...] = w_t
                for hd in range(N_HEADS):
                    c0 = hd * HEAD_W
                    wuq_out[:, c0:c0 + QK_DIM] = wuq_ref[:, hd * QK_DIM:(hd + 1) * QK_DIM].astype(BF16)
                    wuq_out[:, c0 + QK_DIM:c0 + HEAD_W] = jnp.zeros(
                        (wuq_ref.shape[0], HEAD_W - QK_DIM), BF16)
                q, ckv, kp, rope = _attn_in_common(h_scr[...], slice(None), wa_out, qn_ref,
                                                   wuq_out, kvn_ref, cos_ref, s1_ref, s2_ref)
                ckv_out[...] = ckv
                kpe_out[...] = kp[:, :ROPE_DIM]
                for hd in range(N_HEADS):
                    k0 = hd * (NOPE_DIM + V_DIM)
                    w_k = wukv_ref[:, k0:k0 + NOPE_DIM]
                    w_v = wukv_ref[:, k0 + NOPE_DIM:k0 + NOPE_DIM + V_DIM].astype(BF16)
                    wuk_out[:, hd * NOPE_DIM:(hd + 1) * NOPE_DIM] = w_k.astype(BF16)
                    wukt_out[hd] = w_k.T.astype(BF16)
                    wuv_out[:, hd * V_DIM:(hd + 1) * V_DIM] = w_v
                    wuvh_out[hd] = w_v
                for hd in range(N_HEADS):
                    c0 = hd * HEAD_W
                    qa = _dot(q[:, c0:c0 + NOPE_DIM].astype(BF16), wukt_out[hd])
                    qa_out[:, hd] = qa.reshape(nb, t, KV_LORA).astype(BF16)
                    qp = rope(q[:, c0 + NOPE_DIM:c0 + HEAD_W])
                    qpe_out[:, hd] = qp.reshape(nb, t, ROPE_W).astype(BF16)
            else:
                wb_out[...] = w_t
                z_scr[k - 1] = _dot(h_scr[...], wb_out[...])
            if k == len(chunk_rows) - 1:
                conv_out[...] = _conv_post(z_scr[0], z_scr[1], z_scr[2], 0, t, nb,
                                           cw_ref, cn_ref, buf)
                _conv_carry(buf, st_ref, t)


def _mixer_in_prompt(x, mod, w, w_a, w_b, w_uk, w_uv, w_uq, layer, tabs, prev, nbatch, seq,
                     stacks, tm, nsub):
    m, d = x.shape
    depth = w["q_norm"].shape[0]
    dc = w["conv_w"].shape[-1]
    nt = seq // tm
    tab_spec = pl.BlockSpec((tm, ROPE_W), lambda b, s: (s, 0))
    whole = lambda a: pl.BlockSpec(a.shape, lambda b, s: (0,) * a.ndim,
                                   pipeline_mode=pl.Buffered(1))
    ins = [x, mod.arr, mod.arr, w_a, w["q_norm"], w_uq, w["kv_norm"],
           *tabs, w_uk, w_uv, w_b, w["conv_w"], w["conv_out_norm"], prev]
    specs = [_rows(tm, d, nt), mod.spec(0), mod.spec(1), whole(w_a),
             _layer_spec(w["q_norm"], layer), whole(w_uq),
             _layer_spec(w["kv_norm"], layer), tab_spec, tab_spec, tab_spec,
             whole(w_uk), whole(w_uv),
             whole(w_b), _layer_spec(w["conv_w"], layer),
             _layer_spec(w["conv_out_norm"], layer),
             pl.BlockSpec((None, 1, CONV_W - 1, dc), lambda b, s: (0, b, 0, 0))]
    aliases = {}
    if stacks is not None:
        aliases = {len(ins): 3, len(ins) + 1: 4}
        ins += list(stacks)
        specs += [pl.BlockSpec(memory_space=pl.ANY)] * 2
    outs = [jax.ShapeDtypeStruct((m, N_HEADS * HEAD_W), BF16),
            jax.ShapeDtypeStruct((m, N_HEADS * HEAD_W), BF16),
            jax.ShapeDtypeStruct((m, N_HEADS * HEAD_W), BF16),
            jax.ShapeDtypeStruct((depth, m, KV_LORA), F32),
            jax.ShapeDtypeStruct((depth, nbatch, ROPE_DIM, seq), F32),
            jax.ShapeDtypeStruct((m, dc), BF16),
            jax.ShapeDtypeStruct((nbatch, CONV_W - 1, dc), F32)]
    first = stacks is None
    lsel = (depth, 0) if first else (None, layer)
    out_specs = [_rows(tm, N_HEADS * HEAD_W, nt), _rows(tm, N_HEADS * HEAD_W, nt),
                 _rows(tm, N_HEADS * HEAD_W, nt),
                 pl.BlockSpec((lsel[0], tm, KV_LORA), lambda b, s: (lsel[1], b * nt + s, 0)),
                 pl.BlockSpec((lsel[0], None, ROPE_DIM, tm), lambda b, s: (lsel[1], b, 0, s)),
                 _rows(tm, dc, nt),
                 pl.BlockSpec((1, CONV_W - 1, dc), lambda b, s: (b, 0, 0))]
    return pl.pallas_call(
        functools.partial(_mixer_in_prompt_kernel, nsub=nsub, first=first), grid=(nbatch, nt),
        in_specs=specs, out_specs=out_specs, out_shape=outs, input_output_aliases=aliases,
        scratch_shapes=[pltpu.VMEM((1, CONV_PAD + tm, dc), F32)],
        compiler_params=_cparams(2), name="mixer_in_prompt",
    )(*ins)


def _mixer_in_sample(x, mod, w, w_in_t, layer, tabs, prev, nb, t):
    m, d = x.shape
    dc = w["conv_w"].shape[-1]
    steps = 4
    assert w_in_t.shape[1] == Q_LORA + KV_LORA + ROPE_DIM + 3 * dc
    const = lambda a: pl.BlockSpec(a.shape, lambda s: (0,) * a.ndim)
    ins = [x, mod.arr, mod.arr, w_in_t, w["q_norm"], w["w_uq"], w["kv_norm"],
           *tabs, w["w_ukv"], w["conv_w"], w["conv_out_norm"], prev]
    specs = [const(x), mod.spec(0), mod.spec(1), pl.BlockSpec(memory_space=pl.ANY),
             _layer_spec(w["q_norm"], layer), _layer_spec(w["w_uq"], layer),
             _layer_spec(w["kv_norm"], layer), const(tabs[0]), const(tabs[1]), const(tabs[2]),
             _layer_spec(w["w_ukv"], layer), _layer_spec(w["conv_w"], layer),
             _layer_spec(w["conv_out_norm"], layer),
             pl.BlockSpec((None, nb, CONV_W - 1, dc), lambda s: (layer, 0, 0, 0))]
    outs = [jax.ShapeDtypeStruct((nb, N_HEADS, t, KV_LORA), BF16),
            jax.ShapeDtypeStruct((nb, N_HEADS, t, ROPE_W), BF16),
            jax.ShapeDtypeStruct((m, KV_LORA), F32),
            jax.ShapeDtypeStruct((m, ROPE_DIM), F32),
            jax.ShapeDtypeStruct((m, dc), BF16),
            jax.ShapeDtypeStruct((nb, CONV_W - 1, dc), F32),
            jax.ShapeDtypeStruct((d, IN_A_W), BF16),
            jax.ShapeDtypeStruct((d, 3 * dc), BF16),
            jax.ShapeDtypeStruct((KV_LORA, N_HEADS * NOPE_DIM), BF16),
            jax.ShapeDtypeStruct((KV_LORA, N_HEADS * V_DIM), BF16),
            jax.ShapeDtypeStruct((N_HEADS, NOPE_DIM, KV_LORA), BF16),
            jax.ShapeDtypeStruct((N_HEADS, KV_LORA, V_DIM), BF16),
            jax.ShapeDtypeStruct((Q_LORA, N_HEADS * HEAD_W), BF16)]
    out_specs = [const(o) for o in outs[:7]]
    out_specs.append(pl.BlockSpec((d, dc), lambda s: (0, jnp.maximum(s - 1, 0))))
    out_specs += [const(o) for o in outs[8:]]
    return pl.pallas_call(
        functools.partial(_mixer_in_sample_kernel, nb=nb, t=t, layer=layer), grid=(steps,),
        in_specs=specs, out_specs=out_specs, out_shape=outs,
        scratch_shapes=[pltpu.VMEM((2, IN_A_W, d), F32), pltpu.SemaphoreType.DMA((2,)),
                        pltpu.VMEM((m, d), BF16), pltpu.VMEM((3, m, dc), F32),
                        pltpu.VMEM((nb, CONV_PAD + t, dc), F32)],
        compiler_params=_cparams(1), name="mixer_in_sample",
    )(*ins)


def _prompt_attn_kernel(q_ref, k_ref, v_ref, o_ref, s_buf, p_buf, *, seq, tq, heads):
    row_chunk = lax.broadcasted_iota(jnp.int32, (tq, tq), 0) // CHUNK
    col_chunk = lax.broadcasted_iota(jnp.int32, (tq, tq), 1) // CHUNK
    visible = col_chunk <= row_chunk
    tiles = [(hd, qi) for hd in range(heads) for qi in range(seq // tq)]

    def scores(i):
        hd, qi = tiles[i]
        hi = (qi + 1) * tq
        cols = slice(hd * HEAD_W, (hd + 1) * HEAD_W)
        s_buf[i % 2, :, 0:hi] = _dot_t(q_ref[qi * tq:hi, cols], k_ref[0:hi, cols])

    scores(0)
    for i, (hd, qi) in enumerate(tiles):
        lo, hi = qi * tq, (qi + 1) * tq
        if i + 1 < len(tiles):
            scores(i + 1)
        s_d = jnp.where(visible, s_buf[i % 2, :, lo:hi], NEG_BIG)
        m = jnp.max(s_d, axis=-1, keepdims=True)
        if qi > 0:
            m = jnp.maximum(m, jnp.max(s_buf[i % 2, :, 0:lo], axis=-1, keepdims=True))
            p_buf[:, 0:lo] = jnp.exp2(s_buf[i % 2, :, 0:lo] - m).astype(BF16)
        p_buf[:, lo:hi] = jnp.exp2(s_d - m).astype(BF16)
        oe = _dot(p_buf[:, 0:hi], v_ref[0:hi, hd * HEAD_W:(hd + 1) * HEAD_W])
        o_ref[lo:hi, hd * V_DIM:(hd + 1) * V_DIM] = oe[:, :V_DIM] / oe[:, V_DIM:V_DIM + 1]


def _prompt_attention(q, k, v, nbatch, seq, tq=256, heads=4):
    m = q.shape[0]
    qkv_spec = pl.BlockSpec((seq, heads * HEAD_W), lambda b, h: (b, h))
    return pl.pallas_call(
        functools.partial(_prompt_attn_kernel, seq=seq, tq=tq, heads=heads),
        grid=(nbatch, N_HEADS // heads),
        in_specs=[qkv_spec, qkv_spec, qkv_spec],
        out_specs=pl.BlockSpec((seq, heads * V_DIM), lambda b, h: (b, h)),
        out_shape=jax.ShapeDtypeStruct((m, D_ATTN), F32),
        scratch_shapes=[pltpu.VMEM((2, tq, seq), F32), pltpu.VMEM((tq, seq), BF16)],
        compiler_params=_cparams(2), name="prompt_attention",
    )(q, k, v)


def _sample_attn_kernel(qa_ref, qpe_ref, pc_hbm, pk_ref, nc_ref, nk_ref, wuv_ref, o_ref,
                        c_buf, c_sem, *, t, layer, nb):
    b = pl.program_id(0)

    def cache_copy(k):
        slot = lax.rem(k, CACHE_BUFS)
        return pltpu.make_async_copy(pc_hbm.at[layer, k], c_buf.at[slot], c_sem.at[slot])

    @pl.when(b == 0)
    def _():
        for k in range(min(CACHE_BUFS - 1, nb)):
            cache_copy(k).start()

    @pl.when(b + CACHE_BUFS - 1 < nb)
    def _():
        cache_copy(b + CACHE_BUFS - 1).start()

    cache_copy(b).wait()
    rows = N_HEADS * t
    qa = qa_ref[0].reshape(rows, KV_LORA)
    qp = qpe_ref[0].reshape(rows, ROPE_W)[:, :ROPE_DIM]
    pc = c_buf[lax.rem(b, CACHE_BUFS)].astype(BF16)
    pk_t = pk_ref[0].astype(BF16)
    nc = nc_ref[...].astype(BF16)
    nk = nk_ref[...].astype(BF16)
    s_p = _dot_t(qa, pc) + _dot(qp, pk_t)
    s_n = _dot_t(qa, nc) + _dot_t(qp, nk)
    m = jnp.maximum(jnp.max(s_p, axis=-1, keepdims=True), jnp.max(s_n, axis=-1, keepdims=True))
    p_p = jnp.exp2(s_p - m)
    p_n = jnp.exp2(s_n - m)
    l = jnp.sum(p_p, axis=-1, keepdims=True) + jnp.sum(p_n, axis=-1, keepdims=True)
    o_lat = (_dot(p_p.astype(BF16), pc) + _dot(p_n.astype(BF16), nc)) / l
    o_lat = o_lat.astype(BF16)
    for hd in range(N_HEADS):
        o_ref[:, hd * V_DIM:(hd + 1) * V_DIM] = _dot(o_lat[hd * t:(hd + 1) * t, :], wuv_ref[hd])


def _sample_attention(qa, qpe, past_ckv, past_kpe_t, new_ckv, new_kpe, w_uv_h, layer):
    nb, _, t, _ = qa.shape
    past = past_ckv.shape[2]
    return pl.pallas_call(
        functools.partial(_sample_attn_kernel, t=t, layer=layer, nb=nb),
        grid=(nb,),
        in_specs=[pl.BlockSpec((1, N_HEADS, t, KV_LORA), lambda b: (b, 0, 0, 0)),
                  pl.BlockSpec((1, N_HEADS, t, ROPE_W), lambda b: (b, 0, 0, 0)),
                  pl.BlockSpec(memory_space=pl.ANY),
                  pl.BlockSpec((None, 1, ROPE_DIM, past), lambda b: (layer, b, 0, 0)),
                  pl.BlockSpec((t, KV_LORA), lambda b: (b, 0)),
                  pl.BlockSpec((t, ROPE_DIM), lambda b: (b, 0)),
                  pl.BlockSpec(w_uv_h.shape, lambda b: (0, 0, 0))],
        out_specs=pl.BlockSpec((t, D_ATTN), lambda b: (b, 0)),
        out_shape=jax.ShapeDtypeStruct((nb * t, D_ATTN), F32),
        scratch_shapes=[pltpu.VMEM((CACHE_BUFS, past, KV_LORA), F32),
                        pltpu.SemaphoreType.DMA((CACHE_BUFS,))],
        compiler_params=_cparams(1), name="sample_attention",
    )(qa, qpe, past_ckv, past_kpe_t, new_ckv, new_kpe, w_uv_h)


def _out_proj_kernel(x_ref, attn_ref, conv_ref, g_ref, shm_ref, scm_ref, an_ref, woa_ref, woc_ref,
                     o_ref, h_ref, *bf16_w_out, groups):
    if bf16_w_out:
        wo_ref, = bf16_w_out
        da = woa_ref.shape[0]
        wo_ref[0:da, :] = woa_ref[...].astype(BF16)
        wo_ref[da:, :] = woc_ref[...].astype(BF16)
        woa, woc = wo_ref[0:da, :], wo_ref[da:, :]
    else:
        woa, woc = woa_ref[...], woc_ref[...]
    starts = [sum(groups[:j]) for j in range(len(groups))]
    for r0, rows in zip(starts, groups):
        rs = slice(r0, r0 + rows) if len(groups) > 1 else slice(None)
        a = (_rms(attn_ref[rs, :]) * an_ref[...]).astype(BF16)
        mix = _dot(a, woa) + _dot(conv_ref[rs, :], woc)
        y = x_ref[rs, :] + _mod_rows(g_ref, rows) * mix
        o_ref[rs, :] = y
        h_ref[rs, :] = (_rms(y) * (1.0 + _mod_rows(scm_ref, rows))
                        + _mod_rows(shm_ref, rows)).astype(BF16)


def _out_proj(x, attn, conv_n, mod, w, w_out, layer, nbatch, seq, tm, groups=None):
    groups = (tm,) if groups is None else tuple(groups)
    assert sum(groups) == tm
    m, d = x.shape
    nt = seq // tm
    da, dc = attn.shape[1], conv_n.shape[1]
    ins = [x, attn, conv_n, mod.arr, mod.arr, mod.arr, w["attn_out_norm"], w_out, w_out]
    specs = [_rows(tm, d, nt), _rows(tm, da, nt), _rows(tm, dc, nt),
             mod.spec(2), mod.spec(3), mod.spec(4),
             _layer_spec(w["attn_out_norm"], mod.layer)]
    out_specs = [_rows(tm, d, nt), _rows(tm, d, nt)]
    out_shape = [jax.ShapeDtypeStruct((m, d), F32), jax.ShapeDtypeStruct((m, d), BF16)]
    if layer is None:
        specs += [pl.BlockSpec((da, d), lambda b, s: (0, 0), pipeline_mode=pl.Buffered(1)),
                  pl.BlockSpec((dc, d), lambda b, s: (da // dc, 0), pipeline_mode=pl.Buffered(1))]
    else:
        assert nbatch * nt == 1
        specs += [_layer_spec(w_out, layer, (da, d), (0, 0)),
                  _layer_spec(w_out, layer, (dc, d), (da // dc, 0))]
        out_specs.append(pl.BlockSpec((da + dc, d), lambda b, s: (0, 0)))
        out_shape.append(jax.ShapeDtypeStruct((da + dc, d), BF16))
    return pl.pallas_call(
        functools.partial(_out_proj_kernel, groups=groups), grid=(nbatch, nt), in_specs=specs,
        out_specs=out_specs, out_shape=out_shape,
        compiler_params=_cparams(2), name="out_proj",
    )(*ins)


def _mlp_kernel(x_ref, h_ref, g_ref, wup_ref, wdn_ref, fn_ref, o_ref, *extra, nf, final, cast,
                x_rows):
    f = pl.program_id(2)
    if cast:
        wu_ref, wd_ref = extra
        wu_ref[...] = wup_ref[...].astype(BF16)
        wd_ref[...] = wdn_ref[...].astype(BF16)
    else:
        wu_ref, wd_ref = wup_ref, wdn_ref
    if x_rows:
        x_buf, x_sem = extra
        tile = pl.program_id(0) * pl.num_programs(1) + pl.program_id(1)
        row0 = pl.multiple_of(tile * x_rows, x_rows)
        x_copy = pltpu.make_async_copy(x_ref.at[pl.ds(row0, x_rows), :], x_buf, x_sem)

        @pl.when(f == 0)
        def _():
            x_copy.start()

    def partial_down():
        up = _dot(h_ref[...], wu_ref[...])
        act = jnp.square(jnp.maximum(up, 0.0)).astype(BF16)
        return _dot(act, wd_ref[...])

    @pl.when(f == 0)
    def _():
        o_ref[...] = partial_down()

    @pl.when(f > 0)
    def _():
        o_ref[...] += partial_down()

    @pl.when(f == nf - 1)
    def _():
        if x_rows:
            x_copy.wait()
            x = x_buf[...]
        else:
            x = x_ref[...]
        y = x + _mod_rows(g_ref, x.shape[0]) * o_ref[...]
        if final:
            y = _rms(y) * fn_ref[...]
        o_ref[...] = y


def _mlp(x, h, mod, w_up, w_down, layer, final_norm, nbatch, seq, tm, tf, final):
    m, d = x.shape
    nt = seq // tm
    dff = w_up.shape[-1]
    nf = dff // tf
    cast = layer is not None
    out_specs = [_rows(tm, d, nt)]
    out_shape = [jax.ShapeDtypeStruct((m, d), F32)]
    scratch = []
    if cast:
        assert nbatch * nt == 1
        x_spec = _rows(tm, d, nt)
        w_specs = [pl.BlockSpec((None, d, tf), lambda b, s, f: (layer, 0, f)),
                   pl.BlockSpec((None, tf, d), lambda b, s, f: (layer, f, 0))]
        out_specs += [pl.BlockSpec((d, tf), lambda b, s, f: (0, f)),
                      pl.BlockSpec((tf, d), lambda b, s, f: (f, 0))]
        out_shape += [jax.ShapeDtypeStruct((d, dff), BF16), jax.ShapeDtypeStruct((dff, d), BF16)]
    else:
        x_spec = pl.BlockSpec(memory_space=pl.ANY)
        w_specs = [pl.BlockSpec((d, tf), lambda b, s, f: (0, f)),
                   pl.BlockSpec((tf, d), lambda b, s, f: (f, 0))]
        scratch = [pltpu.VMEM((tm, d), F32), pltpu.SemaphoreType.DMA(())]
    return pl.pallas_call(
        functools.partial(_mlp_kernel, nf=nf, final=final, cast=cast,
                          x_rows=0 if cast else tm),
        grid=(nbatch, nt, nf),
        in_specs=[x_spec, _rows(tm, d, nt), mod.spec(5), *w_specs,
                  pl.BlockSpec(final_norm.shape, lambda b, s, f: (0, 0))],
        out_specs=out_specs, out_shape=out_shape, scratch_shapes=scratch,
        compiler_params=_cparams(3), name="mlp",
    )(x, h, mod.arr, w_up, w_down, final_norm)


def _rope_tables(pos):
    half = ROPE_DIM // 2
    inv = ROPE_BASE ** (-jnp.arange(half, dtype=F32) / half)
    ang = pos.astype(F32)[:, None] * inv[None, :]
    cos, sin, zero = jnp.cos(ang), jnp.sin(ang), jnp.zeros_like(ang)
    return (jnp.concatenate([cos, cos, zero, zero], axis=-1),
            jnp.concatenate([-sin, zero, zero, zero], axis=-1),
            jnp.concatenate([zero, sin, zero, zero], axis=-1))


def _prep_weights(q_norm, w_uq, kv_norm, w_ukv, conv_w, attn_out_norm, conv_out_norm,
                  w_out, w_up, w_down):
    return dict(
        w_uq=w_uq, w_ukv=w_ukv,
        q_norm=q_norm[:, None, :], kv_norm=kv_norm[:, None, :],
        conv_w=conv_w, attn_out_norm=attn_out_norm[:, None, :],
        conv_out_norm=conv_out_norm[:, None, :],
        w_out=w_out, w_up=w_up, w_down=w_down)


def kernel(x_prompt, x_sample, c_prompt, c_sample, cache_kv_latent, cache_k_rope, state_conv,
           w_ada, b_ada, w_in, q_norm, w_uq, kv_norm, w_ukv, conv_w,
           attn_out_norm, conv_out_norm, w_out, w_up, w_down, final_norm):
    bp, sp, d = x_prompt.shape
    bs, ss, _ = x_sample.shape
    depth = w_ada.shape[0]
    past = cache_kv_latent.shape[2]
    dc = conv_w.shape[-1]
    ms = bs * ss

    w = _prep_weights(q_norm, w_uq, kv_norm, w_ukv, conv_w, attn_out_norm,
                      conv_out_norm, w_out, w_up, w_down)
    w_in_t = jnp.swapaxes(w_in, 1, 2)
    fnorm = final_norm[None, :]
    tabs_p = _rope_tables(jnp.arange(sp))
    tabs_s = tuple(jnp.tile(tb, (bs, 1)) for tb in _rope_tables(past + jnp.arange(ss)))

    mod = _modulation(jnp.concatenate([c_prompt, c_sample], axis=0), w_ada, b_ada)
    mod = mod.reshape(depth, bp + bs, N_MOD, d)
    mod_p = mod[:, :bp].reshape(depth * bp * N_MOD, 1, d)
    mod_s = jnp.swapaxes(mod[:, bp:], 1, 2).reshape(depth * N_MOD, bs, d)
    zero_conv = jnp.zeros((1, bp, CONV_W - 1, dc), F32)

    xp = x_prompt.reshape(bp * sp, d)
    xs = x_sample.reshape(ms, d)
    past_kpe_t = jnp.swapaxes(cache_k_rope, 2, 3)
    stacks = None
    outs = [[] for _ in range(4)]
    for l in range(depth):
        last = l == depth - 1
        mp = _Mod(mod_p, l, bp)
        msm = _Mod(mod_s, l, 0)

        (qa, qpe, ckv_s, kpe_s, conv_ns, cstate_s, w_a_l, w_b_l, w_uk_l, w_uv_l, _,
         w_uv_h_l, w_uq_l) = _mixer_in_sample(xs, msm, w, w_in_t, l, tabs_s, state_conv, bs, ss)
        attn_s = _sample_attention(qa, qpe, cache_kv_latent, past_kpe_t, ckv_s, kpe_s,
                                   w_uv_h_l, l)
        xs, hs, w_out_l = _out_proj(xs, attn_s, conv_ns, msm, w, w["w_out"], l, 1, ms, tm=ms)
        xs, w_up_l, w_down_l = _mlp(xs, hs, msm, w["w_up"], w["w_down"], l, fnorm, 1, ms,
                                    tm=ms, tf=1024, final=last)
        outs[1].append(ckv_s.reshape(bs, ss, KV_LORA))
        outs[2].append(kpe_s.reshape(bs, ss, ROPE_DIM))
        outs[3].append(cstate_s)

        q, k, v, *stacks, conv_n, cstate = _mixer_in_prompt(
            xp, mp, w, w_a_l, w_b_l, w_uk_l, w_uv_l, w_uq_l, l, tabs_p, zero_conv, bp, sp,
            stacks, tm=256, nsub=1)
        attn = _prompt_attention(q, k, v, bp, sp)
        xp, hp = _out_proj(xp, attn, conv_n, mp, w, w_out_l, None, bp, sp, tm=512, groups=(384, 128))
        xp, = _mlp(xp, hp, mp, w_up_l, w_down_l, None, fnorm, bp, sp, tm=1024, tf=1024, final=last)
        outs[0].append(cstate)

    return (xp.reshape(bp, sp, d), xs.reshape(bs, ss, d),
            stacks[0].reshape(depth, bp, sp, KV_LORA), jnp.swapaxes(stacks[1], 2, 3),
            jnp.stack(outs[0]), jnp.stack(outs[1]), jnp.stack(outs[2]), jnp.stack(outs[3]))
```

```python
import functools
import math

import jax
import jax.numpy as jnp
from jax import lax
from jax.experimental import pallas as pl
from jax.experimental.pallas import tpu as pltpu

F32 = jnp.float32
BF16 = jnp.bfloat16

CHUNK = 64
N_HEADS = 8
NOPE_DIM = 128
ROPE_DIM = 64
V_DIM = 128
QK_DIM = NOPE_DIM + ROPE_DIM
Q_LORA = 512
KV_LORA = 512
D_ATTN = N_HEADS * V_DIM
CONV_W = 3
N_MOD = 6
ROPE_BASE = 10000.0
EPS = 1e-6
ATTN_SCALE = QK_DIM ** -0.5

LANES = 128
HEAD_W = 2 * LANES
ROPE_W = LANES
IN_A_W = Q_LORA + KV_LORA + ROPE_W
CONV_PAD = 8
CACHE_BUFS = 4
Q_SCALE = ATTN_SCALE * math.log2(math.e)
NEG_BIG = -1e30
VMEM_LIMIT = 60000 * 1024


def _cparams(n_axes):
    return pltpu.CompilerParams(dimension_semantics=("arbitrary",) * n_axes,
                                vmem_limit_bytes=VMEM_LIMIT)


def _rms(x):
    return x * lax.rsqrt(jnp.mean(x * x, axis=-1, keepdims=True) + EPS)


def _dot(a, b):
    return jnp.dot(a, b, preferred_element_type=F32)


def _dot_t(a, b):
    return lax.dot_general(a, b, (((1,), (1,)), ((), ())), preferred_element_type=F32)


def _layer_spec(arr, layer, block=None, index=None):
    block = tuple(arr.shape[1:]) if block is None else tuple(block)
    index = (0,) * len(block) if index is None else tuple(index)
    return pl.BlockSpec((None,) + block, lambda *g: (layer,) + index,
                        pipeline_mode=pl.Buffered(1))


def _rows(tm, cols, tiles_per_batch):
    return pl.BlockSpec((tm, cols), lambda b, s, *_: (b * tiles_per_batch + s, 0))


class _Mod:
    def __init__(self, arr, layer, per_batch):
        self.arr, self.layer, self.per_batch = arr, layer, per_batch

    def spec(self, role):
        layer, nb = self.layer, self.per_batch
        if nb:
            return pl.BlockSpec((None,) + self.arr.shape[1:],
                                lambda b, *_: ((layer * nb + b) * N_MOD + role, 0, 0))
        return pl.BlockSpec((None,) + self.arr.shape[1:],
                            lambda *_: (layer * N_MOD + role, 0, 0))


def _mod_kernel(c_ref, w_ref, b_ref, o_ref):
    c = c_ref[...]
    s = (c * jax.nn.sigmoid(c)).astype(BF16)
    o_ref[0] = _dot(s, w_ref[0].astype(BF16)) + b_ref[0]


def _modulation(c_all, w_ada, b_ada, tn=1024):
    depth, d, n = w_ada.shape
    r = c_all.shape[0]
    return pl.pallas_call(
        _mod_kernel,
        grid=(depth, n // tn),
        in_specs=[pl.BlockSpec((r, d), lambda l, j: (0, 0)),
                  pl.BlockSpec((1, d, tn), lambda l, j: (l, 0, j)),
                  pl.BlockSpec((1, 1, tn), lambda l, j: (l, 0, j))],
        out_specs=pl.BlockSpec((1, r, tn), lambda l, j: (l, 0, j)),
        out_shape=jax.ShapeDtypeStruct((depth, r, n), F32),
        compiler_params=_cparams(2),
        name="modulation",
    )(c_all, w_ada, b_ada.reshape(depth, 1, n))


def _rope128(p, cos, s1, s2):
    return p * cos + pltpu.roll(p, 96, 1) * s1 + pltpu.roll(p, 32, 1) * s2


def _mod_rows(m_ref, rows):
    m = m_ref[...]
    nb, d = m.shape
    if nb in (1, rows):
        return m
    return jnp.broadcast_to(m[:, None, :], (nb, rows // nb, d)).reshape(rows, d)


def _adaln(x_ref, rs, sh_ref, sc_ref):
    x = x_ref[rs, :]
    n = x.shape[0]
    return (_rms(x) * (1.0 + _mod_rows(sc_ref, n)) + _mod_rows(sh_ref, n)).astype(BF16)


def _attn_in_common(h, rs, win_ref, qn_ref, wuq_ref, kvn_ref, cos_ref, s1_ref, s2_ref):
    z = _dot(h, win_ref[...])
    cos, s1, s2 = cos_ref[rs, :], s1_ref[rs, :], s2_ref[rs, :]
    qn = (_rms(z[:, :Q_LORA]) * qn_ref[...]).astype(BF16)
    q = _dot(qn, wuq_ref[...]) * Q_SCALE
    ckv = _rms(z[:, Q_LORA:Q_LORA + KV_LORA]) * kvn_ref[...]
    kp = _rope128(z[:, Q_LORA + KV_LORA:], cos, s1, s2)
    rope = functools.partial(_rope128, cos=cos, s1=s1, s2=s2)
    return q, ckv, kp, rope


def _conv_group(h, p0, tg, nb, w_ref, cw_ref, cn_ref, buf):
    dc = cw_ref.shape[-1]
    z = _dot(h, w_ref[...])
    return _conv_post(z[:, :dc], z[:, dc:2 * dc], z[:, 2 * dc:], p0, tg, nb, cw_ref, cn_ref, buf)


def _conv_post(gate_b, gate_c, u, p0, tg, nb, cw_ref, cn_ref, buf):
    dc = cw_ref.shape[-1]
    cu = gate_c * u
    buf[:, CONV_PAD + p0:CONV_PAD + p0 + tg, :] = cu.reshape(nb, tg, dc)
    base = CONV_PAD - (CONV_W - 1) + p0
    y = cw_ref[0:1, :].reshape(1, 1, dc) * buf[:, base:base + tg, :]
    for k in range(1, CONV_W):
        y = y + cw_ref[k:k + 1, :].reshape(1, 1, dc) * buf[:, base + k:base + k + tg, :]
    conv_o = gate_b * y.reshape(nb * tg, dc)
    return (_rms(conv_o) * cn_ref[...]).astype(BF16)


def _conv_carry(buf, st_ref, t):
    last = buf[:, CONV_PAD + t - 2:CONV_PAD + t, :]
    st_ref[...] = last
    buf[:, CONV_PAD - 2:CONV_PAD, :] = last


def _mixer_in_prompt_kernel(x_ref, sh_ref, sc_ref, win_ref, qn_ref, wuq_ref, kvn_ref,
                            cos_ref, s1_ref, s2_ref, wuk_ref, wuv_ref,
                            winb_ref, cw_ref, cn_ref, prev_ref, *rest, nsub, first):
    q_out, k_out, v_out, ckv_out, kpe_out, conv_out, st_ref, buf = rest[-8:]

    @pl.when(pl.program_id(1) == 0)
    def _():
        buf[:, CONV_PAD - 2:CONV_PAD, :] = prev_ref[...]

    if first:
        ckv_out[1:] = jnp.zeros((ckv_out.shape[0] - 1,) + ckv_out.shape[1:], F32)
        kpe_out[1:] = jnp.zeros((kpe_out.shape[0] - 1,) + kpe_out.shape[1:], F32)
        ckv_out, kpe_out = ckv_out.at[0], kpe_out.at[0]
    rows = x_ref.shape[0] // nsub
    lane = lax.broadcasted_iota(jnp.int32, (rows, LANES), 1)
    ones_col = jnp.where(lane == 0, 1.0, 0.0).astype(BF16)
    for j in range(nsub):
        rs = slice(j * rows, (j + 1) * rows)
        h = _adaln(x_ref, rs, sh_ref, sc_ref)
        conv_out[rs, :] = _conv_group(h, j * rows, rows, 1, winb_ref, cw_ref, cn_ref, buf)
        q, ckv, kp, rope = _attn_in_common(h, rs, win_ref, qn_ref, wuq_ref, kvn_ref,
                                           cos_ref, s1_ref, s2_ref)
        ckv_out[rs, :] = ckv
        kpe_out[:, rs] = kp.T[:ROPE_DIM, :]
        ckb = ckv.astype(BF16)
        kn = _dot(ckb, wuk_ref[...])
        v = _dot(ckb, wuv_ref[...]).astype(BF16)
        kpb = kp.astype(BF16)
        for hd in range(N_HEADS):
            c0 = hd * HEAD_W
            v_out[rs, c0:c0 + V_DIM] = v[:, hd * V_DIM:(hd + 1) * V_DIM]
            v_out[rs, c0 + V_DIM:c0 + HEAD_W] = ones_col
            q_out[rs, c0:c0 + NOPE_DIM] = q[:, c0:c0 + NOPE_DIM].astype(BF16)
            q_out[rs, c0 + NOPE_DIM:c0 + HEAD_W] = rope(q[:, c0 + NOPE_DIM:c0 + HEAD_W]).astype(BF16)
            k_out[rs, c0:c0 + NOPE_DIM] = kn[:, hd * NOPE_DIM:(hd + 1) * NOPE_DIM].astype(BF16)
            k_out[rs, c0 + NOPE_DIM:c0 + HEAD_W] = kpb
    _conv_carry(buf, st_ref, x_ref.shape[0])


def _mixer_in_sample_kernel(x_ref, sh_ref, sc_ref, wint_hbm, qn_ref, wuq_ref, kvn_ref,
                            cos_ref, s1_ref, s2_ref, wukv_ref, cw_ref, cn_ref, prev_ref,
                            qa_out, qpe_out, ckv_out, kpe_out, conv_out, st_ref, wa_out, wb_out,
                            wuk_out, wuv_out, wukt_out, wuvh_out, wuq_out,
                            w_buf, w_sem, h_scr, z_scr, buf, *, nb, t, layer):
    s = pl.program_id(0)
    dc = cw_ref.shape[-1]
    n_a = Q_LORA + KV_LORA + ROPE_DIM
    chunk_rows = [(0, IN_A_W)] + [(n_a + j * dc, dc) for j in range(3)]

    def chunk_copy(k):
        r0, rows = chunk_rows[k]
        return pltpu.make_async_copy(wint_hbm.at[layer, pl.ds(r0, rows), :],
                                     w_buf.at[k % 2, pl.ds(0, rows), :], w_sem.at[k % 2])

    for k in range(len(chunk_rows)):
        @pl.when(s == k)
        def _(k=k):
            if k == 0:
                chunk_copy(0).start()
                h_scr[...] = _adaln(x_ref, slice(None), sh_ref, sc_ref)
                buf[:, CONV_PAD - 2:CONV_PAD, :] = prev_ref[...]
            if k + 1 < len(chunk_rows):
                chunk_copy(k + 1).start()
            chunk_copy(k).wait()
            rows = chunk_rows[k][1]
            w_t = w_buf[k % 2, 0:rows, :].T.astype(BF16)
            if k == 0:
                wa_out[...] = w_t
                for hd in range(N_HEADS):
                    c0 = hd * HEAD_W
                    wuq_out[:, c0:c0 + QK_DIM] = wuq_ref[:, hd * QK_DIM:(hd + 1) * QK_DIM].astype(BF16)
                    wuq_out[:, c0 + QK_DIM:c0 + HEAD_W] = jnp.zeros(
                        (wuq_ref.shape[0], HEAD_W - QK_DIM), BF16)
                q, ckv, kp, rope = _attn_in_common(h_scr[...], slice(None), wa_out, qn_ref,
                                                   wuq_out, kvn_ref, cos_ref, s1_ref, s2_ref)
                ckv_out[...] = ckv
                kpe_out[...] = kp[:, :ROPE_DIM]
                for hd in range(N_HEADS):
                    k0 = hd * (NOPE_DIM + V_DIM)
                    w_k = wukv_ref[:, k0:k0 + NOPE_DIM]
                    w_v = wukv_ref[:, k0 + NOPE_DIM:k0 + NOPE_DIM + V_DIM].astype(BF16)
                    wuk_out[:, hd * NOPE_DIM:(hd + 1) * NOPE_DIM] = w_k.astype(BF16)
                    wukt_out[hd] = w_k.T.astype(BF16)
                    wuv_out[:, hd * V_DIM:(hd + 1) * V_DIM] = w_v
                    wuvh_out[hd] = w_v
                for hd in range(N_HEADS):
                    c0 = hd * HEAD_W
                    qa = _dot(q[:, c0:c0 + NOPE_DIM].astype(BF16), wukt_out[hd])
                    qa_out[:, hd] = qa.reshape(nb, t, KV_LORA).astype(BF16)
                    qp = rope(q[:, c0 + NOPE_DIM:c0 + HEAD_W])
                    qpe_out[:, hd] = qp.reshape(nb, t, ROPE_W).astype(BF16)
            else:
                wb_out[...] = w_t
                z_scr[k - 1] = _dot(h_scr[...], wb_out[...])
            if k == len(chunk_rows) - 1:
                conv_out[...] = _conv_post(z_scr[0], z_scr[1], z_scr[2], 0, t, nb,
                                           cw_ref, cn_ref, buf)
                _conv_carry(buf, st_ref, t)


def _mixer_in_prompt(x, mod, w, w_a, w_b, w_uk, w_uv, w_uq, layer, tabs, prev, nbatch, seq,
                     stacks, tm, nsub):
    m, d = x.shape
    depth = w["q_norm"].shape[0]
    dc = w["conv_w"].shape[-1]
    nt = seq // tm
    tab_spec = pl.BlockSpec((tm, ROPE_W), lambda b, s: (s, 0))
    whole = lambda a: pl.BlockSpec(a.shape, lambda b, s: (0,) * a.ndim,
                                   pipeline_mode=pl.Buffered(1))
    ins = [x, mod.arr, mod.arr, w_a, w["q_norm"], w_uq, w["kv_norm"],
           *tabs, w_uk, w_uv, w_b, w["conv_w"], w["conv_out_norm"], prev]
    specs = [_rows(tm, d, nt), mod.spec(0), mod.spec(1), whole(w_a),
             _layer_spec(w["q_norm"], layer), whole(w_uq),
             _layer_spec(w["kv_norm"], layer), tab_spec, tab_spec, tab_spec,
             whole(w_uk), whole(w_uv),
             whole(w_b), _layer_spec(w["conv_w"], layer),
             _layer_spec(w["conv_out_norm"], layer),
             pl.BlockSpec((None, 1, CONV_W - 1, dc), lambda b, s: (0, b, 0, 0))]
    aliases = {}
    if stacks is not None:
        aliases = {len(ins): 3, len(ins) + 1: 4}
        ins += list(stacks)
        specs += [pl.BlockSpec(memory_space=pl.ANY)] * 2
    outs = [jax.ShapeDtypeStruct((m, N_HEADS * HEAD_W), BF16),
            jax.ShapeDtypeStruct((m, N_HEADS * HEAD_W), BF16),
            jax.ShapeDtypeStruct((m, N_HEADS * HEAD_W), BF16),
            jax.ShapeDtypeStruct((depth, m, KV_LORA), F32),
            jax.ShapeDtypeStruct((depth, nbatch, ROPE_DIM, seq), F32),
            jax.ShapeDtypeStruct((m, dc), BF16),
            jax.ShapeDtypeStruct((nbatch, CONV_W - 1, dc), F32)]
    first = stacks is None
    lsel = (depth, 0) if first else (None, layer)
    out_specs = [_rows(tm, N_HEADS * HEAD_W, nt), _rows(tm, N_HEADS * HEAD_W, nt),
                 _rows(tm, N_HEADS * HEAD_W, nt),
                 pl.BlockSpec((lsel[0], tm, KV_LORA), lambda b, s: (lsel[1], b * nt + s, 0)),
                 pl.BlockSpec((lsel[0], None, ROPE_DIM, tm), lambda b, s: (lsel[1], b, 0, s)),
                 _rows(tm, dc, nt),
                 pl.BlockSpec((1, CONV_W - 1, dc), lambda b, s: (b, 0, 0))]
    return pl.pallas_call(
        functools.partial(_mixer_in_prompt_kernel, nsub=nsub, first=first), grid=(nbatch, nt),
        in_specs=specs, out_specs=out_specs, out_shape=outs, input_output_aliases=aliases,
        scratch_shapes=[pltpu.VMEM((1, CONV_PAD + tm, dc), F32)],
        compiler_params=_cparams(2), name="mixer_in_prompt",
    )(*ins)


def _mixer_in_sample(x, mod, w, w_in_t, layer, tabs, prev, nb, t):
    m, d = x.shape
    dc = w["conv_w"].shape[-1]
    steps = 4
    assert w_in_t.shape[1] == Q_LORA + KV_LORA + ROPE_DIM + 3 * dc
    const = lambda a: pl.BlockSpec(a.shape, lambda s: (0,) * a.ndim)
    ins = [x, mod.arr, mod.arr, w_in_t, w["q_norm"], w["w_uq"], w["kv_norm"],
           *tabs, w["w_ukv"], w["conv_w"], w["conv_out_norm"], prev]
    specs = [const(x), mod.spec(0), mod.spec(1), pl.BlockSpec(memory_space=pl.ANY),
             _layer_spec(w["q_norm"], layer), _layer_spec(w["w_uq"], layer),
             _layer_spec(w["kv_norm"], layer), const(tabs[0]), const(tabs[1]), const(tabs[2]),
             _layer_spec(w["w_ukv"], layer), _layer_spec(w["conv_w"], layer),
             _layer_spec(w["conv_out_norm"], layer),
             pl.BlockSpec((None, nb, CONV_W - 1, dc), lambda s: (layer, 0, 0, 0))]
    outs = [jax.ShapeDtypeStruct((nb, N_HEADS, t, KV_LORA), BF16),
            jax.ShapeDtypeStruct((nb, N_HEADS, t, ROPE_W), BF16),
            jax.ShapeDtypeStruct((m, KV_LORA), F32),
            jax.ShapeDtypeStruct((m, ROPE_DIM), F32),
            jax.ShapeDtypeStruct((m, dc), BF16),
            jax.ShapeDtypeStruct((nb, CONV_W - 1, dc), F32),
            jax.ShapeDtypeStruct((d, IN_A_W), BF16),
            jax.ShapeDtypeStruct((d, 3 * dc), BF16),
            jax.ShapeDtypeStruct((KV_LORA, N_HEADS * NOPE_DIM), BF16),
            jax.ShapeDtypeStruct((KV_LORA, N_HEADS * V_DIM), BF16),
            jax.ShapeDtypeStruct((N_HEADS, NOPE_DIM, KV_LORA), BF16),
            jax.ShapeDtypeStruct((N_HEADS, KV_LORA, V_DIM), BF16),
            jax.ShapeDtypeStruct((Q_LORA, N_HEADS * HEAD_W), BF16)]
    out_specs = [const(o) for o in outs[:7]]
    out_specs.append(pl.BlockSpec((d, dc), lambda s: (0, jnp.maximum(s - 1, 0))))
    out_specs += [const(o) for o in outs[8:]]
    return pl.pallas_call(
        functools.partial(_mixer_in_sample_kernel, nb=nb, t=t, layer=layer), grid=(steps,),
        in_specs=specs, out_specs=out_specs, out_shape=outs,
        scratch_shapes=[pltpu.VMEM((2, IN_A_W, d), F32), pltpu.SemaphoreType.DMA((2,)),
                        pltpu.VMEM((m, d), BF16), pltpu.VMEM((3, m, dc), F32),
                        pltpu.VMEM((nb, CONV_PAD + t, dc), F32)],
        compiler_params=_cparams(1), name="mixer_in_sample",
    )(*ins)


def _prompt_attn_kernel(q_ref, k_ref, v_ref, o_ref, s_buf, p_buf, *, seq, tq, heads):
    row_chunk = lax.broadcasted_iota(jnp.int32, (tq, tq), 0) // CHUNK
    col_chunk = lax.broadcasted_iota(jnp.int32, (tq, tq), 1) // CHUNK
    visible = col_chunk <= row_chunk
    tiles = [(hd, qi) for hd in range(heads) for qi in range(seq // tq)]

    def scores(i):
        hd, qi = tiles[i]
        hi = (qi + 1) * tq
        cols = slice(hd * HEAD_W, (hd + 1) * HEAD_W)
        s_buf[i % 2, :, 0:hi] = _dot_t(q_ref[qi * tq:hi, cols], k_ref[0:hi, cols])

    scores(0)
    for i, (hd, qi) in enumerate(tiles):
        lo, hi = qi * tq, (qi + 1) * tq
        if i + 1 < len(tiles):
            scores(i + 1)
        s_d = jnp.where(visible, s_buf[i % 2, :, lo:hi], NEG_BIG)
        m = jnp.max(s_d, axis=-1, keepdims=True)
        if qi > 0:
            m = jnp.maximum(m, jnp.max(s_buf[i % 2, :, 0:lo], axis=-1, keepdims=True))
            p_buf[:, 0:lo] = jnp.exp2(s_buf[i % 2, :, 0:lo] - m).astype(BF16)
        p_buf[:, lo:hi] = jnp.exp2(s_d - m).astype(BF16)
        oe = _dot(p_buf[:, 0:hi], v_ref[0:hi, hd * HEAD_W:(hd + 1) * HEAD_W])
        o_ref[lo:hi, hd * V_DIM:(hd + 1) * V_DIM] = oe[:, :V_DIM] / oe[:, V_DIM:V_DIM + 1]


def _prompt_attention(q, k, v, nbatch, seq, tq=256, heads=4):
    m = q.shape[0]
    qkv_spec = pl.BlockSpec((seq, heads * HEAD_W), lambda b, h: (b, h))
    return pl.pallas_call(
        functools.partial(_prompt_attn_kernel, seq=seq, tq=tq, heads=heads),
        grid=(nbatch, N_HEADS // heads),
        in_specs=[qkv_spec, qkv_spec, qkv_spec],
        out_specs=pl.BlockSpec((seq, heads * V_DIM), lambda b, h: (b, h)),
        out_shape=jax.ShapeDtypeStruct((m, D_ATTN), F32),
        scratch_shapes=[pltpu.VMEM((2, tq, seq), F32), pltpu.VMEM((tq, seq), BF16)],
        compiler_params=_cparams(2), name="prompt_attention",
    )(q, k, v)


def _sample_attn_kernel(qa_ref, qpe_ref, pc_hbm, pk_ref, nc_ref, nk_ref, wuv_ref, o_ref,
                        c_buf, c_sem, *, t, layer, nb):
    b = pl.program_id(0)

    def cache_copy(k):
        slot = lax.rem(k, CACHE_BUFS)
        return pltpu.make_async_copy(pc_hbm.at[layer, k], c_buf.at[slot], c_sem.at[slot])

    @pl.when(b == 0)
    def _():
        for k in range(min(CACHE_BUFS - 1, nb)):
            cache_copy(k).start()

    @pl.when(b + CACHE_BUFS - 1 < nb)
    def _():
        cache_copy(b + CACHE_BUFS - 1).start()

    cache_copy(b).wait()
    rows = N_HEADS * t
    qa = qa_ref[0].reshape(rows, KV_LORA)
    qp = qpe_ref[0].reshape(rows, ROPE_W)[:, :ROPE_DIM]
    pc = c_buf[lax.rem(b, CACHE_BUFS)].astype(BF16)
    pk_t = pk_ref[0].astype(BF16)
    nc = nc_ref[...].astype(BF16)
    nk = nk_ref[...].astype(BF16)
    s_p = _dot_t(qa, pc) + _dot(qp, pk_t)
    s_n = _dot_t(qa, nc) + _dot_t(qp, nk)
    m = jnp.maximum(jnp.max(s_p, axis=-1, keepdims=True), jnp.max(s_n, axis=-1, keepdims=True))
    p_p = jnp.exp2(s_p - m)
    p_n = jnp.exp2(s_n - m)
    l = jnp.sum(p_p, axis=-1, keepdims=True) + jnp.sum(p_n, axis=-1, keepdims=True)
    o_lat = (_dot(p_p.astype(BF16), pc) + _dot(p_n.astype(BF16), nc)) / l
    o_lat = o_lat.astype(BF16)
    for hd in range(N_HEADS):
        o_ref[:, hd * V_DIM:(hd + 1) * V_DIM] = _dot(o_lat[hd * t:(hd + 1) * t, :], wuv_ref[hd])


def _sample_attention(qa, qpe, past_ckv, past_kpe_t, new_ckv, new_kpe, w_uv_h, layer):
    nb, _, t, _ = qa.shape
    past = past_ckv.shape[2]
    return pl.pallas_call(
        functools.partial(_sample_attn_kernel, t=t, layer=layer, nb=nb),
        grid=(nb,),
        in_specs=[pl.BlockSpec((1, N_HEADS, t, KV_LORA), lambda b: (b, 0, 0, 0)),
                  pl.BlockSpec((1, N_HEADS, t, ROPE_W), lambda b: (b, 0, 0, 0)),
                  pl.BlockSpec(memory_space=pl.ANY),
                  pl.BlockSpec((None, 1, ROPE_DIM, past), lambda b: (layer, b, 0, 0)),
                  pl.BlockSpec((t, KV_LORA), lambda b: (b, 0)),
                  pl.BlockSpec((t, ROPE_DIM), lambda b: (b, 0)),
                  pl.BlockSpec(w_uv_h.shape, lambda b: (0, 0, 0))],
        out_specs=pl.BlockSpec((t, D_ATTN), lambda b: (b, 0)),
        out_shape=jax.ShapeDtypeStruct((nb * t, D_ATTN), F32),
        scratch_shapes=[pltpu.VMEM((CACHE_BUFS, past, KV_LORA), F32),
                        pltpu.SemaphoreType.DMA((CACHE_BUFS,))],
        compiler_params=_cparams(1), name="sample_attention",
    )(qa, qpe, past_ckv, past_kpe_t, new_ckv, new_kpe, w_uv_h)


def _out_proj_kernel(x_ref, attn_ref, conv_ref, g_ref, shm_ref, scm_ref, an_ref, woa_ref, woc_ref,
                     o_ref, h_ref, *bf16_w_out, groups):
    if bf16_w_out:
        wo_ref, = bf16_w_out
        da = woa_ref.shape[0]
        wo_ref[0:da, :] = woa_ref[...].astype(BF16)
        wo_ref[da:, :] = woc_ref[...].astype(BF16)
        woa, woc = wo_ref[0:da, :], wo_ref[da:, :]
    else:
        woa, woc = woa_ref[...], woc_ref[...]
    starts = [sum(groups[:j]) for j in range(len(groups))]
    for r0, rows in zip(starts, groups):
        rs = slice(r0, r0 + rows) if len(groups) > 1 else slice(None)
        a = (_rms(attn_ref[rs, :]) * an_ref[...]).astype(BF16)
        mix = _dot(a, woa) + _dot(conv_ref[rs, :], woc)
        y = x_ref[rs, :] + _mod_rows(g_ref, rows) * mix
        o_ref[rs, :] = y
        h_ref[rs, :] = (_rms(y) * (1.0 + _mod_rows(scm_ref, rows))
                        + _mod_rows(shm_ref, rows)).astype(BF16)


def _out_proj(x, attn, conv_n, mod, w, w_out, layer, nbatch, seq, tm, groups=None):
    groups = (tm,) if groups is None else tuple(groups)
    assert sum(groups) == tm
    m, d = x.shape
    nt = seq // tm
    da, dc = attn.shape[1], conv_n.shape[1]
    ins = [x, attn, conv_n, mod.arr, mod.arr, mod.arr, w["attn_out_norm"], w_out, w_out]
    specs = [_rows(tm, d, nt), _rows(tm, da, nt), _rows(tm, dc, nt),
             mod.spec(2), mod.spec(3), mod.spec(4),
             _layer_spec(w["attn_out_norm"], mod.layer)]
    out_specs = [_rows(tm, d, nt), _rows(tm, d, nt)]
    out_shape = [jax.ShapeDtypeStruct((m, d), F32), jax.ShapeDtypeStruct((m, d), BF16)]
    if layer is None:
        specs += [pl.BlockSpec((da, d), lambda b, s: (0, 0), pipeline_mode=pl.Buffered(1)),
                  pl.BlockSpec((dc, d), lambda b, s: (da // dc, 0), pipeline_mode=pl.Buffered(1))]
    else:
        assert nbatch * nt == 1
        specs += [_layer_spec(w_out, layer, (da, d), (0, 0)),
                  _layer_spec(w_out, layer, (dc, d), (da // dc, 0))]
        out_specs.append(pl.BlockSpec((da + dc, d), lambda b, s: (0, 0)))
        out_shape.append(jax.ShapeDtypeStruct((da + dc, d), BF16))
    return pl.pallas_call(
        functools.partial(_out_proj_kernel, groups=groups), grid=(nbatch, nt), in_specs=specs,
        out_specs=out_specs, out_shape=out_shape,
        compiler_params=_cparams(2), name="out_proj",
    )(*ins)


def _mlp_kernel(x_ref, h_ref, g_ref, wup_ref, wdn_ref, fn_ref, o_ref, *extra, nf, final, cast,
                x_rows):
    f = pl.program_id(2)
    if cast:
        wu_ref, wd_ref = extra
        wu_ref[...] = wup_ref[...].astype(BF16)
        wd_ref[...] = wdn_ref[...].astype(BF16)
    else:
        wu_ref, wd_ref = wup_ref, wdn_ref
    if x_rows:
        x_buf, x_sem = extra
        tile = pl.program_id(0) * pl.num_programs(1) + pl.program_id(1)
        row0 = pl.multiple_of(tile * x_rows, x_rows)
        x_copy = pltpu.make_async_copy(x_ref.at[pl.ds(row0, x_rows), :], x_buf, x_sem)

        @pl.when(f == 0)
        def _():
            x_copy.start()

    def partial_down():
        up = _dot(h_ref[...], wu_ref[...])
        act = jnp.square(jnp.maximum(up, 0.0)).astype(BF16)
        return _dot(act, wd_ref[...])

    @pl.when(f == 0)
    def _():
        o_ref[...] = partial_down()

    @pl.when(f > 0)
    def _():
        o_ref[...] += partial_down()

    @pl.when(f == nf - 1)
    def _():
        if x_rows:
            x_copy.wait()
            x = x_buf[...]
        else:
            x = x_ref[...]
        y = x + _mod_rows(g_ref, x.shape[0]) * o_ref[...]
        if final:
            y = _rms(y) * fn_ref[...]
        o_ref[...] = y


def _mlp(x, h, mod, w_up, w_down, layer, final_norm, nbatch, seq, tm, tf, final):
    m, d = x.shape
    nt = seq // tm
    dff = w_up.shape[-1]
    nf = dff // tf
    cast = layer is not None
    out_specs = [_rows(tm, d, nt)]
    out_shape = [jax.ShapeDtypeStruct((m, d), F32)]
    scratch = []
    if cast:
        assert nbatch * nt == 1
        x_spec = _rows(tm, d, nt)
        w_specs = [pl.BlockSpec((None, d, tf), lambda b, s, f: (layer, 0, f)),
                   pl.BlockSpec((None, tf, d), lambda b, s, f: (layer, f, 0))]
        out_specs += [pl.BlockSpec((d, tf), lambda b, s, f: (0, f)),
                      pl.BlockSpec((tf, d), lambda b, s, f: (f, 0))]
        out_shape += [jax.ShapeDtypeStruct((d, dff), BF16), jax.ShapeDtypeStruct((dff, d), BF16)]
    else:
        x_spec = pl.BlockSpec(memory_space=pl.ANY)
        w_specs = [pl.BlockSpec((d, tf), lambda b, s, f: (0, f)),
                   pl.BlockSpec((tf, d), lambda b, s, f: (f, 0))]
        scratch = [pltpu.VMEM((tm, d), F32), pltpu.SemaphoreType.DMA(())]
    return pl.pallas_call(
        functools.partial(_mlp_kernel, nf=nf, final=final, cast=cast,
                          x_rows=0 if cast else tm),
        grid=(nbatch, nt, nf),
        in_specs=[x_spec, _rows(tm, d, nt), mod.spec(5), *w_specs,
                  pl.BlockSpec(final_norm.shape, lambda b, s, f: (0, 0))],
        out_specs=out_specs, out_shape=out_shape, scratch_shapes=scratch,
        compiler_params=_cparams(3), name="mlp",
    )(x, h, mod.arr, w_up, w_down, final_norm)


def _rope_tables(pos):
    half = ROPE_DIM // 2
    inv = ROPE_BASE ** (-jnp.arange(half, dtype=F32) / half)
    ang = pos.astype(F32)[:, None] * inv[None, :]
    cos, sin, zero = jnp.cos(ang), jnp.sin(ang), jnp.zeros_like(ang)
    return (jnp.concatenate([cos, cos, zero, zero], axis=-1),
            jnp.concatenate([-sin, zero, zero, zero], axis=-1),
            jnp.concatenate([zero, sin, zero, zero], axis=-1))


def _prep_weights(q_norm, w_uq, kv_norm, w_ukv, conv_w, attn_out_norm, conv_out_norm,
                  w_out, w_up, w_down):
    return dict(
        w_uq=w_uq, w_ukv=w_ukv,
        q_norm=q_norm[:, None, :], kv_norm=kv_norm[:, None, :],
        conv_w=conv_w, attn_out_norm=attn_out_norm[:, None, :],
        conv_out_norm=conv_out_norm[:, None, :],
        w_out=w_out, w_up=w_up, w_down=w_down)


def kernel(x_prompt, x_sample, c_prompt, c_sample, cache_kv_latent, cache_k_rope, state_conv,
           w_ada, b_ada, w_in, q_norm, w_uq, kv_norm, w_ukv, conv_w,
           attn_out_norm, conv_out_norm, w_out, w_up, w_down, final_norm):
    bp, sp, d = x_prompt.shape
    bs, ss, _ = x_sample.shape
    depth = w_ada.shape[0]
    past = cache_kv_latent.shape[2]
    dc = conv_w.shape[-1]
    ms = bs * ss

    w = _prep_weights(q_norm, w_uq, kv_norm, w_ukv, conv_w, attn_out_norm,
                      conv_out_norm, w_out, w_up, w_down)
    w_in_t = jnp.swapaxes(w_in, 1, 2)
    fnorm = final_norm[None, :]
    tabs_p = _rope_tables(jnp.arange(sp))
    tabs_s = tuple(jnp.tile(tb, (bs, 1)) for tb in _rope_tables(past + jnp.arange(ss)))

    mod = _modulation(jnp.concatenate([c_prompt, c_sample], axis=0), w_ada, b_ada)
    mod = mod.reshape(depth, bp + bs, N_MOD, d)
    mod_p = mod[:, :bp].reshape(depth * bp * N_MOD, 1, d)
    mod_s = jnp.swapaxes(mod[:, bp:], 1, 2).reshape(depth * N_MOD, bs, d)
    zero_conv = jnp.zeros((1, bp, CONV_W - 1, dc), F32)

    xp = x_prompt.reshape(bp * sp, d)
    xs = x_sample.reshape(ms, d)
    past_kpe_t = jnp.swapaxes(cache_k_rope, 2, 3)
    stacks = None
    outs = [[] for _ in range(4)]
    for l in range(depth):
        last = l == depth - 1
        mp = _Mod(mod_p, l, bp)
        msm = _Mod(mod_s, l, 0)

        (qa, qpe, ckv_s, kpe_s, conv_ns, cstate_s, w_a_l, w_b_l, w_uk_l, w_uv_l, _,
         w_uv_h_l, w_uq_l) = _mixer_in_sample(xs, msm, w, w_in_t, l, tabs_s, state_conv, bs, ss)
        attn_s = _sample_attention(qa, qpe, cache_kv_latent, past_kpe_t, ckv_s, kpe_s,
                                   w_uv_h_l, l)
        xs, hs, w_out_l = _out_proj(xs, attn_s, conv_ns, msm, w, w["w_out"], l, 1, ms, tm=ms)
        xs, w_up_l, w_down_l = _mlp(xs, hs, msm, w["w_up"], w["w_down"], l, fnorm, 1, ms,
                                    tm=ms, tf=1024, final=last)
        outs[1].append(ckv_s.reshape(bs, ss, KV_LORA))
        outs[2].append(kpe_s.reshape(bs, ss, ROPE_DIM))
        outs[3].append(cstate_s)

        q, k, v, *stacks, conv_n, cstate = _mixer_in_prompt(
            xp, mp, w, w_a_l, w_b_l, w_uk_l, w_uv_l, w_uq_l, l, tabs_p, zero_conv, bp, sp,
            stacks, tm=256, nsub=1)
        attn = _prompt_attention(q, k, v, bp, sp)
        xp, hp = _out_proj(xp, attn, conv_n, mp, w, w_out_l, None, bp, sp, tm=512, groups=(256, 128, 128))
        xp, = _mlp(xp, hp, mp, w_up_l, w_down_l, None, fnorm, bp, sp, tm=1024, tf=1024, final=last)
        outs[0].append(cstate)

    return (xp.reshape(bp, sp, d), xs.reshape(bs, ss, d),
            stacks[0].reshape(depth, bp, sp, KV_LORA), jnp.swapaxes(stacks[1], 2, 3),
            jnp.stack(outs[0]), jnp.stack(outs[1]), jnp.stack(outs[2]), jnp.stack(outs[3]))
```
